```python
import math
import jax, jax.numpy as jnp
from jax import lax
import numpy as np

D_MODEL = 4096
BATCH = 4
SEQ = 2048
DEPTH = 2
DEC_BATCH = 128
DEC_SEQ = 1
PAST_LEN = 16384
PAGE_SIZE = 128

C_CONV = D_MODEL // 2
C_LRU = D_MODEL // 2
D_MIX = C_CONV + C_LRU
D_IN = 2 * C_CONV + 2 * C_LRU
CONV_GROUPS = 16
CONV_K = 31
LRU_HEADS = 16
LRU_HD = C_LRU // LRU_HEADS
LRU_CONV_K = 4
RG_C = 8.0
MEM_LEN = 256
MEM_HEADS = 4
MEM_HD = D_MODEL // 16
MEM_W = MEM_HEADS * MEM_HD
N_GROUPS = 4
EXPERTS_PER_GROUP = 8
N_EXPERTS = N_GROUPS * EXPERTS_PER_GROUP
TOP_K = 2
D_EXPERT = D_MODEL // 4
MOE_BLOCK = 128
EPS = 1e-6

kernel_name = 'hybrid_conv_rglru_hmoe_step'

F32 = jnp.float32


def rmsnorm(x, g):
    xf = x.astype(F32)
    y = xf * lax.rsqrt(jnp.mean(xf * xf, axis=-1, keepdims=True) + EPS)
    return (y * g.astype(F32)).astype(x.dtype)


def causal_depthwise(x, buf, w, b):
    xx = jnp.concatenate([buf.astype(x.dtype), x], axis=1)
    y = lax.conv_general_dilated(xx, w[:, None, :].astype(x.dtype), window_strides=(1,), padding='VALID',
                                 dimension_numbers=('NWC', 'WIO', 'NWC'), feature_group_count=x.shape[-1])
    return y + b.astype(x.dtype), xx[:, -(w.shape[0] - 1):]


def conv_module(u_val, u_gate, buf, p):
    u = u_val * jax.nn.sigmoid(u_gate)
    y, new_buf = causal_depthwise(u, buf, p['conv_w'], p['conv_b'])
    B, T, C = y.shape
    yf = y.astype(F32).reshape(B, T, CONV_GROUPS, C // CONV_GROUPS)
    mu = jnp.mean(yf, axis=-1, keepdims=True)
    var = jnp.mean(jnp.square(yf - mu), axis=-1, keepdims=True)
    yn = ((yf - mu) * lax.rsqrt(var + EPS)).reshape(B, T, C) * p['conv_gn_g'].astype(F32) + p['conv_gn_b'].astype(F32)
    return jax.nn.silu(yn).astype(u_val.dtype), new_buf


def _lin_combine(l, r):
    a_l, b_l = l
    a_r, b_r = r
    return a_l * a_r, a_r * b_l + b_r


def rglru_block(xb, gate, buf, h0, p):
    xc, new_buf = causal_depthwise(xb, buf, p['lru_conv_w'], p['lru_conv_b'])
    B, T, C = xc.shape
    xf = xc.astype(F32)
    xh = xf.reshape(B, T, LRU_HEADS, LRU_HD)
    r = jax.nn.sigmoid(jnp.einsum('bthi,hij->bthj', xh, p['lru_wa'].astype(F32)).reshape(B, T, C) + p['lru_ba'].astype(F32))
    i = jax.nn.sigmoid(jnp.einsum('bthi,hij->bthj', xh, p['lru_wi'].astype(F32)).reshape(B, T, C) + p['lru_bi'].astype(F32))
    log_a = RG_C * r * jax.nn.log_sigmoid(p['lru_lambda'].astype(F32))
    a = jnp.exp(log_a)
    bterm = jnp.sqrt(-jnp.expm1(2.0 * log_a)) * (i * xf)
    bterm = bterm.at[:, 0].add(a[:, 0] * h0.astype(F32))
    _, h = lax.associative_scan(_lin_combine, (a, bterm), axis=1)
    y = h.astype(xb.dtype) * jax.nn.gelu(gate)
    return y, new_buf, h[:, -1].astype(xb.dtype)


def mixer(xn, conv_buf, lru_buf, h0, p):
    z = xn @ p['w_in']
    cv, cg, lx, lg = jnp.split(z, [C_CONV, 2 * C_CONV, 2 * C_CONV + C_LRU], axis=-1)
    yc, conv_buf = conv_module(cv, cg, conv_buf, p)
    yl, lru_buf, h = rglru_block(lx, lg, lru_buf, h0, p)
    return jnp.concatenate([yc, yl], axis=-1) @ p['w_out'], conv_buf, lru_buf, h


def mem_kv(mem, p):
    B, M, _ = mem.shape
    mn = rmsnorm(mem, p['norm_mem_kv'])
    k = (mn @ p['w_k']).reshape(B, M, MEM_HEADS, MEM_HD)
    v = (mn @ p['w_v']).reshape(B, M, MEM_HEADS, MEM_HD)
    return k, v


def mem_attention(xn, k, v, p):
    B, T, _ = xn.shape
    q = (xn @ p['w_q']).reshape(B, T, MEM_HEADS, MEM_HD)
    s = jnp.einsum('bthd,bmhd->bhtm', q.astype(F32), k.astype(F32)) * (MEM_HD ** -0.5)
    pr = jax.nn.softmax(s, axis=-1)
    o = jnp.einsum('bhtm,bmhd->bthd', pr, v.astype(F32)).astype(xn.dtype).reshape(B, T, MEM_W)
    return o @ p['w_o']


def hier_moe(xn, p):
    B, T, D = xn.shape
    x2d = xn.reshape(B * T, D)
    n_tok = B * T
    xf = x2d.astype(F32)
    g_logits = xf @ p['w_router_g'].astype(F32) + p['b_router_g'].astype(F32)
    g_prob = jax.nn.softmax(g_logits, axis=-1)
    g_sel = jnp.argmax(g_logits, axis=-1)
    p_g = jnp.take_along_axis(g_prob, g_sel[:, None], axis=1)[:, 0]
    e_all = jnp.einsum('nd,dge->nge', xf, p['w_router_e'].astype(F32)) + p['b_router_e'].astype(F32)
    e_logits = jnp.take_along_axis(e_all, g_sel[:, None, None], axis=1)[:, 0]
    top_v, top_i = lax.top_k(e_logits, TOP_K)
    gates = p_g[:, None] * jax.nn.softmax(top_v, axis=-1)
    expert_id = (g_sel[:, None] * EXPERTS_PER_GROUP + top_i).astype(jnp.int32)
    n_slot = n_tok * TOP_K
    flat_e = expert_id.reshape(-1)
    flat_tok = jnp.repeat(jnp.arange(n_tok, dtype=jnp.int32), TOP_K)
    flat_w = gates.reshape(-1)
    order = jnp.argsort(flat_e)
    sorted_e = flat_e[order]
    counts = jnp.bincount(flat_e, length=N_EXPERTS)
    padded = ((counts + MOE_BLOCK - 1) // MOE_BLOCK) * MOE_BLOCK
    start = jnp.cumsum(counts) - counts
    pend = jnp.cumsum(padded)
    pstart = pend - padded
    dest = pstart[sorted_e] + (jnp.arange(n_slot, dtype=jnp.int32) - start[sorted_e])
    n_blocks = -(-(n_slot + N_EXPERTS * (MOE_BLOCK - 1)) // MOE_BLOCK)
    n_rows = n_blocks * MOE_BLOCK
    buf_tok = jnp.zeros((n_rows,), jnp.int32).at[dest].set(flat_tok[order])
    buf_w = jnp.zeros((n_rows,), x2d.dtype).at[dest].set(flat_w[order].astype(x2d.dtype))
    block_start = jnp.arange(n_blocks, dtype=jnp.int32) * MOE_BLOCK
    block_e = jnp.minimum(jnp.sum(pend[None, :] <= block_start[:, None], axis=1), N_EXPERTS - 1)
    w_gate, w_up, w_down = p['w_gate'], p['w_up'], p['w_down']

    def expert_block(args):
        tok, w, e = args
        xb = x2d[tok]
        h = jax.nn.silu(xb @ w_gate[e]) * (xb @ w_up[e])
        return (h @ w_down[e]) * w[:, None]

    out = lax.map(expert_block, (buf_tok.reshape(n_blocks, MOE_BLOCK), buf_w.reshape(n_blocks, MOE_BLOCK), block_e))
    y = jnp.zeros_like(x2d).at[buf_tok].add(out.reshape(n_rows, D))
    return y.reshape(B, T, D)


def layer(x, conv_buf, lru_buf, h0, k, v, p):
    m, conv_buf, lru_buf, h = mixer(rmsnorm(x, p['norm_mix']), conv_buf, lru_buf, h0, p)
    x = x + m
    x = x + mem_attention(rmsnorm(x, p['norm_attn']), k, v, p)
    x = x + hier_moe(rmsnorm(x, p['norm_ffn']), p)
    return x, conv_buf, lru_buf, h


def setup_inputs(seed: int = 0) -> dict:
    key = jax.random.key(seed)
    ks = iter(jax.random.split(key, 48))

    def nrm(shape, scale):
        return jax.random.normal(next(ks), shape, F32) * scale

    def gain(shape):
        return 1.0 + nrm(shape, 0.01)

    u = jax.random.uniform(next(ks), (DEPTH, C_LRU), F32, minval=0.9, maxval=0.999)
    s = u ** (1.0 / RG_C)
    lam = jnp.log(s) - jnp.log1p(-s)
    return {
        'x_prompt': nrm((BATCH, SEQ, D_MODEL), 1.0),
        'x_sample': nrm((DEC_BATCH, DEC_SEQ, D_MODEL), 1.0),
        'mem_prompt': nrm((BATCH, MEM_LEN, D_MODEL), 1.0),
        'state_conv': nrm((DEPTH, DEC_BATCH, CONV_K - 1, C_CONV), 0.5),
        'state_lru_conv': nrm((DEPTH, DEC_BATCH, LRU_CONV_K - 1, C_LRU), 1.0),
        'state_lru_h': nrm((DEPTH, DEC_BATCH, C_LRU), 0.5),
        'cache_mem_k': nrm((DEPTH, DEC_BATCH, MEM_LEN, MEM_HEADS, MEM_HD), 1.0),
        'cache_mem_v': nrm((DEPTH, DEC_BATCH, MEM_LEN, MEM_HEADS, MEM_HD), 1.0),
        'norm_mix': gain((DEPTH, D_MODEL)),
        'w_in': nrm((DEPTH, D_MODEL, D_IN), D_MODEL ** -0.5),
        'conv_w': nrm((DEPTH, CONV_K, C_CONV), CONV_K ** -0.5),
        'conv_b': nrm((DEPTH, C_CONV), 0.01),
        'conv_gn_g': gain((DEPTH, C_CONV)),
        'conv_gn_b': nrm((DEPTH, C_CONV), 0.01),
        'lru_conv_w': nrm((DEPTH, LRU_CONV_K, C_LRU), LRU_CONV_K ** -0.5),
        'lru_conv_b': nrm((DEPTH, C_LRU), 0.01),
        'lru_wa': nrm((DEPTH, LRU_HEADS, LRU_HD, LRU_HD), LRU_HD ** -0.5),
        'lru_ba': nrm((DEPTH, C_LRU), 0.01),
        'lru_wi': nrm((DEPTH, LRU_HEADS, LRU_HD, LRU_HD), LRU_HD ** -0.5),
        'lru_bi': nrm((DEPTH, C_LRU), 0.01),
        'lru_lambda': lam,
        'w_out': nrm((DEPTH, D_MIX, D_MODEL), D_MIX ** -0.5),
        'norm_attn': gain((DEPTH, D_MODEL)),
        'norm_mem_kv': gain((DEPTH, D_MODEL)),
        'w_q': nrm((DEPTH, D_MODEL, MEM_W), D_MODEL ** -0.5),
        'w_k': nrm((DEPTH, D_MODEL, MEM_W), D_MODEL ** -0.5),
        'w_v': nrm((DEPTH, D_MODEL, MEM_W), D_MODEL ** -0.5),
        'w_o': nrm((DEPTH, MEM_W, D_MODEL), MEM_W ** -0.5),
        'norm_ffn': gain((DEPTH, D_MODEL)),
        'w_router_g': nrm((DEPTH, D_MODEL, N_GROUPS), D_MODEL ** -0.5),
        'b_router_g': nrm((DEPTH, N_GROUPS), 0.01),
        'w_router_e': nrm((DEPTH, D_MODEL, N_GROUPS, EXPERTS_PER_GROUP), D_MODEL ** -0.5),
        'b_router_e': nrm((DEPTH, N_GROUPS, EXPERTS_PER_GROUP), 0.01),
        'w_gate': nrm((DEPTH, N_EXPERTS, D_MODEL, D_EXPERT), D_MODEL ** -0.5),
        'w_up': nrm((DEPTH, N_EXPERTS, D_MODEL, D_EXPERT), D_MODEL ** -0.5),
        'w_down': nrm((DEPTH, N_EXPERTS, D_EXPERT, D_MODEL), D_EXPERT ** -0.5),
        'norm_final': gain((D_MODEL,)),
    }


def reference(x_prompt, x_sample, mem_prompt, state_conv, state_lru_conv, state_lru_h, cache_mem_k, cache_mem_v,
              norm_mix, w_in, conv_w, conv_b, conv_gn_g, conv_gn_b, lru_conv_w, lru_conv_b, lru_wa, lru_ba,
              lru_wi, lru_bi, lru_lambda, w_out, norm_attn, norm_mem_kv, w_q, w_k, w_v, w_o, norm_ffn,
              w_router_g, b_router_g, w_router_e, b_router_e, w_gate, w_up, w_down, norm_final):
    xp, xs = x_prompt, x_sample
    bp = x_prompt.shape[0]
    conv_p, lruc_p, h_p, mk_p, mv_p = [], [], [], [], []
    conv_s, lruc_s, h_s = [], [], []
    for l in range(DEPTH):
        p = dict(norm_mix=norm_mix[l], w_in=w_in[l], conv_w=conv_w[l], conv_b=conv_b[l], conv_gn_g=conv_gn_g[l],
                 conv_gn_b=conv_gn_b[l], lru_conv_w=lru_conv_w[l], lru_conv_b=lru_conv_b[l], lru_wa=lru_wa[l],
                 lru_ba=lru_ba[l], lru_wi=lru_wi[l], lru_bi=lru_bi[l], lru_lambda=lru_lambda[l], w_out=w_out[l],
                 norm_attn=norm_attn[l], norm_mem_kv=norm_mem_kv[l], w_q=w_q[l], w_k=w_k[l], w_v=w_v[l], w_o=w_o[l],
                 norm_ffn=norm_ffn[l], w_router_g=w_router_g[l], b_router_g=b_router_g[l], w_router_e=w_router_e[l],
                 b_router_e=b_router_e[l], w_gate=w_gate[l], w_up=w_up[l], w_down=w_down[l])
        k_p, v_p = mem_kv(mem_prompt, p)
        zc = jnp.zeros((bp, CONV_K - 1, C_CONV), xp.dtype)
        zl = jnp.zeros((bp, LRU_CONV_K - 1, C_LRU), xp.dtype)
        zh = jnp.zeros((bp, C_LRU), xp.dtype)
        xp, cb, lb, hh = layer(xp, zc, zl, zh, k_p, v_p, p)
        conv_p.append(cb); lruc_p.append(lb); h_p.append(hh); mk_p.append(k_p); mv_p.append(v_p)
        xs, cb, lb, hh = layer(xs, state_conv[l], state_lru_conv[l], state_lru_h[l], cache_mem_k[l], cache_mem_v[l], p)
        conv_s.append(cb); lruc_s.append(lb); h_s.append(hh)
    y_prompt = rmsnorm(xp, norm_final)
    y_sample = rmsnorm(xs, norm_final)
    return (y_prompt, y_sample, jnp.stack(conv_p), jnp.stack(lruc_p), jnp.stack(h_p), jnp.stack(mk_p), jnp.stack(mv_p),
            jnp.stack(conv_s), jnp.stack(lruc_s), jnp.stack(h_s))
```

```python
import functools
import math

import jax
import jax.numpy as jnp
from jax import lax
from jax.experimental import pallas as pl
from jax.experimental.pallas import tpu as pltpu

F32 = jnp.float32
BF16 = jnp.bfloat16
I32 = jnp.int32

D_MODEL = 4096
BATCH = 4
SEQ = 2048
DEPTH = 2
DEC_BATCH = 128
C_CONV = D_MODEL // 2
C_LRU = D_MODEL // 2
D_IN = 2 * C_CONV + 2 * C_LRU
CONV_GROUPS = 16
GROUP_W = C_CONV // CONV_GROUPS
CONV_K = 31
LRU_HEADS = 16
LRU_HD = C_LRU // LRU_HEADS
LRU_CONV_K = 4
RG_C = 8.0
MEM_LEN = 256
MEM_HEADS = 4
MEM_HD = D_MODEL // 16
MEM_W = MEM_HEADS * MEM_HD
N_GROUPS = 4
EXPERTS_PER_GROUP = 8
N_EXPERTS = N_GROUPS * EXPERTS_PER_GROUP
TOP_K = 2
D_EXPERT = D_MODEL // 4
EPS = 1e-6

N_P = BATCH * SEQ
N_S = DEC_BATCH
N_TOK = N_P + N_S
N_SLOT = N_TOK * TOP_K

LANES = 128
SUBLANES = 8
VMEM_CAP = 56 * 1024 * 1024

TM = 1040
TM_NORM = 416
MOE_BLOCK = 128
N_BLOCKS = -(-(N_SLOT + N_EXPERTS * (MOE_BLOCK - 1)) // MOE_BLOCK)
N_ROWS = N_BLOCKS * MOE_BLOCK
TF = 256
TN_DOWN = 1024
ROUTER_W = LANES
HALF = D_MODEL // 2
CONV_TT = 64
CONV_TC = 256
LRU_TT = 256
COMB_TM = 128
ATT_TQ = 512
S_BT = 16
ATT_S_BT = 4
DMA_WINDOW = 64


def _cparams(sem, vmem_bytes):
    limit = min(VMEM_CAP, max(32 * 1024 * 1024, int(vmem_bytes)))
    return pltpu.CompilerParams(dimension_semantics=sem, vmem_limit_bytes=limit)


def _sigmoid(x):
    return 1.0 / (1.0 + jnp.exp(-x))


def _silu(x):
    return x * _sigmoid(x)


def _gelu_tanh(x):
    c = math.sqrt(2.0 / math.pi)
    return 0.5 * x * (1.0 + jnp.tanh(c * (x + 0.044715 * (x * x * x))))


def _rms(x, g):
    ms = jnp.mean(x * x, axis=-1, keepdims=True)
    return x * lax.rsqrt(ms + EPS) * g


def _norm_kernel(x_ref, g_ref, o_ref):
    o_ref[...] = _rms(x_ref[...], g_ref[...]).astype(o_ref.dtype)


def _norm(x, g, *, tm, out_dtype=BF16, row_block0=0, n_rows=None):
    n_rows = x.shape[0] if n_rows is None else n_rows
    d = x.shape[1]
    return pl.pallas_call(
        _norm_kernel,
        grid=(n_rows // tm,),
        in_specs=[pl.BlockSpec((tm, d), lambda i: (i + row_block0, 0)),
                  pl.BlockSpec((1, d), lambda i: (0, 0))],
        out_specs=pl.BlockSpec((tm, d), lambda i: (i, 0)),
        out_shape=jax.ShapeDtypeStruct((n_rows, d), out_dtype),
        compiler_params=_cparams(("arbitrary",), 6 * tm * d * 4),
        name="rmsnorm",
    )(x, g.reshape(1, d))


def _mm_kernel(*refs, n_a, has_res):
    a_refs = refs[:n_a]
    w_ref = refs[n_a]
    res_ref = refs[n_a + 1] if has_res else None
    o_ref = refs[n_a + 1 + int(has_res)]
    wbf_ref = refs[n_a + 2 + int(has_res)]

    @pl.when(pl.program_id(1) == 0)
    def _():
        wbf_ref[...] = w_ref[...].astype(BF16)

    acc = None
    off = 0
    for a_ref in a_refs:
        k = a_ref.shape[1]
        a = a_ref[...].astype(BF16)
        part = jnp.dot(a, wbf_ref[off:off + k, :], preferred_element_type=F32)
        acc = part if acc is None else acc + part
        off += k
    if has_res:
        acc = acc + res_ref[...]
    o_ref[...] = acc.astype(o_ref.dtype)


def _mm(a_list, w, *, tn, tm, res=None, out_dtype=F32, name="proj"):
    m = a_list[0].shape[0]
    k, n = w.shape
    assert sum(a.shape[1] for a in a_list) == k
    in_specs = [pl.BlockSpec((tm, a.shape[1]), lambda j, i: (i, 0)) for a in a_list]
    in_specs.append(pl.BlockSpec((k, tn), lambda j, i: (0, j)))
    args = list(a_list) + [w]
    if res is not None:
        in_specs.append(pl.BlockSpec((tm, tn), lambda j, i: (i, j)))
        args.append(res)
    a_bytes = sum(a.dtype.itemsize * a.shape[1] for a in a_list) * tm
    vmem = 2 * a_bytes + 2 * k * tn * 4 + k * tn * 2 + 6 * tm * tn * 4 + (4 << 20)
    return pl.pallas_call(
        functools.partial(_mm_kernel, n_a=len(a_list), has_res=res is not None),
        grid=(n // tn, m // tm),
        in_specs=in_specs,
        out_specs=pl.BlockSpec((tm, tn), lambda j, i: (i, j)),
        out_shape=jax.ShapeDtypeStruct((m, n), out_dtype),
        scratch_shapes=[pltpu.VMEM((k, tn), BF16)],
        compiler_params=_cparams(("arbitrary", "arbitrary"), vmem),
        name=name,
    )(*args)


def _group_norm_silu(y, gn_g, gn_b, store):
    for g in range(y.shape[1] // GROUP_W):
        sl = slice(g * GROUP_W, (g + 1) * GROUP_W)
        yg = y[:, sl]
        mu = jnp.mean(yg, axis=-1, keepdims=True)
        d = yg - mu
        var = jnp.mean(d * d, axis=-1, keepdims=True)
        yn = d * lax.rsqrt(var + EPS) * gn_g[:, sl] + gn_b[:, sl]
        store(sl, _silu(yn))


CONV_PAD = 32


def _causal_taps(src_ref, w_ref, t0, tt, first_off, n_taps, init):
    acc = init
    for s in range(SUBLANES):
        part = None
        for k in range(n_taps):
            off = first_off + k
            if off % SUBLANES != s:
                continue
            base = pl.multiple_of(t0 + (off - s), SUBLANES)
            term = src_ref[pl.ds(base, tt + SUBLANES), :] * w_ref[k:k + 1, :]
            part = term if part is None else part + term
        if part is not None:
            acc = acc + part[s:s + tt, :]
    return acc


def _last_seq(b):
    return jnp.minimum(b, BATCH - 1)


def _prompt_or_zero(body, y_ref):
    b = pl.program_id(1)

    @pl.when(b < BATCH)
    def _():
        body()

    @pl.when(b == BATCH)
    def _():
        y_ref[...] = jnp.zeros(y_ref.shape, y_ref.dtype)


def _conv_p_kernel(cv_ref, cg_ref, w_ref, b_ref, gng_ref, gnb_ref, y_ref, st_ref, upad_ref):
    _prompt_or_zero(functools.partial(_conv_p_body, cv_ref, cg_ref, w_ref, b_ref, gng_ref, gnb_ref,
                                      y_ref, st_ref, upad_ref), y_ref)


def _conv_p_body(cv_ref, cg_ref, w_ref, b_ref, gng_ref, gnb_ref, y_ref, st_ref, upad_ref):
    t_len = cv_ref.shape[0]
    tc = cv_ref.shape[1]
    upad_ref[0:CONV_PAD, :] = jnp.zeros((CONV_PAD, tc), F32)
    upad_ref[CONV_PAD + t_len:CONV_PAD + t_len + SUBLANES, :] = jnp.zeros((SUBLANES, tc), F32)

    def glu(c, carry):
        t0 = pl.multiple_of(c * LRU_TT, LRU_TT)
        u = cv_ref[pl.ds(t0, LRU_TT), :] * _sigmoid(cg_ref[pl.ds(t0, LRU_TT), :])
        upad_ref[pl.ds(CONV_PAD + t0, LRU_TT), :] = u
        return carry

    lax.fori_loop(0, t_len // LRU_TT, glu, 0)
    st_ref[0] = upad_ref[CONV_PAD + t_len - (CONV_K - 1):CONV_PAD + t_len, :]

    bias = b_ref[...]
    gn_g = gng_ref[...]
    gn_b = gnb_ref[...]
    shift = CONV_PAD - (CONV_K - 1)

    def chunk(c, carry):
        t0 = pl.multiple_of(c * CONV_TT, CONV_TT)
        acc = _causal_taps(upad_ref, w_ref, t0, CONV_TT, shift, CONV_K,
                           jnp.zeros((CONV_TT, tc), F32) + bias)

        def store(sl, v):
            y_ref[pl.ds(t0, CONV_TT), sl] = v.astype(y_ref.dtype)

        _group_norm_silu(acc, gn_g, gn_b, store)
        return carry

    lax.fori_loop(0, t_len // CONV_TT, chunk, 0)


def _conv_prompt(z, conv_w, conv_b, gn_g, gn_b):
    tc = CONV_TC
    nct = C_CONV // tc
    return pl.pallas_call(
        _conv_p_kernel,
        grid=(nct, BATCH + 1),
        in_specs=[pl.BlockSpec((SEQ, tc), lambda c, b: (_last_seq(b), c)),
                  pl.BlockSpec((SEQ, tc), lambda c, b: (_last_seq(b), c + nct)),
                  pl.BlockSpec((CONV_K, tc), lambda c, b: (0, c)),
                  pl.BlockSpec((1, tc), lambda c, b: (0, c)),
                  pl.BlockSpec((1, tc), lambda c, b: (0, c)),
                  pl.BlockSpec((1, tc), lambda c, b: (0, c))],
        out_specs=[pl.BlockSpec((SEQ, tc), lambda c, b: (b, c)),
                   pl.BlockSpec((1, CONV_K - 1, tc), lambda c, b: (_last_seq(b), 0, c))],
        out_shape=[jax.ShapeDtypeStruct((N_TOK, C_CONV), BF16),
                   jax.ShapeDtypeStruct((BATCH, CONV_K - 1, C_CONV), F32)],
        scratch_shapes=[pltpu.VMEM((CONV_PAD + SEQ + SUBLANES, tc), F32)],
        compiler_params=_cparams(("arbitrary", "arbitrary"), 8 * SEQ * tc * 4),
        name="conv_prompt",
    )(z, z, conv_w, conv_b.reshape(1, -1), gn_g.reshape(1, -1), gn_b.reshape(1, -1))


def _log_sigmoid(x):
    return -(jnp.maximum(-x, 0.0) + jnp.log(1.0 + jnp.exp(-jnp.abs(x))))


def _lru_gates(xc, wa, ba, wi, bi, log_sig_lam):
    xb = xc.astype(BF16)
    r = _sigmoid(jnp.dot(xb, wa, preferred_element_type=F32) + ba)
    i = _sigmoid(jnp.dot(xb, wi, preferred_element_type=F32) + bi)
    a = jnp.exp(RG_C * r * log_sig_lam)
    return a, jnp.sqrt(1.0 - a * a) * (i * xc)


LRU_PAD = 8


def _lru_p_kernel(*refs):
    _prompt_or_zero(functools.partial(_lru_p_body, *refs), refs[9])


def _lru_p_body(lx_ref, lg_ref, cw_ref, cb_ref, wa_ref, ba_ref, wi_ref, bi_ref, lam_ref,
                y_ref, buf_ref, h_ref, xpad_ref, a_ref, b_ref):
    t_len = lx_ref.shape[0]
    tc = lx_ref.shape[1]
    n_heads = tc // LRU_HD
    xpad_ref[0:LRU_PAD, :] = jnp.zeros((LRU_PAD, tc), F32)
    xpad_ref[LRU_PAD + t_len:LRU_PAD + t_len + SUBLANES, :] = jnp.zeros((SUBLANES, tc), F32)

    def copy_in(c, carry):
        t0 = pl.multiple_of(c * LRU_TT, LRU_TT)
        xpad_ref[pl.ds(LRU_PAD + t0, LRU_TT), :] = lx_ref[pl.ds(t0, LRU_TT), :]
        return carry

    lax.fori_loop(0, t_len // LRU_TT, copy_in, 0)
    buf_ref[0] = lx_ref[t_len - (LRU_CONV_K - 1):t_len, :]

    cb = cb_ref[...]
    log_sig_lam = _log_sigmoid(lam_ref[...])
    shift = LRU_PAD - (LRU_CONV_K - 1)

    def gates(c, carry):
        t0 = pl.multiple_of(c * LRU_TT, LRU_TT)
        xc = _causal_taps(xpad_ref, cw_ref, t0, LRU_TT, shift, LRU_CONV_K,
                          jnp.zeros((LRU_TT, tc), F32) + cb)
        for hh in range(n_heads):
            sl = slice(hh * LRU_HD, (hh + 1) * LRU_HD)
            a, b = _lru_gates(xc[:, sl], wa_ref[hh].astype(BF16), ba_ref[:, sl],
                              wi_ref[hh].astype(BF16), bi_ref[:, sl], log_sig_lam[:, sl])
            a_ref[pl.ds(t0, LRU_TT), sl] = a
            b_ref[pl.ds(t0, LRU_TT), sl] = b
        return carry

    lax.fori_loop(0, t_len // LRU_TT, gates, 0)

    row = lax.broadcasted_iota(I32, (SUBLANES, tc), 0)

    def scan(i, h):
        t0 = pl.multiple_of(i * SUBLANES, SUBLANES)
        a = a_ref[pl.ds(t0, SUBLANES), :]
        b = b_ref[pl.ds(t0, SUBLANES), :]
        for d in (1, 2, 4):
            a_sh = pltpu.roll(a, d, axis=0)
            b_sh = pltpu.roll(b, d, axis=0)
            m = row >= d
            b = jnp.where(m, a * b_sh + b, b)
            a = jnp.where(m, a * a_sh, a)
        hb = a * h + b
        b_ref[pl.ds(t0, SUBLANES), :] = hb
        return hb[SUBLANES - 1:SUBLANES, :]

    h_last = lax.fori_loop(0, t_len // SUBLANES, scan, jnp.zeros((1, tc), F32), unroll=4)
    h_ref[0] = h_last

    def gate_out(c, carry):
        t0 = pl.multiple_of(c * LRU_TT, LRU_TT)
        y = b_ref[pl.ds(t0, LRU_TT), :] * _gelu_tanh(lg_ref[pl.ds(t0, LRU_TT), :])
        y_ref[pl.ds(t0, LRU_TT), :] = y.astype(y_ref.dtype)
        return carry

    lax.fori_loop(0, t_len // LRU_TT, gate_out, 0)


def _lru_prompt(z, cw, cb, wa, ba, wi, bi, lam):
    tc = CONV_TC
    nct = C_LRU // tc
    hpb = tc // LRU_HD
    col0 = 2 * C_CONV // tc
    vec = lambda v: v.reshape(1, -1)
    return pl.pallas_call(
        _lru_p_kernel,
        grid=(nct, BATCH + 1),
        in_specs=[pl.BlockSpec((SEQ, tc), lambda c, b: (_last_seq(b), c + col0)),
                  pl.BlockSpec((SEQ, tc), lambda c, b: (_last_seq(b), c + col0 + nct)),
                  pl.BlockSpec((LRU_CONV_K, tc), lambda c, b: (0, c)),
                  pl.BlockSpec((1, tc), lambda c, b: (0, c)),
                  pl.BlockSpec((hpb, LRU_HD, LRU_HD), lambda c, b: (c, 0, 0)),
                  pl.BlockSpec((1, tc), lambda c, b: (0, c)),
                  pl.BlockSpec((hpb, LRU_HD, LRU_HD), lambda c, b: (c, 0, 0)),
                  pl.BlockSpec((1, tc), lambda c, b: (0, c)),
                  pl.BlockSpec((1, tc), lambda c, b: (0, c))],
        out_specs=[pl.BlockSpec((SEQ, tc), lambda c, b: (b, c)),
                   pl.BlockSpec((1, LRU_CONV_K - 1, tc), lambda c, b: (_last_seq(b), 0, c)),
                   pl.BlockSpec((1, 1, tc), lambda c, b: (_last_seq(b), 0, c))],
        out_shape=[jax.ShapeDtypeStruct((N_TOK, C_LRU), BF16),
                   jax.ShapeDtypeStruct((BATCH, LRU_CONV_K - 1, C_LRU), F32),
                   jax.ShapeDtypeStruct((BATCH, 1, C_LRU), F32)],
        scratch_shapes=[pltpu.VMEM((LRU_PAD + SEQ + SUBLANES, tc), F32),
                        pltpu.VMEM((SEQ, tc), F32),
                        pltpu.VMEM((SEQ, tc), F32)],
        compiler_params=_cparams(("arbitrary", "arbitrary"), 12 * SEQ * tc * 4),
        name="lru_prompt",
    )(z, z, cw, vec(cb), wa, vec(ba), wi, vec(bi), vec(lam))


def _mixer_s_kernel(cv_ref, cg_ref, lx_ref, lg_ref, st_ref, lst_ref, h0_ref,
                    cw_ref, cb_ref, gng_ref, gnb_ref, lcw_ref, lcb_ref,
                    wa_ref, ba_ref, wi_ref, bi_ref, lam_ref, yc_in, yl_in,
                    yc_ref, yl_ref, nst_ref, nlst_ref, nh_ref):
    del yc_in, yl_in
    u = cv_ref[...] * _sigmoid(cg_ref[...])
    acc = u * cw_ref[CONV_K - 1:CONV_K, :] + cb_ref[...]
    for k in range(CONV_K - 1):
        row = st_ref[:, k, :]
        acc = acc + row * cw_ref[k:k + 1, :]
        if k > 0:
            nst_ref[:, k - 1, :] = row
    nst_ref[:, CONV_K - 2, :] = u

    def store_c(sl, v):
        yc_ref[:, sl] = v.astype(yc_ref.dtype)

    _group_norm_silu(acc, gng_ref[...], gnb_ref[...], store_c)

    lx = lx_ref[...]
    xc = lx * lcw_ref[LRU_CONV_K - 1:LRU_CONV_K, :] + lcb_ref[...]
    for k in range(LRU_CONV_K - 1):
        row = lst_ref[:, k, :]
        xc = xc + row * lcw_ref[k:k + 1, :]
        if k > 0:
            nlst_ref[:, k - 1, :] = row
    nlst_ref[:, LRU_CONV_K - 2, :] = lx

    log_sig_lam = _log_sigmoid(lam_ref[...])
    for hh in range(LRU_HEADS):
        sl = slice(hh * LRU_HD, (hh + 1) * LRU_HD)
        a, b = _lru_gates(xc[:, sl], wa_ref[hh].astype(BF16), ba_ref[:, sl],
                          wi_ref[hh].astype(BF16), bi_ref[:, sl], log_sig_lam[:, sl])
        h = a * h0_ref[:, sl] + b
        nh_ref[:, sl] = h
        yl_ref[:, sl] = (h * _gelu_tanh(lg_ref[:, sl])).astype(yl_ref.dtype)


def _mixer_sample(z, st, lst, h0, yc, yl, cw, cb, gn_g, gn_b, lcw, lcb, wa, ba, wi, bi, lam):
    bt = S_BT
    rb0 = N_P // bt
    vec = lambda v: v.reshape(1, -1)
    zspec = lambda col: pl.BlockSpec((bt, C_CONV), lambda i: (i + rb0, col))
    full = lambda shape: pl.BlockSpec(shape, lambda i: (0,) * len(shape))
    any_spec = pl.BlockSpec(memory_space=pl.ANY)
    return pl.pallas_call(
        _mixer_s_kernel,
        grid=(N_S // bt,),
        in_specs=[zspec(0), zspec(1), zspec(2), zspec(3),
                  pl.BlockSpec((bt, CONV_K - 1, C_CONV), lambda i: (i, 0, 0)),
                  pl.BlockSpec((bt, LRU_CONV_K - 1, C_LRU), lambda i: (i, 0, 0)),
                  pl.BlockSpec((bt, C_LRU), lambda i: (i, 0)),
                  full((CONV_K, C_CONV)), full((1, C_CONV)), full((1, C_CONV)), full((1, C_CONV)),
                  full((LRU_CONV_K, C_LRU)), full((1, C_LRU)),
                  full((LRU_HEADS, LRU_HD, LRU_HD)), full((1, C_LRU)),
                  full((LRU_HEADS, LRU_HD, LRU_HD)), full((1, C_LRU)), full((1, C_LRU)),
                  any_spec, any_spec],
        out_specs=[pl.BlockSpec((bt, C_CONV), lambda i: (i + rb0, 0)),
                   pl.BlockSpec((bt, C_LRU), lambda i: (i + rb0, 0)),
                   pl.BlockSpec((bt, CONV_K - 1, C_CONV), lambda i: (i, 0, 0)),
                   pl.BlockSpec((bt, LRU_CONV_K - 1, C_LRU), lambda i: (i, 0, 0)),
                   pl.BlockSpec((bt, C_LRU), lambda i: (i, 0))],
        out_shape=[jax.ShapeDtypeStruct(yc.shape, yc.dtype),
                   jax.ShapeDtypeStruct(yl.shape, yl.dtype),
                   jax.ShapeDtypeStruct(st.shape, F32),
                   jax.ShapeDtypeStruct(lst.shape, F32),
                   jax.ShapeDtypeStruct(h0.shape, F32)],
        input_output_aliases={18: 0, 19: 1},
        compiler_params=_cparams(("arbitrary",), 48 << 20),
        name="mixer_sample",
    )(z, z, z, z, st, lst, h0, cw, vec(cb), vec(gn_g), vec(gn_b), lcw, vec(lcb),
      wa, vec(ba), wi, vec(bi), vec(lam), yc, yl)


def _attn_p_kernel(q_ref, k_ref, v_ref, o_ref):
    i = pl.program_id(0)

    @pl.when(i < N_P // ATT_TQ)
    def _():
        _attn_p_body(q_ref, k_ref, v_ref, o_ref)

    @pl.when(i == N_P // ATT_TQ)
    def _():
        o_ref[...] = jnp.zeros(o_ref.shape, o_ref.dtype)


def _attn_p_body(q_ref, k_ref, v_ref, o_ref):
    scale = MEM_HD ** -0.5
    for h in range(MEM_HEADS):
        sl = slice(h * MEM_HD, (h + 1) * MEM_HD)
        q = q_ref[:, sl].astype(BF16)
        k = k_ref[:, sl].astype(BF16)
        v = v_ref[:, sl].astype(BF16)
        s = lax.dot_general(q, k, (((1,), (1,)), ((), ())), preferred_element_type=F32) * scale
        p = jnp.exp(s - jnp.max(s, axis=-1, keepdims=True))
        l = jnp.sum(p, axis=-1, keepdims=True)
        o = jnp.dot(p.astype(BF16), v, preferred_element_type=F32) / l
        o_ref[:, sl] = o.astype(o_ref.dtype)


def _attn_prompt(q, k, v):
    nq = SEQ // ATT_TQ
    return pl.pallas_call(
        _attn_p_kernel,
        grid=(BATCH * nq + 1,),
        in_specs=[pl.BlockSpec((ATT_TQ, MEM_W), lambda i: (i, 0)),
                  pl.BlockSpec((MEM_LEN, MEM_W), lambda i: (_last_seq(i // nq), 0)),
                  pl.BlockSpec((MEM_LEN, MEM_W), lambda i: (_last_seq(i // nq), 0))],
        out_specs=pl.BlockSpec((ATT_TQ, MEM_W), lambda i: (i, 0)),
        out_shape=jax.ShapeDtypeStruct((N_TOK, MEM_W), BF16),
        compiler_params=_cparams(("arbitrary",), 32 << 20),
        name="attn_prompt",
    )(q, k, v)


def _attn_s_kernel(q_ref, k_ref, v_ref, o_in, o_ref, stage_ref):
    del o_in
    jj = pl.program_id(1)
    scale = MEM_HD ** -0.5
    for bb in range(ATT_S_BT):
        r = jj * ATT_S_BT + bb
        q = q_ref[pl.ds(r, 1), :]
        prod = k_ref[bb] * q
        for h in range(MEM_HEADS):
            sl = slice(h * MEM_HD, (h + 1) * MEM_HD)
            s = jnp.sum(prod[:, sl], axis=-1, keepdims=True) * scale
            p = jnp.exp(s - jnp.max(s, axis=0, keepdims=True))
            l = jnp.sum(p, axis=0, keepdims=True)
            o = jnp.sum(p * v_ref[bb, :, sl], axis=0, keepdims=True) / l
            stage_ref[pl.ds(r, 1), sl] = o

    @pl.when(jj == pl.num_programs(1) - 1)
    def _():
        o_ref[...] = stage_ref[...].astype(o_ref.dtype)


def _attn_sample(q, kc, vc, o):
    bt = S_BT
    inner = bt // ATT_S_BT
    rb0 = N_P // bt
    return pl.pallas_call(
        _attn_s_kernel,
        grid=(N_S // bt, inner),
        in_specs=[pl.BlockSpec((bt, MEM_W), lambda i, j: (i + rb0, 0)),
                  pl.BlockSpec((ATT_S_BT, MEM_LEN, MEM_W), lambda i, j: (i * inner + j, 0, 0)),
                  pl.BlockSpec((ATT_S_BT, MEM_LEN, MEM_W), lambda i, j: (i * inner + j, 0, 0)),
                  pl.BlockSpec(memory_space=pl.ANY)],
        out_specs=pl.BlockSpec((bt, MEM_W), lambda i, j: (i + rb0, 0)),
        out_shape=jax.ShapeDtypeStruct(o.shape, o.dtype),
        scratch_shapes=[pltpu.VMEM((bt, MEM_W), F32)],
        input_output_aliases={3: 0},
        compiler_params=_cparams(("arbitrary", "arbitrary"), 40 << 20),
        name="attn_sample",
    )(q, kc, vc, o)


def _norm_router_kernel(x_ref, g_ref, wr_ref, br_ref, xp_ref, ids_ref, gates_ref):
    xn = _rms(x_ref[...], g_ref[...])
    xp_ref[...] = pltpu.pack_elementwise([xn[:, :HALF], xn[:, HALF:]], packed_dtype=BF16)
    logits = jnp.dot(xn.astype(BF16), wr_ref[...], preferred_element_type=F32) + br_ref[...]
    lane = lax.broadcasted_iota(I32, logits.shape, 1)
    neg = jnp.float32(-jnp.inf)

    def first_max(vals):
        m = jnp.max(vals, axis=-1, keepdims=True)
        idx = jnp.min(jnp.where(vals == m, lane, ROUTER_W), axis=-1, keepdims=True)
        return m, idx

    is_group = lane < N_GROUPS
    g_max, g_sel = first_max(jnp.where(is_group, logits, neg))
    p_g = 1.0 / jnp.sum(jnp.where(is_group, jnp.exp(logits - g_max), 0.0), axis=-1, keepdims=True)
    lo = N_GROUPS + EXPERTS_PER_GROUP * g_sel
    in_group = (lane >= lo) & (lane < lo + EXPERTS_PER_GROUP)
    e_logits = jnp.where(in_group, logits, neg)
    t1, i1 = first_max(e_logits)
    t2, i2 = first_max(jnp.where(lane == i1, neg, e_logits))
    e = jnp.exp(t2 - t1)
    w1 = p_g / (1.0 + e)
    w2 = p_g * e / (1.0 + e)
    ids_ref[...] = jnp.where(lane == 0, i1 - N_GROUPS, jnp.where(lane == 1, i2 - N_GROUPS, 0))
    gates_ref[...] = jnp.where(lane == 0, w1, jnp.where(lane == 1, w2, 0.0))


def _norm_router(x, g, wr, br):
    tm = TM_NORM
    return pl.pallas_call(
        _norm_router_kernel,
        grid=(N_TOK // tm,),
        in_specs=[pl.BlockSpec((tm, D_MODEL), lambda i: (i, 0)),
                  pl.BlockSpec((1, D_MODEL), lambda i: (0, 0)),
                  pl.BlockSpec((D_MODEL, ROUTER_W), lambda i: (0, 0)),
                  pl.BlockSpec((1, ROUTER_W), lambda i: (0, 0))],
        out_specs=[pl.BlockSpec((tm, HALF), lambda i: (i, 0)),
                   pl.BlockSpec((tm, ROUTER_W), lambda i: (i, 0)),
                   pl.BlockSpec((tm, ROUTER_W), lambda i: (i, 0))],
        out_shape=[jax.ShapeDtypeStruct((N_TOK, HALF), jnp.uint32),
                   jax.ShapeDtypeStruct((N_TOK, ROUTER_W), I32),
                   jax.ShapeDtypeStruct((N_TOK, ROUTER_W), F32)],
        compiler_params=_cparams(("arbitrary",), 8 * tm * D_MODEL * 4),
        name="norm_router",
    )(x, g.reshape(1, -1), wr, br)


def _dispatch_kernel(dest_ref, pad_lo_ref, pad_hi_ref, xp_hbm, xs_hbm, zero_ref, sem, zsem):
    def row_copy(s):
        return pltpu.make_async_copy(xp_hbm.at[pl.ds(s // TOP_K, 1), :],
                                     xs_hbm.at[pl.ds(dest_ref[s], 1), :], sem)

    def windowed(lo, hi, make_copy):
        def issue(r, carry):
            make_copy(r).start()

            @pl.when(r - lo >= DMA_WINDOW)
            def _():
                make_copy(r - DMA_WINDOW).wait()

            return carry

        lax.fori_loop(lo, hi, issue, 0)

        def drain(r, carry):
            make_copy(r).wait()
            return carry

        lax.fori_loop(jnp.maximum(lo, hi - DMA_WINDOW), hi, drain, 0)

    windowed(0, N_SLOT, row_copy)

    zero_ref[...] = jnp.zeros(zero_ref.shape, zero_ref.dtype)

    def zero_copy(r):
        return pltpu.make_async_copy(zero_ref, xs_hbm.at[pl.ds(r, 1), :], zsem)

    def per_range(e, carry):
        windowed(pad_lo_ref[e], pad_hi_ref[e], zero_copy)
        return carry

    lax.fori_loop(0, N_EXPERTS + 1, per_range, 0)


def _dispatch(xp, dest, pad_lo, pad_hi):
    return pl.pallas_call(
        _dispatch_kernel,
        grid_spec=pltpu.PrefetchScalarGridSpec(
            num_scalar_prefetch=3,
            grid=(1,),
            in_specs=[pl.BlockSpec(memory_space=pl.ANY)],
            out_specs=pl.BlockSpec(memory_space=pl.ANY),
            scratch_shapes=[pltpu.VMEM((1, HALF), jnp.uint32),
                            pltpu.SemaphoreType.DMA(()),
                            pltpu.SemaphoreType.DMA(())]),
        out_shape=jax.ShapeDtypeStruct((N_ROWS, HALF), jnp.uint32),
        compiler_params=_cparams(("arbitrary",), 32 << 20),
        name="moe_dispatch",
    )(dest, pad_lo, pad_hi, xp)


def _new_expert(be_ref, b):
    prev = be_ref[jnp.maximum(b - 1, 0)]
    return (b == 0) | (be_ref[b] != prev)


def _unpack_x(xp):
    lo = pltpu.unpack_elementwise(xp, index=0, packed_dtype=BF16, unpacked_dtype=F32)
    hi = pltpu.unpack_elementwise(xp, index=1, packed_dtype=BF16, unpacked_dtype=F32)
    return lo.astype(BF16), hi.astype(BF16)


def _up_kernel(be_ref, nv_ref, xs_ref, wg_ref, wu_ref, h_ref, wgb_ref, wub_ref):
    b = pl.program_id(1)
    valid = b < nv_ref[0]

    @pl.when(valid & _new_expert(be_ref, b))
    def _():
        wgb_ref[...] = wg_ref[...].astype(BF16)
        wub_ref[...] = wu_ref[...].astype(BF16)

    @pl.when(valid)
    def _():
        lo, hi = _unpack_x(xs_ref[...])
        g = (jnp.dot(lo, wgb_ref[:HALF, :], preferred_element_type=F32)
             + jnp.dot(hi, wgb_ref[HALF:, :], preferred_element_type=F32))
        u = (jnp.dot(lo, wub_ref[:HALF, :], preferred_element_type=F32)
             + jnp.dot(hi, wub_ref[HALF:, :], preferred_element_type=F32))
        h_ref[...] = (_silu(g) * u).astype(h_ref.dtype)

    @pl.when(jnp.logical_not(valid))
    def _():
        h_ref[...] = jnp.zeros(h_ref.shape, h_ref.dtype)


def _clamp_block(b, nv_ref):
    return jnp.minimum(b, nv_ref[0] - 1)


def _moe_up(xs, block_e, n_valid, w_gate, w_up):
    nj = D_EXPERT // TF
    wspec = pl.BlockSpec((None, D_MODEL, TF),
                         lambda j, b, be, nv: (be[_clamp_block(b, nv)], 0, j))
    return pl.pallas_call(
        _up_kernel,
        grid_spec=pltpu.PrefetchScalarGridSpec(
            num_scalar_prefetch=2,
            grid=(nj, N_BLOCKS),
            in_specs=[pl.BlockSpec((MOE_BLOCK, HALF), lambda j, b, be, nv: (_clamp_block(b, nv), 0)),
                      wspec, wspec],
            out_specs=pl.BlockSpec((MOE_BLOCK, TF), lambda j, b, be, nv: (b, j)),
            scratch_shapes=[pltpu.VMEM((D_MODEL, TF), BF16), pltpu.VMEM((D_MODEL, TF), BF16)]),
        out_shape=jax.ShapeDtypeStruct((N_ROWS, D_EXPERT), BF16),
        compiler_params=_cparams(("arbitrary", "arbitrary"), 40 << 20),
        name="moe_up",
    )(block_e, n_valid, xs, w_gate, w_up)


def _down_kernel(be_ref, nv_ref, h_ref, wd_ref, y_ref, wdb_ref):
    b = pl.program_id(1)
    valid = b < nv_ref[0]

    @pl.when(valid & _new_expert(be_ref, b))
    def _():
        wdb_ref[...] = wd_ref[...].astype(BF16)

    @pl.when(valid)
    def _():
        y_ref[...] = jnp.dot(h_ref[...], wdb_ref[...], preferred_element_type=F32)

    @pl.when(jnp.logical_not(valid))
    def _():
        y_ref[...] = jnp.zeros(y_ref.shape, y_ref.dtype)


def _moe_down(h, block_e, n_valid, w_down):
    nn = D_MODEL // TN_DOWN
    return pl.pallas_call(
        _down_kernel,
        grid_spec=pltpu.PrefetchScalarGridSpec(
            num_scalar_prefetch=2,
            grid=(nn, N_BLOCKS),
            in_specs=[pl.BlockSpec((MOE_BLOCK, D_EXPERT), lambda n, b, be, nv: (_clamp_block(b, nv), 0)),
                      pl.BlockSpec((None, D_EXPERT, TN_DOWN),
                                   lambda n, b, be, nv: (be[_clamp_block(b, nv)], 0, n))],
            out_specs=pl.BlockSpec((MOE_BLOCK, TN_DOWN), lambda n, b, be, nv: (b, n)),
            scratch_shapes=[pltpu.VMEM((D_EXPERT, TN_DOWN), BF16)]),
        out_shape=jax.ShapeDtypeStruct((N_ROWS, D_MODEL), F32),
        compiler_params=_cparams(("arbitrary", "arbitrary"), 32 << 20),
        name="moe_down",
    )(block_e, n_valid, h, w_down)


def _combine_kernel(dest_ref, x_ref, gates_ref, g_ref, ys_hbm, out_a, out_b, buf_ref, sem, *, final):
    i = pl.program_id(0)
    n = pl.num_programs(0)

    def row_copy(step, slot, t, k):
        s = (step * COMB_TM + t) * TOP_K + k
        return pltpu.make_async_copy(ys_hbm.at[pl.ds(dest_ref[s], 1), :],
                                     buf_ref.at[slot, k, pl.ds(t, 1), :], sem.at[slot])

    def start_all(step, slot):
        def body(t, carry):
            for k in range(TOP_K):
                row_copy(step, slot, t, k).start()
            return carry

        lax.fori_loop(0, COMB_TM, body, 0)

    def wait_all(step, slot):
        def body(t, carry):
            for k in range(TOP_K):
                row_copy(step, slot, t, k).wait()
            return carry

        lax.fori_loop(0, COMB_TM, body, 0)

    slot = i % 2

    @pl.when(i == 0)
    def _():
        start_all(i, slot)

    @pl.when(i + 1 < n)
    def _():
        start_all(i + 1, 1 - slot)

    wait_all(i, slot)
    gates = gates_ref[...]
    y = x_ref[...] + gates[:, 0:1] * buf_ref[slot, 0] + gates[:, 1:2] * buf_ref[slot, 1]
    yn = _rms(y, g_ref[...])
    if final:
        @pl.when(i < n - 1)
        def _():
            out_a[...] = yn

        @pl.when(i == n - 1)
        def _():
            out_b[...] = yn
    else:
        out_a[...] = y
        out_b[...] = yn.astype(out_b.dtype)


def _combine(x, ys, dest, gates, g_next, *, final):
    tm = COMB_TM
    assert N_S == tm
    n_p_blocks = N_P // tm
    if final:
        out_specs = [pl.BlockSpec((tm, D_MODEL), lambda i, d: (jnp.minimum(i, n_p_blocks - 1), 0)),
                     pl.BlockSpec((tm, D_MODEL), lambda i, d: (0, 0))]
        out_shape = [jax.ShapeDtypeStruct((N_P, D_MODEL), F32),
                     jax.ShapeDtypeStruct((N_S, D_MODEL), F32)]
    else:
        out_specs = [pl.BlockSpec((tm, D_MODEL), lambda i, d: (i, 0)),
                     pl.BlockSpec((tm, D_MODEL), lambda i, d: (i, 0))]
        out_shape = [jax.ShapeDtypeStruct((N_TOK, D_MODEL), F32),
                     jax.ShapeDtypeStruct((N_TOK, D_MODEL), BF16)]
    return pl.pallas_call(
        functools.partial(_combine_kernel, final=final),
        grid_spec=pltpu.PrefetchScalarGridSpec(
            num_scalar_prefetch=1,
            grid=(N_TOK // tm,),
            in_specs=[pl.BlockSpec((tm, D_MODEL), lambda i, d: (i, 0)),
                      pl.BlockSpec((tm, ROUTER_W), lambda i, d: (i, 0)),
                      pl.BlockSpec((1, D_MODEL), lambda i, d: (0, 0)),
                      pl.BlockSpec(memory_space=pl.ANY)],
            out_specs=out_specs,
            scratch_shapes=[pltpu.VMEM((2, TOP_K, tm, D_MODEL), F32),
                            pltpu.SemaphoreType.DMA((2,))]),
        out_shape=out_shape,
        compiler_params=_cparams(("arbitrary",), 40 << 20),
        name="moe_combine",
    )(dest, x, gates, g_next.reshape(1, -1), ys)


def _dispatch_plan(ids):
    flat_e = ids[:, :TOP_K].reshape(-1)
    onehot = (flat_e[:, None] == jnp.arange(N_EXPERTS, dtype=I32)[None, :]).astype(I32)
    csum = jnp.cumsum(onehot, axis=0)
    rank = jnp.sum(onehot * csum, axis=1) - 1
    counts = csum[-1]
    padded = ((counts + MOE_BLOCK - 1) // MOE_BLOCK) * MOE_BLOCK
    pend = jnp.cumsum(padded)
    pstart = pend - padded
    dest = (jnp.sum(onehot * pstart[None, :], axis=1) + rank).astype(I32)
    block_start = jnp.arange(N_BLOCKS, dtype=I32) * MOE_BLOCK
    block_e = jnp.minimum(jnp.sum((pend[None, :] <= block_start[:, None]).astype(I32), axis=1),
                          N_EXPERTS - 1).astype(I32)
    n_valid = (pend[-1:] // MOE_BLOCK).astype(I32)
    pad_lo = jnp.concatenate([pstart + counts, pend[-1:]]).astype(I32)
    pad_hi = jnp.concatenate([pend, jnp.full((1,), N_ROWS, I32)]).astype(I32)
    return dest, block_e, n_valid, pad_lo, pad_hi


def kernel(x_prompt, x_sample, mem_prompt, state_conv, state_lru_conv, state_lru_h, cache_mem_k, cache_mem_v, norm_mix, w_in, conv_w, conv_b, conv_gn_g, conv_gn_b, lru_conv_w, lru_conv_b, lru_wa, lru_ba, lru_wi, lru_bi, lru_lambda, w_out, norm_attn, norm_mem_kv, w_q, w_k, w_v, w_o, norm_ffn, w_router_g, b_router_g, w_router_e, b_router_e, w_gate, w_up, w_down, norm_final):
    x = jnp.concatenate([x_prompt.reshape(N_P, D_MODEL), x_sample.reshape(N_S, D_MODEL)], axis=0)
    mem = mem_prompt.reshape(BATCH * MEM_LEN, D_MODEL)
    xn = _norm(x, norm_mix[0], tm=TM_NORM)

    conv_p, lruc_p, h_p, mk_p, mv_p, conv_s, lruc_s, h_s = ([] for _ in range(8))
    for l in range(DEPTH):
        z = _mm([xn], w_in[l], tn=512, tm=TM, name="w_in")
        yc, cst = _conv_prompt(z, conv_w[l], conv_b[l], conv_gn_g[l], conv_gn_b[l])
        yl, lst, hl = _lru_prompt(z, lru_conv_w[l], lru_conv_b[l], lru_wa[l], lru_ba[l],
                                  lru_wi[l], lru_bi[l], lru_lambda[l])
        yc, yl, cst_s, lst_s, hl_s = _mixer_sample(
            z, state_conv[l], state_lru_conv[l], state_lru_h[l], yc, yl,
            conv_w[l], conv_b[l], conv_gn_g[l], conv_gn_b[l], lru_conv_w[l], lru_conv_b[l],
            lru_wa[l], lru_ba[l], lru_wi[l], lru_bi[l], lru_lambda[l])
        x = _mm([yc, yl], w_out[l], tn=512, tm=TM, res=x, name="w_out")
        conv_p.append(cst); lruc_p.append(lst); h_p.append(hl.reshape(BATCH, C_LRU))
        conv_s.append(cst_s); lruc_s.append(lst_s); h_s.append(hl_s)

        xn = _norm(x, norm_attn[l], tm=TM_NORM)
        q = _mm([xn], w_q[l], tn=512, tm=TM, name="w_q")
        mn = _norm(mem, norm_mem_kv[l], tm=256)
        k_p = _mm([mn], w_k[l], tn=512, tm=BATCH * MEM_LEN, name="w_k")
        v_p = _mm([mn], w_v[l], tn=512, tm=BATCH * MEM_LEN, name="w_v")
        o = _attn_prompt(q, k_p, v_p)
        o = _attn_sample(q, cache_mem_k[l].reshape(N_S, MEM_LEN, MEM_W),
                         cache_mem_v[l].reshape(N_S, MEM_LEN, MEM_W), o)
        x = _mm([o], w_o[l], tn=1024, tm=TM, res=x, name="w_o")
        mk_p.append(k_p.reshape(BATCH, MEM_LEN, MEM_HEADS, MEM_HD))
        mv_p.append(v_p.reshape(BATCH, MEM_LEN, MEM_HEADS, MEM_HD))

        wr = jnp.concatenate([w_router_g[l], w_router_e[l].reshape(D_MODEL, N_EXPERTS),
                              jnp.zeros((D_MODEL, ROUTER_W - N_GROUPS - N_EXPERTS), F32)],
                             axis=1).astype(BF16)
        br = jnp.concatenate([b_router_g[l], b_router_e[l].reshape(N_EXPERTS),
                              jnp.zeros((ROUTER_W - N_GROUPS - N_EXPERTS,), F32)]).reshape(1, ROUTER_W)
        xp, ids, gates = _norm_router(x, norm_ffn[l], wr, br)
        dest, block_e, n_valid, pad_lo, pad_hi = _dispatch_plan(ids)
        xs = _dispatch(xp, dest, pad_lo, pad_hi)
        hid = _moe_up(xs, block_e, n_valid, w_gate[l], w_up[l])
        ys = _moe_down(hid, block_e, n_valid, w_down[l])
        if l < DEPTH - 1:
            x, xn = _combine(x, ys, dest, gates, norm_mix[l + 1], final=False)
        else:
            y_p, y_s = _combine(x, ys, dest, gates, norm_final, final=True)

    y_prompt = y_p.reshape(BATCH, SEQ, D_MODEL)
    y_sample = y_s.reshape(DEC_BATCH, 1, D_MODEL)
    return (y_prompt, y_sample, jnp.stack(conv_p), jnp.stack(lruc_p), jnp.stack(h_p),
            jnp.stack(mk_p), jnp.stack(mv_p), jnp.stack(conv_s), jnp.stack(lruc_s), jnp.stack(h_s))
```

```python
import functools
import math

import jax
import jax.numpy as jnp
from jax import lax
from jax.experimental import pallas as pl
from jax.experimental.pallas import tpu as pltpu

F32 = jnp.float32
BF16 = jnp.bfloat16
I32 = jnp.int32

D_MODEL = 4096
BATCH = 4
SEQ = 2048
DEPTH = 2
DEC_BATCH = 128
C_CONV = D_MODEL // 2
C_LRU = D_MODEL // 2
D_IN = 2 * C_CONV + 2 * C_LRU
CONV_GROUPS = 16
GROUP_W = C_CONV // CONV_GROUPS
CONV_K = 31
LRU_HEADS = 16
LRU_HD = C_LRU // LRU_HEADS
LRU_CONV_K = 4
RG_C = 8.0
MEM_LEN = 256
MEM_HEADS = 4
MEM_HD = D_MODEL // 16
MEM_W = MEM_HEADS * MEM_HD
N_GROUPS = 4
EXPERTS_PER_GROUP = 8
N_EXPERTS = N_GROUPS * EXPERTS_PER_GROUP
TOP_K = 2
D_EXPERT = D_MODEL // 4
EPS = 1e-6

N_P = BATCH * SEQ
N_S = DEC_BATCH
N_TOK = N_P + N_S
N_SLOT = N_TOK * TOP_K

LANES = 128
SUBLANES = 8
VMEM_CAP = 56 * 1024 * 1024

TM = 1040
TM_NORM = 416
MOE_BLOCK = 128
N_BLOCKS = -(-(N_SLOT + N_EXPERTS * (MOE_BLOCK - 1)) // MOE_BLOCK)
N_ROWS = N_BLOCKS * MOE_BLOCK
TF = 256
TN_DOWN = 1024
ROUTER_W = LANES
HALF = D_MODEL // 2
CONV_TT = 64
CONV_TC = 256
LRU_TT = 256
COMB_TM = 128
ATT_TQ = 512
S_BT = 16
ATT_S_BT = 4
NSUB = 8
SB_ROWS = NSUB * MOE_BLOCK
N_SB = N_EXPERTS + 1 + N_BLOCKS // NSUB


def _cparams(sem, vmem_bytes):
    limit = min(VMEM_CAP, max(32 * 1024 * 1024, int(vmem_bytes)))
    return pltpu.CompilerParams(dimension_semantics=sem, vmem_limit_bytes=limit)


def _sigmoid(x):
    return 1.0 / (1.0 + jnp.exp(-x))


def _silu(x):
    return x * _sigmoid(x)


def _gelu_tanh(x):
    c = math.sqrt(2.0 / math.pi)
    return 0.5 * x * (1.0 + jnp.tanh(c * (x + 0.044715 * (x * x * x))))


def _rms(x, g):
    ms = jnp.mean(x * x, axis=-1, keepdims=True)
    return x * lax.rsqrt(ms + EPS) * g


def _norm_kernel(x_ref, g_ref, o_ref):
    o_ref[...] = _rms(x_ref[...], g_ref[...]).astype(o_ref.dtype)


def _norm(x, g, *, tm, out_dtype=BF16, row_block0=0, n_rows=None):
    n_rows = x.shape[0] if n_rows is None else n_rows
    d = x.shape[1]
    return pl.pallas_call(
        _norm_kernel,
        grid=(n_rows // tm,),
        in_specs=[pl.BlockSpec((tm, d), lambda i: (i + row_block0, 0)),
                  pl.BlockSpec((1, d), lambda i: (0, 0))],
        out_specs=pl.BlockSpec((tm, d), lambda i: (i, 0)),
        out_shape=jax.ShapeDtypeStruct((n_rows, d), out_dtype),
        compiler_params=_cparams(("arbitrary",), 6 * tm * d * 4),
        name="rmsnorm",
    )(x, g.reshape(1, d))


def _mm_kernel(*refs, n_a, has_res):
    a_refs = refs[:n_a]
    w_ref = refs[n_a]
    res_ref = refs[n_a + 1] if has_res else None
    o_ref = refs[n_a + 1 + int(has_res)]
    wbf_ref = refs[n_a + 2 + int(has_res)]

    @pl.when(pl.program_id(1) == 0)
    def _():
        wbf_ref[...] = w_ref[...].astype(BF16)

    acc = None
    off = 0
    for a_ref in a_refs:
        k = a_ref.shape[1]
        a = a_ref[...].astype(BF16)
        part = jnp.dot(a, wbf_ref[off:off + k, :], preferred_element_type=F32)
        acc = part if acc is None else acc + part
        off += k
    if has_res:
        acc = acc + res_ref[...]
    o_ref[...] = acc.astype(o_ref.dtype)


def _mm(a_list, w, layer, *, tn, tm, res=None, out_dtype=F32, name="proj"):
    m = a_list[0].shape[0]
    _, k, n = w.shape
    assert sum(a.shape[1] for a in a_list) == k
    in_specs = [pl.BlockSpec((tm, a.shape[1]), lambda j, i: (i, 0)) for a in a_list]
    in_specs.append(pl.BlockSpec((None, k, tn), lambda j, i: (layer, 0, j)))
    args = list(a_list) + [w]
    if res is not None:
        in_specs.append(pl.BlockSpec((tm, tn), lambda j, i: (i, j)))
        args.append(res)
    a_bytes = sum(a.dtype.itemsize * a.shape[1] for a in a_list) * tm
    vmem = 2 * a_bytes + 2 * k * tn * 4 + k * tn * 2 + 6 * tm * tn * 4 + (4 << 20)
    return pl.pallas_call(
        functools.partial(_mm_kernel, n_a=len(a_list), has_res=res is not None),
        grid=(n // tn, m // tm),
        in_specs=in_specs,
        out_specs=pl.BlockSpec((tm, tn), lambda j, i: (i, j)),
        out_shape=jax.ShapeDtypeStruct((m, n), out_dtype),
        scratch_shapes=[pltpu.VMEM((k, tn), BF16)],
        compiler_params=_cparams(("arbitrary", "arbitrary"), vmem),
        name=name,
    )(*args)


def _group_norm_silu(y, gn_g, gn_b, store):
    for g in range(y.shape[1] // GROUP_W):
        sl = slice(g * GROUP_W, (g + 1) * GROUP_W)
        yg = y[:, sl]
        mu = jnp.mean(yg, axis=-1, keepdims=True)
        d = yg - mu
        var = jnp.mean(d * d, axis=-1, keepdims=True)
        yn = d * lax.rsqrt(var + EPS) * gn_g[:, sl] + gn_b[:, sl]
        store(sl, _silu(yn))


CONV_PAD = 32


def _causal_taps(src_ref, w_ref, t0, tt, first_off, n_taps, init):
    acc = init
    for s in range(SUBLANES):
        part = None
        for k in range(n_taps):
            off = first_off + k
            if off % SUBLANES != s:
                continue
            base = pl.multiple_of(t0 + (off - s), SUBLANES)
            term = src_ref[pl.ds(base, tt + SUBLANES), :] * w_ref[k:k + 1, :]
            part = term if part is None else part + term
        if part is not None:
            acc = acc + part[s:s + tt, :]
    return acc


def _last_seq(b):
    return jnp.minimum(b, BATCH - 1)


def _prompt_or_zero(body, y_ref):
    b = pl.program_id(1)

    @pl.when(b < BATCH)
    def _():
        body()

    @pl.when(b == BATCH)
    def _():
        y_ref[...] = jnp.zeros(y_ref.shape, y_ref.dtype)


def _conv_p_kernel(cv_ref, cg_ref, w_ref, b_ref, gng_ref, gnb_ref, y_ref, st_ref, upad_ref):
    _prompt_or_zero(functools.partial(_conv_p_body, cv_ref, cg_ref, w_ref, b_ref, gng_ref, gnb_ref,
                                      y_ref, st_ref, upad_ref), y_ref)


def _conv_p_body(cv_ref, cg_ref, w_ref, b_ref, gng_ref, gnb_ref, y_ref, st_ref, upad_ref):
    t_len = cv_ref.shape[0]
    tc = cv_ref.shape[1]
    upad_ref[0:CONV_PAD, :] = jnp.zeros((CONV_PAD, tc), F32)
    upad_ref[CONV_PAD + t_len:CONV_PAD + t_len + SUBLANES, :] = jnp.zeros((SUBLANES, tc), F32)

    def glu(c, carry):
        t0 = pl.multiple_of(c * LRU_TT, LRU_TT)
        u = cv_ref[pl.ds(t0, LRU_TT), :] * _sigmoid(cg_ref[pl.ds(t0, LRU_TT), :])
        upad_ref[pl.ds(CONV_PAD + t0, LRU_TT), :] = u
        return carry

    lax.fori_loop(0, t_len // LRU_TT, glu, 0)
    st_ref[0] = upad_ref[CONV_PAD + t_len - (CONV_K - 1):CONV_PAD + t_len, :]

    bias = b_ref[...]
    gn_g = gng_ref[...]
    gn_b = gnb_ref[...]
    shift = CONV_PAD - (CONV_K - 1)

    def chunk(c, carry):
        t0 = pl.multiple_of(c * CONV_TT, CONV_TT)
        acc = _causal_taps(upad_ref, w_ref, t0, CONV_TT, shift, CONV_K,
                           jnp.zeros((CONV_TT, tc), F32) + bias)

        def store(sl, v):
            y_ref[pl.ds(t0, CONV_TT), sl] = v.astype(y_ref.dtype)

        _group_norm_silu(acc, gn_g, gn_b, store)
        return carry

    lax.fori_loop(0, t_len // CONV_TT, chunk, 0)


def _conv_prompt(z, conv_w, conv_b, gn_g, gn_b):
    tc = CONV_TC
    nct = C_CONV // tc
    return pl.pallas_call(
        _conv_p_kernel,
        grid=(nct, BATCH + 1),
        in_specs=[pl.BlockSpec((SEQ, tc), lambda c, b: (_last_seq(b), c)),
                  pl.BlockSpec((SEQ, tc), lambda c, b: (_last_seq(b), c + nct)),
                  pl.BlockSpec((CONV_K, tc), lambda c, b: (0, c)),
                  pl.BlockSpec((1, tc), lambda c, b: (0, c)),
                  pl.BlockSpec((1, tc), lambda c, b: (0, c)),
                  pl.BlockSpec((1, tc), lambda c, b: (0, c))],
        out_specs=[pl.BlockSpec((SEQ, tc), lambda c, b: (b, c)),
                   pl.BlockSpec((1, CONV_K - 1, tc), lambda c, b: (_last_seq(b), 0, c))],
        out_shape=[jax.ShapeDtypeStruct((N_TOK, C_CONV), BF16),
                   jax.ShapeDtypeStruct((BATCH, CONV_K - 1, C_CONV), F32)],
        scratch_shapes=[pltpu.VMEM((CONV_PAD + SEQ + SUBLANES, tc), F32)],
        compiler_params=_cparams(("arbitrary", "arbitrary"), 8 * SEQ * tc * 4),
        name="conv_prompt",
    )(z, z, conv_w, conv_b.reshape(1, -1), gn_g.reshape(1, -1), gn_b.reshape(1, -1))


def _log_sigmoid(x):
    return -(jnp.maximum(-x, 0.0) + jnp.log(1.0 + jnp.exp(-jnp.abs(x))))


def _lru_gates(xc, wa, ba, wi, bi, log_sig_lam):
    xb = xc.astype(BF16)
    r = _sigmoid(jnp.dot(xb, wa, preferred_element_type=F32) + ba)
    i = _sigmoid(jnp.dot(xb, wi, preferred_element_type=F32) + bi)
    a = jnp.exp(RG_C * r * log_sig_lam)
    return a, jnp.sqrt(1.0 - a * a) * (i * xc)


LRU_PAD = 8


def _lru_p_kernel(*refs):
    _prompt_or_zero(functools.partial(_lru_p_body, *refs), refs[9])


def _lru_p_body(lx_ref, lg_ref, cw_ref, cb_ref, wa_ref, ba_ref, wi_ref, bi_ref, lam_ref,
                y_ref, buf_ref, h_ref, xpad_ref, a_ref, b_ref):
    t_len = lx_ref.shape[0]
    tc = lx_ref.shape[1]
    n_heads = tc // LRU_HD
    xpad_ref[0:LRU_PAD, :] = jnp.zeros((LRU_PAD, tc), F32)
    xpad_ref[LRU_PAD + t_len:LRU_PAD + t_len + SUBLANES, :] = jnp.zeros((SUBLANES, tc), F32)

    def copy_in(c, carry):
        t0 = pl.multiple_of(c * LRU_TT, LRU_TT)
        xpad_ref[pl.ds(LRU_PAD + t0, LRU_TT), :] = lx_ref[pl.ds(t0, LRU_TT), :]
        return carry

    lax.fori_loop(0, t_len // LRU_TT, copy_in, 0)
    buf_ref[0] = lx_ref[t_len - (LRU_CONV_K - 1):t_len, :]

    cb = cb_ref[...]
    log_sig_lam = _log_sigmoid(lam_ref[...])
    shift = LRU_PAD - (LRU_CONV_K - 1)

    def gates(c, carry):
        t0 = pl.multiple_of(c * LRU_TT, LRU_TT)
        xc = _causal_taps(xpad_ref, cw_ref, t0, LRU_TT, shift, LRU_CONV_K,
                          jnp.zeros((LRU_TT, tc), F32) + cb)
        for hh in range(n_heads):
            sl = slice(hh * LRU_HD, (hh + 1) * LRU_HD)
            a, b = _lru_gates(xc[:, sl], wa_ref[hh].astype(BF16), ba_ref[:, sl],
                              wi_ref[hh].astype(BF16), bi_ref[:, sl], log_sig_lam[:, sl])
            a_ref[pl.ds(t0, LRU_TT), sl] = a
            b_ref[pl.ds(t0, LRU_TT), sl] = b
        return carry

    lax.fori_loop(0, t_len // LRU_TT, gates, 0)

    row = lax.broadcasted_iota(I32, (SUBLANES, tc), 0)

    def scan(i, h):
        t0 = pl.multiple_of(i * SUBLANES, SUBLANES)
        a = a_ref[pl.ds(t0, SUBLANES), :]
        b = b_ref[pl.ds(t0, SUBLANES), :]
        for d in (1, 2, 4):
            a_sh = pltpu.roll(a, d, axis=0)
            b_sh = pltpu.roll(b, d, axis=0)
            m = row >= d
            b = jnp.where(m, a * b_sh + b, b)
            a = jnp.where(m, a * a_sh, a)
        hb = a * h + b
        b_ref[pl.ds(t0, SUBLANES), :] = hb
        return hb[SUBLANES - 1:SUBLANES, :]

    h_last = lax.fori_loop(0, t_len // SUBLANES, scan, jnp.zeros((1, tc), F32), unroll=4)
    h_ref[0] = h_last

    def gate_out(c, carry):
        t0 = pl.multiple_of(c * LRU_TT, LRU_TT)
        y = b_ref[pl.ds(t0, LRU_TT), :] * _gelu_tanh(lg_ref[pl.ds(t0, LRU_TT), :])
        y_ref[pl.ds(t0, LRU_TT), :] = y.astype(y_ref.dtype)
        return carry

    lax.fori_loop(0, t_len // LRU_TT, gate_out, 0)


def _lru_prompt(z, cw, cb, wa, ba, wi, bi, lam):
    tc = CONV_TC
    nct = C_LRU // tc
    hpb = tc // LRU_HD
    col0 = 2 * C_CONV // tc
    vec = lambda v: v.reshape(1, -1)
    return pl.pallas_call(
        _lru_p_kernel,
        grid=(nct, BATCH + 1),
        in_specs=[pl.BlockSpec((SEQ, tc), lambda c, b: (_last_seq(b), c + col0)),
                  pl.BlockSpec((SEQ, tc), lambda c, b: (_last_seq(b), c + col0 + nct)),
                  pl.BlockSpec((LRU_CONV_K, tc), lambda c, b: (0, c)),
                  pl.BlockSpec((1, tc), lambda c, b: (0, c)),
                  pl.BlockSpec((hpb, LRU_HD, LRU_HD), lambda c, b: (c, 0, 0)),
                  pl.BlockSpec((1, tc), lambda c, b: (0, c)),
                  pl.BlockSpec((hpb, LRU_HD, LRU_HD), lambda c, b: (c, 0, 0)),
                  pl.BlockSpec((1, tc), lambda c, b: (0, c)),
                  pl.BlockSpec((1, tc), lambda c, b: (0, c))],
        out_specs=[pl.BlockSpec((SEQ, tc), lambda c, b: (b, c)),
                   pl.BlockSpec((1, LRU_CONV_K - 1, tc), lambda c, b: (_last_seq(b), 0, c)),
                   pl.BlockSpec((1, 1, tc), lambda c, b: (_last_seq(b), 0, c))],
        out_shape=[jax.ShapeDtypeStruct((N_TOK, C_LRU), BF16),
                   jax.ShapeDtypeStruct((BATCH, LRU_CONV_K - 1, C_LRU), F32),
                   jax.ShapeDtypeStruct((BATCH, 1, C_LRU), F32)],
        scratch_shapes=[pltpu.VMEM((LRU_PAD + SEQ + SUBLANES, tc), F32),
                        pltpu.VMEM((SEQ, tc), F32),
                        pltpu.VMEM((SEQ, tc), F32)],
        compiler_params=_cparams(("arbitrary", "arbitrary"), 12 * SEQ * tc * 4),
        name="lru_prompt",
    )(z, z, cw, vec(cb), wa, vec(ba), wi, vec(bi), vec(lam))


def _mixer_s_kernel(cv_ref, cg_ref, lx_ref, lg_ref, st_ref, lst_ref, h0_ref,
                    cw_ref, cb_ref, gng_ref, gnb_ref, lcw_ref, lcb_ref,
                    wa_ref, ba_ref, wi_ref, bi_ref, lam_ref, yc_in, yl_in,
                    yc_ref, yl_ref, nst_ref, nlst_ref, nh_ref):
    del yc_in, yl_in
    u = cv_ref[...] * _sigmoid(cg_ref[...])
    acc = u * cw_ref[CONV_K - 1:CONV_K, :] + cb_ref[...]
    for k in range(CONV_K - 1):
        row = st_ref[:, k, :]
        acc = acc + row * cw_ref[k:k + 1, :]
        if k > 0:
            nst_ref[:, k - 1, :] = row
    nst_ref[:, CONV_K - 2, :] = u

    def store_c(sl, v):
        yc_ref[:, sl] = v.astype(yc_ref.dtype)

    _group_norm_silu(acc, gng_ref[...], gnb_ref[...], store_c)

    lx = lx_ref[...]
    xc = lx * lcw_ref[LRU_CONV_K - 1:LRU_CONV_K, :] + lcb_ref[...]
    for k in range(LRU_CONV_K - 1):
        row = lst_ref[:, k, :]
        xc = xc + row * lcw_ref[k:k + 1, :]
        if k > 0:
            nlst_ref[:, k - 1, :] = row
    nlst_ref[:, LRU_CONV_K - 2, :] = lx

    log_sig_lam = _log_sigmoid(lam_ref[...])
    for hh in range(LRU_HEADS):
        sl = slice(hh * LRU_HD, (hh + 1) * LRU_HD)
        a, b = _lru_gates(xc[:, sl], wa_ref[hh].astype(BF16), ba_ref[:, sl],
                          wi_ref[hh].astype(BF16), bi_ref[:, sl], log_sig_lam[:, sl])
        h = a * h0_ref[:, sl] + b
        nh_ref[:, sl] = h
        yl_ref[:, sl] = (h * _gelu_tanh(lg_ref[:, sl])).astype(yl_ref.dtype)


def _mixer_sample(z, st, lst, h0, layer, yc, yl, cw, cb, gn_g, gn_b, lcw, lcb, wa, ba, wi, bi, lam):
    bt = S_BT
    rb0 = N_P // bt
    vec = lambda v: v.reshape(1, -1)
    zspec = lambda col: pl.BlockSpec((bt, C_CONV), lambda i: (i + rb0, col))
    full = lambda shape: pl.BlockSpec(shape, lambda i: (0,) * len(shape))
    any_spec = pl.BlockSpec(memory_space=pl.ANY)
    return pl.pallas_call(
        _mixer_s_kernel,
        grid=(N_S // bt,),
        in_specs=[zspec(0), zspec(1), zspec(2), zspec(3),
                  pl.BlockSpec((None, bt, CONV_K - 1, C_CONV), lambda i: (layer, i, 0, 0)),
                  pl.BlockSpec((None, bt, LRU_CONV_K - 1, C_LRU), lambda i: (layer, i, 0, 0)),
                  pl.BlockSpec((None, bt, C_LRU), lambda i: (layer, i, 0)),
                  full((CONV_K, C_CONV)), full((1, C_CONV)), full((1, C_CONV)), full((1, C_CONV)),
                  full((LRU_CONV_K, C_LRU)), full((1, C_LRU)),
                  full((LRU_HEADS, LRU_HD, LRU_HD)), full((1, C_LRU)),
                  full((LRU_HEADS, LRU_HD, LRU_HD)), full((1, C_LRU)), full((1, C_LRU)),
                  any_spec, any_spec],
        out_specs=[pl.BlockSpec((bt, C_CONV), lambda i: (i + rb0, 0)),
                   pl.BlockSpec((bt, C_LRU), lambda i: (i + rb0, 0)),
                   pl.BlockSpec((bt, CONV_K - 1, C_CONV), lambda i: (i, 0, 0)),
                   pl.BlockSpec((bt, LRU_CONV_K - 1, C_LRU), lambda i: (i, 0, 0)),
                   pl.BlockSpec((bt, C_LRU), lambda i: (i, 0))],
        out_shape=[jax.ShapeDtypeStruct(yc.shape, yc.dtype),
                   jax.ShapeDtypeStruct(yl.shape, yl.dtype),
                   jax.ShapeDtypeStruct(st.shape[1:], F32),
                   jax.ShapeDtypeStruct(lst.shape[1:], F32),
                   jax.ShapeDtypeStruct(h0.shape[1:], F32)],
        input_output_aliases={18: 0, 19: 1},
        compiler_params=_cparams(("arbitrary",), 48 << 20),
        name="mixer_sample",
    )(z, z, z, z, st, lst, h0, cw, vec(cb), vec(gn_g), vec(gn_b), lcw, vec(lcb),
      wa, vec(ba), wi, vec(bi), vec(lam), yc, yl)


def _attn_p_kernel(q_ref, k_ref, v_ref, o_ref):
    i = pl.program_id(0)

    @pl.when(i < N_P // ATT_TQ)
    def _():
        _attn_p_body(q_ref, k_ref, v_ref, o_ref)

    @pl.when(i == N_P // ATT_TQ)
    def _():
        o_ref[...] = jnp.zeros(o_ref.shape, o_ref.dtype)


def _attn_p_body(q_ref, k_ref, v_ref, o_ref):
    scale = MEM_HD ** -0.5
    for h in range(MEM_HEADS):
        sl = slice(h * MEM_HD, (h + 1) * MEM_HD)
        q = q_ref[:, sl].astype(BF16)
        k = k_ref[:, sl].astype(BF16)
        v = v_ref[:, sl].astype(BF16)
        s = lax.dot_general(q, k, (((1,), (1,)), ((), ())), preferred_element_type=F32) * scale
        p = jnp.exp(s - jnp.max(s, axis=-1, keepdims=True))
        l = jnp.sum(p, axis=-1, keepdims=True)
        o = jnp.dot(p.astype(BF16), v, preferred_element_type=F32) / l
        o_ref[:, sl] = o.astype(o_ref.dtype)


def _attn_prompt(q, k, v):
    nq = SEQ // ATT_TQ
    return pl.pallas_call(
        _attn_p_kernel,
        grid=(BATCH * nq + 1,),
        in_specs=[pl.BlockSpec((ATT_TQ, MEM_W), lambda i: (i, 0)),
                  pl.BlockSpec((MEM_LEN, MEM_W), lambda i: (_last_seq(i // nq), 0)),
                  pl.BlockSpec((MEM_LEN, MEM_W), lambda i: (_last_seq(i // nq), 0))],
        out_specs=pl.BlockSpec((ATT_TQ, MEM_W), lambda i: (i, 0)),
        out_shape=jax.ShapeDtypeStruct((N_TOK, MEM_W), BF16),
        compiler_params=_cparams(("arbitrary",), 32 << 20),
        name="attn_prompt",
    )(q, k, v)


def _attn_s_kernel(q_ref, k_ref, v_ref, o_in, o_ref, stage_ref):
    del o_in
    jj = pl.program_id(1)
    scale = MEM_HD ** -0.5
    for bb in range(ATT_S_BT):
        r = jj * ATT_S_BT + bb
        q = q_ref[pl.ds(r, 1), :]
        prod = k_ref[bb] * q
        for h in range(MEM_HEADS):
            sl = slice(h * MEM_HD, (h + 1) * MEM_HD)
            s = jnp.sum(prod[:, sl], axis=-1, keepdims=True) * scale
            p = jnp.exp(s - jnp.max(s, axis=0, keepdims=True))
            l = jnp.sum(p, axis=0, keepdims=True)
            o = jnp.sum(p * v_ref[bb, :, sl], axis=0, keepdims=True) / l
            stage_ref[pl.ds(r, 1), sl] = o

    @pl.when(jj == pl.num_programs(1) - 1)
    def _():
        o_ref[...] = stage_ref[...].astype(o_ref.dtype)


def _attn_sample(q, kc, vc, layer, o):
    bt = S_BT
    inner = bt // ATT_S_BT
    rb0 = N_P // bt
    return pl.pallas_call(
        _attn_s_kernel,
        grid=(N_S // bt, inner),
        in_specs=[pl.BlockSpec((bt, MEM_W), lambda i, j: (i + rb0, 0)),
                  pl.BlockSpec((None, ATT_S_BT, MEM_LEN, MEM_W), lambda i, j: (layer, i * inner + j, 0, 0)),
                  pl.BlockSpec((None, ATT_S_BT, MEM_LEN, MEM_W), lambda i, j: (layer, i * inner + j, 0, 0)),
                  pl.BlockSpec(memory_space=pl.ANY)],
        out_specs=pl.BlockSpec((bt, MEM_W), lambda i, j: (i + rb0, 0)),
        out_shape=jax.ShapeDtypeStruct(o.shape, o.dtype),
        scratch_shapes=[pltpu.VMEM((bt, MEM_W), F32)],
        input_output_aliases={3: 0},
        compiler_params=_cparams(("arbitrary", "arbitrary"), 40 << 20),
        name="attn_sample",
    )(q, kc, vc, o)


def _norm_router_kernel(x_ref, g_ref, wr_ref, br_ref, xp_ref, ids_ref, gates_ref):
    xn = _rms(x_ref[...], g_ref[...])
    xp_ref[...] = pltpu.pack_elementwise([xn[:, :HALF], xn[:, HALF:]], packed_dtype=BF16)
    logits = jnp.dot(xn.astype(BF16), wr_ref[...], preferred_element_type=F32) + br_ref[...]
    lane = lax.broadcasted_iota(I32, logits.shape, 1)
    neg = jnp.float32(-jnp.inf)

    def first_max(vals):
        m = jnp.max(vals, axis=-1, keepdims=True)
        idx = jnp.min(jnp.where(vals == m, lane, ROUTER_W), axis=-1, keepdims=True)
        return m, idx

    is_group = lane < N_GROUPS
    g_max, g_sel = first_max(jnp.where(is_group, logits, neg))
    p_g = 1.0 / jnp.sum(jnp.where(is_group, jnp.exp(logits - g_max), 0.0), axis=-1, keepdims=True)
    lo = N_GROUPS + EXPERTS_PER_GROUP * g_sel
    in_group = (lane >= lo) & (lane < lo + EXPERTS_PER_GROUP)
    e_logits = jnp.where(in_group, logits, neg)
    t1, i1 = first_max(e_logits)
    t2, i2 = first_max(jnp.where(lane == i1, neg, e_logits))
    e = jnp.exp(t2 - t1)
    w1 = p_g / (1.0 + e)
    w2 = p_g * e / (1.0 + e)
    ids_ref[...] = jnp.where(lane == 0, i1 - N_GROUPS, jnp.where(lane == 1, i2 - N_GROUPS, 0))
    gates_ref[...] = jnp.where(lane == 0, w1, jnp.where(lane == 1, w2, 0.0))


def _norm_router(x, g, wr, br):
    tm = TM_NORM
    return pl.pallas_call(
        _norm_router_kernel,
        grid=(N_TOK // tm,),
        in_specs=[pl.BlockSpec((tm, D_MODEL), lambda i: (i, 0)),
                  pl.BlockSpec((1, D_MODEL), lambda i: (0, 0)),
                  pl.BlockSpec((D_MODEL, ROUTER_W), lambda i: (0, 0)),
                  pl.BlockSpec((1, ROUTER_W), lambda i: (0, 0))],
        out_specs=[pl.BlockSpec((tm, HALF), lambda i: (i, 0)),
                   pl.BlockSpec((tm, ROUTER_W), lambda i: (i, 0)),
                   pl.BlockSpec((tm, ROUTER_W), lambda i: (i, 0))],
        out_shape=[jax.ShapeDtypeStruct((N_TOK, HALF), jnp.uint32),
                   jax.ShapeDtypeStruct((N_TOK, ROUTER_W), I32),
                   jax.ShapeDtypeStruct((N_TOK, ROUTER_W), F32)],
        compiler_params=_cparams(("arbitrary",), 8 * tm * D_MODEL * 4),
        name="norm_router",
    )(x, g.reshape(1, -1), wr, br)


def _unpack_x(xp):
    lo = pltpu.unpack_elementwise(xp, index=0, packed_dtype=BF16, unpacked_dtype=F32)
    hi = pltpu.unpack_elementwise(xp, index=1, packed_dtype=BF16, unpacked_dtype=F32)
    return lo.astype(BF16), hi.astype(BF16)


def _up_kernel(sbe_ref, sbb_ref, sbn_ref, tok_ref, xp_hbm, wg_ref, wu_ref, h_ref,
               xbuf_ref, wgb_ref, wub_ref, xsem):
    del sbe_ref
    sb = pl.program_id(0)
    j = pl.program_id(1)
    slot = sb % 2

    def row_copy(s, slot_, r):
        tok = tok_ref[sbb_ref[s] * MOE_BLOCK + r]
        return pltpu.make_async_copy(xp_hbm.at[pl.ds(tok, 1), :],
                                     xbuf_ref.at[slot_, pl.ds(r, 1), :], xsem.at[slot_])

    def start_rows(s, slot_):
        def body(r, carry):
            row_copy(s, slot_, r).start()
            return carry

        lax.fori_loop(0, sbn_ref[s] * MOE_BLOCK, body, 0)

    def wait_rows(s, slot_):
        def body(r, carry):
            row_copy(s, slot_, r).wait()
            return carry

        lax.fori_loop(0, sbn_ref[s] * MOE_BLOCK, body, 0)

    @pl.when(j == 0)
    def _():
        @pl.when(sb == 0)
        def _():
            start_rows(sb, slot)

        @pl.when(sb + 1 < pl.num_programs(0))
        def _():
            start_rows(sb + 1, 1 - slot)

        wait_rows(sb, slot)

    nsub = sbn_ref[sb]

    @pl.when(nsub > 0)
    def _():
        wgb_ref[...] = wg_ref[...].astype(BF16)
        wub_ref[...] = wu_ref[...].astype(BF16)

    def sub(i, carry):
        r0 = pl.multiple_of(i * MOE_BLOCK, MOE_BLOCK)
        lo, hi = _unpack_x(xbuf_ref[slot, pl.ds(r0, MOE_BLOCK), :])
        g = (jnp.dot(lo, wgb_ref[:HALF, :], preferred_element_type=F32)
             + jnp.dot(hi, wgb_ref[HALF:, :], preferred_element_type=F32))
        u = (jnp.dot(lo, wub_ref[:HALF, :], preferred_element_type=F32)
             + jnp.dot(hi, wub_ref[HALF:, :], preferred_element_type=F32))
        h_ref[pl.ds(r0, MOE_BLOCK), :] = (_silu(g) * u).astype(h_ref.dtype)
        return carry

    lax.fori_loop(0, nsub, sub, 0)

    def zero(i, carry):
        r0 = pl.multiple_of(i * MOE_BLOCK, MOE_BLOCK)
        h_ref[pl.ds(r0, MOE_BLOCK), :] = jnp.zeros((MOE_BLOCK, h_ref.shape[1]), h_ref.dtype)
        return carry

    lax.fori_loop(nsub, NSUB, zero, 0)


def _moe_up(xp, plan, w_gate, w_up, layer):
    nj = D_EXPERT // TF

    def w_index(sb, j, sbe, sbb, sbn, tok):
        return (layer, sbe[sb], 0, jnp.where(sbn[sb] > 0, j, nj - 1))

    wspec = pl.BlockSpec((None, None, D_MODEL, TF), w_index)
    return pl.pallas_call(
        _up_kernel,
        grid_spec=pltpu.PrefetchScalarGridSpec(
            num_scalar_prefetch=4,
            grid=(N_SB, nj),
            in_specs=[pl.BlockSpec(memory_space=pl.ANY), wspec, wspec],
            out_specs=pl.BlockSpec((SB_ROWS, TF), lambda sb, j, *_: (sb, j)),
            scratch_shapes=[pltpu.VMEM((2, SB_ROWS, HALF), jnp.uint32),
                            pltpu.VMEM((D_MODEL, TF), BF16), pltpu.VMEM((D_MODEL, TF), BF16),
                            pltpu.SemaphoreType.DMA((2,))]),
        out_shape=jax.ShapeDtypeStruct((N_SB * SB_ROWS, D_EXPERT), BF16),
        compiler_params=_cparams(("arbitrary", "arbitrary"), 48 << 20),
        name="moe_up",
    )(plan["sb_e"], plan["sb_blk"], plan["sb_n"], plan["row_tok"], xp, w_gate, w_up)


def _down_kernel(sbe_ref, sbb_ref, sbn_ref, sbz_ref, h_ref, wd_ref, ys_hbm,
                 wdb_ref, stage_ref, ysem, count_ref):
    del sbe_ref
    sb = pl.program_id(0)
    n = pl.program_id(1)

    @pl.when((sb == 0) & (n == 0))
    def _():
        count_ref[0] = 0

    def out_copy(slot, blk):
        r = pl.multiple_of(blk * MOE_BLOCK, MOE_BLOCK)
        c = pl.multiple_of(n * TN_DOWN, TN_DOWN)
        return pltpu.make_async_copy(stage_ref.at[slot],
                                     ys_hbm.at[pl.ds(r, MOE_BLOCK), pl.ds(c, TN_DOWN)], ysem.at[slot])

    def emit(blk, value):
        k = count_ref[0]
        slot = k % 2

        @pl.when(k >= 2)
        def _():
            out_copy(slot, blk).wait()

        stage_ref[slot] = value()
        out_copy(slot, blk).start()
        count_ref[0] = k + 1

    nsub = sbn_ref[sb]

    @pl.when(nsub > 0)
    def _():
        wdb_ref[...] = wd_ref[...].astype(BF16)

    def sub(i, carry):
        r0 = pl.multiple_of(i * MOE_BLOCK, MOE_BLOCK)
        emit(sbb_ref[sb] + i,
             lambda: jnp.dot(h_ref[pl.ds(r0, MOE_BLOCK), :], wdb_ref[...], preferred_element_type=F32))
        return carry

    lax.fori_loop(0, nsub, sub, 0)

    def zero(i, carry):
        emit(sbb_ref[sb] + i, lambda: jnp.zeros((MOE_BLOCK, TN_DOWN), F32))
        return carry

    lax.fori_loop(0, sbz_ref[sb], zero, 0)

    @pl.when((sb == pl.num_programs(0) - 1) & (n == pl.num_programs(1) - 1))
    def _():
        k = count_ref[0]

        @pl.when(k >= 1)
        def _():
            out_copy((k - 1) % 2, 0).wait()

        @pl.when(k >= 2)
        def _():
            out_copy(k % 2, 0).wait()


def _moe_down(hid, plan, w_down, layer):
    nn = D_MODEL // TN_DOWN

    def w_index(sb, n, sbe, sbb, sbn, sbz):
        return (layer, sbe[sb], 0, jnp.where(sbn[sb] > 0, n, nn - 1))

    return pl.pallas_call(
        _down_kernel,
        grid_spec=pltpu.PrefetchScalarGridSpec(
            num_scalar_prefetch=4,
            grid=(N_SB, nn),
            in_specs=[pl.BlockSpec((SB_ROWS, D_EXPERT), lambda sb, n, *_: (sb, 0)),
                      pl.BlockSpec((None, None, D_EXPERT, TN_DOWN), w_index)],
            out_specs=pl.BlockSpec(memory_space=pl.ANY),
            scratch_shapes=[pltpu.VMEM((D_EXPERT, TN_DOWN), BF16),
                            pltpu.VMEM((2, MOE_BLOCK, TN_DOWN), F32),
                            pltpu.SemaphoreType.DMA((2,)),
                            pltpu.SMEM((1,), I32)]),
        out_shape=jax.ShapeDtypeStruct((N_ROWS, D_MODEL), F32),
        compiler_params=_cparams(("arbitrary", "arbitrary"), 32 << 20),
        name="moe_down",
    )(plan["sb_e"], plan["sb_blk"], plan["sb_n"], plan["sb_z"], hid, w_down)


def _combine_kernel(dest_ref, x_ref, gates_ref, g_ref, ys_hbm, out_a, out_b, buf_ref, sem, *, final):
    i = pl.program_id(0)
    n = pl.num_programs(0)

    def row_copy(step, slot, t, k):
        s = (step * COMB_TM + t) * TOP_K + k
        return pltpu.make_async_copy(ys_hbm.at[pl.ds(dest_ref[s], 1), :],
                                     buf_ref.at[slot, k, pl.ds(t, 1), :], sem.at[slot])

    def start_all(step, slot):
        def body(t, carry):
            for k in range(TOP_K):
                row_copy(step, slot, t, k).start()
            return carry

        lax.fori_loop(0, COMB_TM, body, 0)

    def wait_all(step, slot):
        def body(t, carry):
            for k in range(TOP_K):
                row_copy(step, slot, t, k).wait()
            return carry

        lax.fori_loop(0, COMB_TM, body, 0)

    slot = i % 2

    @pl.when(i == 0)
    def _():
        start_all(i, slot)

    @pl.when(i + 1 < n)
    def _():
        start_all(i + 1, 1 - slot)

    wait_all(i, slot)
    gates = gates_ref[...]
    y = x_ref[...] + gates[:, 0:1] * buf_ref[slot, 0] + gates[:, 1:2] * buf_ref[slot, 1]
    yn = _rms(y, g_ref[...])
    if final:
        @pl.when(i < n - 1)
        def _():
            out_a[...] = yn

        @pl.when(i == n - 1)
        def _():
            out_b[...] = yn
    else:
        out_a[...] = y
        out_b[...] = yn.astype(out_b.dtype)


def _combine(x, ys, dest, gates, g_next, *, final):
    tm = COMB_TM
    assert N_S == tm
    n_p_blocks = N_P // tm
    if final:
        out_specs = [pl.BlockSpec((tm, D_MODEL), lambda i, d: (jnp.minimum(i, n_p_blocks - 1), 0)),
                     pl.BlockSpec((tm, D_MODEL), lambda i, d: (0, 0))]
        out_shape = [jax.ShapeDtypeStruct((N_P, D_MODEL), F32),
                     jax.ShapeDtypeStruct((N_S, D_MODEL), F32)]
    else:
        out_specs = [pl.BlockSpec((tm, D_MODEL), lambda i, d: (i, 0)),
                     pl.BlockSpec((tm, D_MODEL), lambda i, d: (i, 0))]
        out_shape = [jax.ShapeDtypeStruct((N_TOK, D_MODEL), F32),
                     jax.ShapeDtypeStruct((N_TOK, D_MODEL), BF16)]
    return pl.pallas_call(
        functools.partial(_combine_kernel, final=final),
        grid_spec=pltpu.PrefetchScalarGridSpec(
            num_scalar_prefetch=1,
            grid=(N_TOK // tm,),
            in_specs=[pl.BlockSpec((tm, D_MODEL), lambda i, d: (i, 0)),
                      pl.BlockSpec((tm, ROUTER_W), lambda i, d: (i, 0)),
                      pl.BlockSpec((1, D_MODEL), lambda i, d: (0, 0)),
                      pl.BlockSpec(memory_space=pl.ANY)],
            out_specs=out_specs,
            scratch_shapes=[pltpu.VMEM((2, TOP_K, tm, D_MODEL), F32),
                            pltpu.SemaphoreType.DMA((2,))]),
        out_shape=out_shape,
        compiler_params=_cparams(("arbitrary",), 40 << 20),
        name="moe_combine",
    )(dest, x, gates, g_next.reshape(1, -1), ys)


def _dispatch_plan(ids):
    flat_e = ids[:, :TOP_K].reshape(-1)
    onehot = (flat_e[:, None] == jnp.arange(N_EXPERTS, dtype=I32)[None, :]).astype(I32)
    csum = jnp.cumsum(onehot, axis=0)
    rank = jnp.sum(onehot * csum, axis=1) - 1
    counts = csum[-1]
    nblk = (counts + MOE_BLOCK - 1) // MOE_BLOCK
    bend = jnp.cumsum(nblk)
    bstart = bend - nblk
    dest = (jnp.sum(onehot * bstart[None, :], axis=1) * MOE_BLOCK + rank).astype(I32)
    row_tok = jnp.zeros((N_ROWS,), I32).at[dest].set(jnp.arange(N_SLOT, dtype=I32) // TOP_K)

    nsup = (nblk + NSUB - 1) // NSUB
    sup_end = jnp.cumsum(nsup)
    sup_start = sup_end - nsup
    total_sup = sup_end[-1]
    sb = jnp.arange(N_SB, dtype=I32)
    e_of = jnp.minimum(jnp.sum((sup_end[None, :] <= sb[:, None]).astype(I32), axis=1), N_EXPERTS - 1)
    q = sb - sup_start[e_of]
    real = sb < total_sup
    n_real = jnp.clip(nblk[e_of] - NSUB * q, 0, NSUB)
    blk_real = bstart[e_of] + NSUB * q
    blk_tail = bend[-1] + NSUB * (sb - total_sup)
    n_tail = jnp.clip(N_BLOCKS - blk_tail, 0, NSUB)
    e_last = e_of[total_sup - 1]
    return dict(
        dest=dest, row_tok=row_tok,
        sb_e=jnp.where(real, e_of, e_last).astype(I32),
        sb_blk=jnp.where(real, blk_real, jnp.minimum(blk_tail, N_BLOCKS - 1)).astype(I32),
        sb_n=jnp.where(real, n_real, 0).astype(I32),
        sb_z=jnp.where(real, 0, n_tail).astype(I32))


def kernel(x_prompt, x_sample, mem_prompt, state_conv, state_lru_conv, state_lru_h, cache_mem_k, cache_mem_v, norm_mix, w_in, conv_w, conv_b, conv_gn_g, conv_gn_b, lru_conv_w, lru_conv_b, lru_wa, lru_ba, lru_wi, lru_bi, lru_lambda, w_out, norm_attn, norm_mem_kv, w_q, w_k, w_v, w_o, norm_ffn, w_router_g, b_router_g, w_router_e, b_router_e, w_gate, w_up, w_down, norm_final):
    x = jnp.concatenate([x_prompt.reshape(N_P, D_MODEL), x_sample.reshape(N_S, D_MODEL)], axis=0)
    mem = mem_prompt.reshape(BATCH * MEM_LEN, D_MODEL)
    xn = _norm(x, norm_mix[0], tm=TM_NORM)

    conv_p, lruc_p, h_p, mk_p, mv_p, conv_s, lruc_s, h_s = ([] for _ in range(8))
    for l in range(DEPTH):
        z = _mm([xn], w_in, l, tn=512, tm=TM, name="w_in")
        yc, cst = _conv_prompt(z, conv_w[l], conv_b[l], conv_gn_g[l], conv_gn_b[l])
        yl, lst, hl = _lru_prompt(z, lru_conv_w[l], lru_conv_b[l], lru_wa[l], lru_ba[l],
                                  lru_wi[l], lru_bi[l], lru_lambda[l])
        yc, yl, cst_s, lst_s, hl_s = _mixer_sample(
            z, state_conv, state_lru_conv, state_lru_h, l, yc, yl,
            conv_w[l], conv_b[l], conv_gn_g[l], conv_gn_b[l], lru_conv_w[l], lru_conv_b[l],
            lru_wa[l], lru_ba[l], lru_wi[l], lru_bi[l], lru_lambda[l])
        x = _mm([yc, yl], w_out, l, tn=512, tm=TM, res=x, name="w_out")
        conv_p.append(cst); lruc_p.append(lst); h_p.append(hl.reshape(BATCH, C_LRU))
        conv_s.append(cst_s); lruc_s.append(lst_s); h_s.append(hl_s)

        xn = _norm(x, norm_attn[l], tm=TM_NORM)
        q = _mm([xn], w_q, l, tn=512, tm=TM, name="w_q")
        mn = _norm(mem, norm_mem_kv[l], tm=256)
        k_p = _mm([mn], w_k, l, tn=512, tm=BATCH * MEM_LEN, name="w_k")
        v_p = _mm([mn], w_v, l, tn=512, tm=BATCH * MEM_LEN, name="w_v")
        o = _attn_prompt(q, k_p, v_p)
        o = _attn_sample(q, cache_mem_k.reshape(DEPTH, N_S, MEM_LEN, MEM_W),
                         cache_mem_v.reshape(DEPTH, N_S, MEM_LEN, MEM_W), l, o)
        x = _mm([o], w_o, l, tn=1024, tm=TM, res=x, name="w_o")
        mk_p.append(k_p.reshape(BATCH, MEM_LEN, MEM_HEADS, MEM_HD))
        mv_p.append(v_p.reshape(BATCH, MEM_LEN, MEM_HEADS, MEM_HD))

        wr = jnp.concatenate([w_router_g[l], w_router_e[l].reshape(D_MODEL, N_EXPERTS),
                              jnp.zeros((D_MODEL, ROUTER_W - N_GROUPS - N_EXPERTS), F32)],
                             axis=1).astype(BF16)
        br = jnp.concatenate([b_router_g[l], b_router_e[l].reshape(N_EXPERTS),
                              jnp.zeros((ROUTER_W - N_GROUPS - N_EXPERTS,), F32)]).reshape(1, ROUTER_W)
        xp, ids, gates = _norm_router(x, norm_ffn[l], wr, br)
        plan = _dispatch_plan(ids)
        hid = _moe_up(xp, plan, w_gate, w_up, l)
        ys = _moe_down(hid, plan, w_down, l)
        if l < DEPTH - 1:
            x, xn = _combine(x, ys, plan["dest"], gates, norm_mix[l + 1], final=False)
        else:
            y_p, y_s = _combine(x, ys, plan["dest"], gates, norm_final, final=True)

    y_prompt = y_p.reshape(BATCH, SEQ, D_MODEL)
    y_sample = y_s.reshape(DEC_BATCH, 1, D_MODEL)
    return (y_prompt, y_sample, jnp.stack(conv_p), jnp.stack(lruc_p), jnp.stack(h_p),
            jnp.stack(mk_p), jnp.stack(mv_p), jnp.stack(conv_s), jnp.stack(lruc_s), jnp.stack(h_s))
```

```python
import functools
import math

import jax
import jax.numpy as jnp
from jax import lax
from jax.experimental import pallas as pl
from jax.experimental.pallas import tpu as pltpu

F32 = jnp.float32
BF16 = jnp.bfloat16
I32 = jnp.int32

D_MODEL = 4096
BATCH = 4
SEQ = 2048
DEPTH = 2
DEC_BATCH = 128
C_CONV = D_MODEL // 2
C_LRU = D_MODEL // 2
D_IN = 2 * C_CONV + 2 * C_LRU
CONV_GROUPS = 16
GROUP_W = C_CONV // CONV_GROUPS
CONV_K = 31
LRU_HEADS = 16
LRU_HD = C_LRU // LRU_HEADS
LRU_CONV_K = 4
RG_C = 8.0
MEM_LEN = 256
MEM_HEADS = 4
MEM_HD = D_MODEL // 16
MEM_W = MEM_HEADS * MEM_HD
N_GROUPS = 4
EXPERTS_PER_GROUP = 8
N_EXPERTS = N_GROUPS * EXPERTS_PER_GROUP
TOP_K = 2
D_EXPERT = D_MODEL // 4
EPS = 1e-6

N_P = BATCH * SEQ
N_S = DEC_BATCH
N_TOK = N_P + N_S
N_SLOT = N_TOK * TOP_K

LANES = 128
SUBLANES = 8
VMEM_CAP = 56 * 1024 * 1024

TM = 1040
TM_NORM = 416
MOE_BLOCK = 128
N_BLOCKS = -(-(N_SLOT + N_EXPERTS * (MOE_BLOCK - 1)) // MOE_BLOCK)
N_ROWS = N_BLOCKS * MOE_BLOCK
TF = 256
TN_DOWN = 1024
ROUTER_W = LANES
HALF = D_MODEL // 2
CONV_TT = 64
CONV_TC = 256
LRU_TT = 256
COMB_TM = 128
ATT_TQ = 512
S_BT = 16
ATT_S_BT = 4
ISSUE_UNROLL = 8
N_STAGE = 6
NSUB = 8
SB_ROWS = NSUB * MOE_BLOCK
N_SB = N_EXPERTS + 1 + N_BLOCKS // NSUB


def _cparams(sem, vmem_bytes):
    limit = min(VMEM_CAP, max(32 * 1024 * 1024, int(vmem_bytes)))
    return pltpu.CompilerParams(dimension_semantics=sem, vmem_limit_bytes=limit)


def _sigmoid(x):
    return 1.0 / (1.0 + jnp.exp(-x))


def _silu(x):
    return x * _sigmoid(x)


def _gelu_tanh(x):
    c = math.sqrt(2.0 / math.pi)
    return 0.5 * x * (1.0 + jnp.tanh(c * (x + 0.044715 * (x * x * x))))


def _rms(x, g):
    ms = jnp.mean(x * x, axis=-1, keepdims=True)
    return x * lax.rsqrt(ms + EPS) * g


def _norm_kernel(x_ref, g_ref, o_ref):
    o_ref[...] = _rms(x_ref[...], g_ref[...]).astype(o_ref.dtype)


def _norm(x, g, *, tm, out_dtype=BF16, row_block0=0, n_rows=None):
    n_rows = x.shape[0] if n_rows is None else n_rows
    d = x.shape[1]
    return pl.pallas_call(
        _norm_kernel,
        grid=(n_rows // tm,),
        in_specs=[pl.BlockSpec((tm, d), lambda i: (i + row_block0, 0)),
                  pl.BlockSpec((1, d), lambda i: (0, 0))],
        out_specs=pl.BlockSpec((tm, d), lambda i: (i, 0)),
        out_shape=jax.ShapeDtypeStruct((n_rows, d), out_dtype),
        compiler_params=_cparams(("arbitrary",), 6 * tm * d * 4),
        name="rmsnorm",
    )(x, g.reshape(1, d))


def _mm_kernel(*refs, n_a, has_res):
    a_refs = refs[:n_a]
    w_ref = refs[n_a]
    res_ref = refs[n_a + 1] if has_res else None
    o_ref = refs[n_a + 1 + int(has_res)]
    wbf_ref = refs[n_a + 2 + int(has_res)]

    @pl.when(pl.program_id(1) == 0)
    def _():
        wbf_ref[...] = w_ref[...].astype(BF16)

    acc = None
    off = 0
    for a_ref in a_refs:
        k = a_ref.shape[1]
        a = a_ref[...].astype(BF16)
        part = jnp.dot(a, wbf_ref[off:off + k, :], preferred_element_type=F32)
        acc = part if acc is None else acc + part
        off += k
    if has_res:
        acc = acc + res_ref[...]
    o_ref[...] = acc.astype(o_ref.dtype)


def _mm(a_list, w, layer, *, tn, tm, res=None, out_dtype=F32, name="proj"):
    m = a_list[0].shape[0]
    _, k, n = w.shape
    assert sum(a.shape[1] for a in a_list) == k
    in_specs = [pl.BlockSpec((tm, a.shape[1]), lambda j, i: (i, 0)) for a in a_list]
    in_specs.append(pl.BlockSpec((None, k, tn), lambda j, i: (layer, 0, j)))
    args = list(a_list) + [w]
    if res is not None:
        in_specs.append(pl.BlockSpec((tm, tn), lambda j, i: (i, j)))
        args.append(res)
    a_bytes = sum(a.dtype.itemsize * a.shape[1] for a in a_list) * tm
    vmem = 2 * a_bytes + 2 * k * tn * 4 + k * tn * 2 + 6 * tm * tn * 4 + (4 << 20)
    return pl.pallas_call(
        functools.partial(_mm_kernel, n_a=len(a_list), has_res=res is not None),
        grid=(n // tn, m // tm),
        in_specs=in_specs,
        out_specs=pl.BlockSpec((tm, tn), lambda j, i: (i, j)),
        out_shape=jax.ShapeDtypeStruct((m, n), out_dtype),
        scratch_shapes=[pltpu.VMEM((k, tn), BF16)],
        compiler_params=_cparams(("arbitrary", "arbitrary"), vmem),
        name=name,
    )(*args)


def _group_norm_silu(y, gn_g, gn_b, store):
    for g in range(y.shape[1] // GROUP_W):
        sl = slice(g * GROUP_W, (g + 1) * GROUP_W)
        yg = y[:, sl]
        mu = jnp.mean(yg, axis=-1, keepdims=True)
        d = yg - mu
        var = jnp.mean(d * d, axis=-1, keepdims=True)
        yn = d * lax.rsqrt(var + EPS) * gn_g[:, sl] + gn_b[:, sl]
        store(sl, _silu(yn))


CONV_PAD = 32


def _causal_taps(src_ref, w_ref, t0, tt, first_off, n_taps, init):
    acc = init
    for s in range(SUBLANES):
        part = None
        for k in range(n_taps):
            off = first_off + k
            if off % SUBLANES != s:
                continue
            base = pl.multiple_of(t0 + (off - s), SUBLANES)
            term = src_ref[pl.ds(base, tt + SUBLANES), :] * w_ref[k:k + 1, :]
            part = term if part is None else part + term
        if part is not None:
            acc = acc + part[s:s + tt, :]
    return acc


def _last_seq(b):
    return jnp.minimum(b, BATCH - 1)


def _prompt_or_zero(body, y_ref):
    b = pl.program_id(1)

    @pl.when(b < BATCH)
    def _():
        body()

    @pl.when(b == BATCH)
    def _():
        y_ref[...] = jnp.zeros(y_ref.shape, y_ref.dtype)


def _conv_p_kernel(cv_ref, cg_ref, w_ref, b_ref, gng_ref, gnb_ref, y_ref, st_ref, upad_ref):
    _prompt_or_zero(functools.partial(_conv_p_body, cv_ref, cg_ref, w_ref, b_ref, gng_ref, gnb_ref,
                                      y_ref, st_ref, upad_ref), y_ref)


def _conv_p_body(cv_ref, cg_ref, w_ref, b_ref, gng_ref, gnb_ref, y_ref, st_ref, upad_ref):
    t_len = cv_ref.shape[0]
    tc = cv_ref.shape[1]
    upad_ref[0:CONV_PAD, :] = jnp.zeros((CONV_PAD, tc), F32)
    upad_ref[CONV_PAD + t_len:CONV_PAD + t_len + SUBLANES, :] = jnp.zeros((SUBLANES, tc), F32)

    def glu(c, carry):
        t0 = pl.multiple_of(c * LRU_TT, LRU_TT)
        u = cv_ref[pl.ds(t0, LRU_TT), :] * _sigmoid(cg_ref[pl.ds(t0, LRU_TT), :])
        upad_ref[pl.ds(CONV_PAD + t0, LRU_TT), :] = u
        return carry

    lax.fori_loop(0, t_len // LRU_TT, glu, 0)
    st_ref[0] = upad_ref[CONV_PAD + t_len - (CONV_K - 1):CONV_PAD + t_len, :]

    bias = b_ref[...]
    gn_g = gng_ref[...]
    gn_b = gnb_ref[...]
    shift = CONV_PAD - (CONV_K - 1)

    def chunk(c, carry):
        t0 = pl.multiple_of(c * CONV_TT, CONV_TT)
        acc = _causal_taps(upad_ref, w_ref, t0, CONV_TT, shift, CONV_K,
                           jnp.zeros((CONV_TT, tc), F32) + bias)

        def store(sl, v):
            y_ref[pl.ds(t0, CONV_TT), sl] = v.astype(y_ref.dtype)

        _group_norm_silu(acc, gn_g, gn_b, store)
        return carry

    lax.fori_loop(0, t_len // CONV_TT, chunk, 0)


def _conv_prompt(z, conv_w, conv_b, gn_g, gn_b):
    tc = CONV_TC
    nct = C_CONV // tc
    return pl.pallas_call(
        _conv_p_kernel,
        grid=(nct, BATCH + 1),
        in_specs=[pl.BlockSpec((SEQ, tc), lambda c, b: (_last_seq(b), c)),
                  pl.BlockSpec((SEQ, tc), lambda c, b: (_last_seq(b), c + nct)),
                  pl.BlockSpec((CONV_K, tc), lambda c, b: (0, c)),
                  pl.BlockSpec((1, tc), lambda c, b: (0, c)),
                  pl.BlockSpec((1, tc), lambda c, b: (0, c)),
                  pl.BlockSpec((1, tc), lambda c, b: (0, c))],
        out_specs=[pl.BlockSpec((SEQ, tc), lambda c, b: (b, c)),
                   pl.BlockSpec((1, CONV_K - 1, tc), lambda c, b: (_last_seq(b), 0, c))],
        out_shape=[jax.ShapeDtypeStruct((N_TOK, C_CONV), BF16),
                   jax.ShapeDtypeStruct((BATCH, CONV_K - 1, C_CONV), F32)],
        scratch_shapes=[pltpu.VMEM((CONV_PAD + SEQ + SUBLANES, tc), F32)],
        compiler_params=_cparams(("arbitrary", "arbitrary"), 8 * SEQ * tc * 4),
        name="conv_prompt",
    )(z, z, conv_w, conv_b.reshape(1, -1), gn_g.reshape(1, -1), gn_b.reshape(1, -1))


def _log_sigmoid(x):
    return -(jnp.maximum(-x, 0.0) + jnp.log(1.0 + jnp.exp(-jnp.abs(x))))


def _lru_gates(xc, wa, ba, wi, bi, log_sig_lam):
    xb = xc.astype(BF16)
    r = _sigmoid(jnp.dot(xb, wa, preferred_element_type=F32) + ba)
    i = _sigmoid(jnp.dot(xb, wi, preferred_element_type=F32) + bi)
    a = jnp.exp(RG_C * r * log_sig_lam)
    return a, jnp.sqrt(1.0 - a * a) * (i * xc)


LRU_PAD = 8


def _lru_p_kernel(*refs):
    _prompt_or_zero(functools.partial(_lru_p_body, *refs), refs[9])


def _lru_p_body(lx_ref, lg_ref, cw_ref, cb_ref, wa_ref, ba_ref, wi_ref, bi_ref, lam_ref,
                y_ref, buf_ref, h_ref, xpad_ref, a_ref, b_ref):
    t_len = lx_ref.shape[0]
    tc = lx_ref.shape[1]
    n_heads = tc // LRU_HD
    xpad_ref[0:LRU_PAD, :] = jnp.zeros((LRU_PAD, tc), F32)
    xpad_ref[LRU_PAD + t_len:LRU_PAD + t_len + SUBLANES, :] = jnp.zeros((SUBLANES, tc), F32)

    def copy_in(c, carry):
        t0 = pl.multiple_of(c * LRU_TT, LRU_TT)
        xpad_ref[pl.ds(LRU_PAD + t0, LRU_TT), :] = lx_ref[pl.ds(t0, LRU_TT), :]
        return carry

    lax.fori_loop(0, t_len // LRU_TT, copy_in, 0)
    buf_ref[0] = lx_ref[t_len - (LRU_CONV_K - 1):t_len, :]

    cb = cb_ref[...]
    log_sig_lam = _log_sigmoid(lam_ref[...])
    shift = LRU_PAD - (LRU_CONV_K - 1)

    def gates(c, carry):
        t0 = pl.multiple_of(c * LRU_TT, LRU_TT)
        xc = _causal_taps(xpad_ref, cw_ref, t0, LRU_TT, shift, LRU_CONV_K,
                          jnp.zeros((LRU_TT, tc), F32) + cb)
        for hh in range(n_heads):
            sl = slice(hh * LRU_HD, (hh + 1) * LRU_HD)
            a, b = _lru_gates(xc[:, sl], wa_ref[hh].astype(BF16), ba_ref[:, sl],
                              wi_ref[hh].astype(BF16), bi_ref[:, sl], log_sig_lam[:, sl])
            a_ref[pl.ds(t0, LRU_TT), sl] = a
            b_ref[pl.ds(t0, LRU_TT), sl] = b
        return carry

    lax.fori_loop(0, t_len // LRU_TT, gates, 0)

    row = lax.broadcasted_iota(I32, (SUBLANES, tc), 0)

    def scan(i, h):
        t0 = pl.multiple_of(i * SUBLANES, SUBLANES)
        a = a_ref[pl.ds(t0, SUBLANES), :]
        b = b_ref[pl.ds(t0, SUBLANES), :]
        for d in (1, 2, 4):
            a_sh = pltpu.roll(a, d, axis=0)
            b_sh = pltpu.roll(b, d, axis=0)
            m = row >= d
            b = jnp.where(m, a * b_sh + b, b)
            a = jnp.where(m, a * a_sh, a)
        hb = a * h + b
        b_ref[pl.ds(t0, SUBLANES), :] = hb
        return hb[SUBLANES - 1:SUBLANES, :]

    h_last = lax.fori_loop(0, t_len // SUBLANES, scan, jnp.zeros((1, tc), F32), unroll=4)
    h_ref[0] = h_last

    def gate_out(c, carry):
        t0 = pl.multiple_of(c * LRU_TT, LRU_TT)
        y = b_ref[pl.ds(t0, LRU_TT), :] * _gelu_tanh(lg_ref[pl.ds(t0, LRU_TT), :])
        y_ref[pl.ds(t0, LRU_TT), :] = y.astype(y_ref.dtype)
        return carry

    lax.fori_loop(0, t_len // LRU_TT, gate_out, 0)


def _lru_prompt(z, cw, cb, wa, ba, wi, bi, lam):
    tc = CONV_TC
    nct = C_LRU // tc
    hpb = tc // LRU_HD
    col0 = 2 * C_CONV // tc
    vec = lambda v: v.reshape(1, -1)
    return pl.pallas_call(
        _lru_p_kernel,
        grid=(nct, BATCH + 1),
        in_specs=[pl.BlockSpec((SEQ, tc), lambda c, b: (_last_seq(b), c + col0)),
                  pl.BlockSpec((SEQ, tc), lambda c, b: (_last_seq(b), c + col0 + nct)),
                  pl.BlockSpec((LRU_CONV_K, tc), lambda c, b: (0, c)),
                  pl.BlockSpec((1, tc), lambda c, b: (0, c)),
                  pl.BlockSpec((hpb, LRU_HD, LRU_HD), lambda c, b: (c, 0, 0)),
                  pl.BlockSpec((1, tc), lambda c, b: (0, c)),
                  pl.BlockSpec((hpb, LRU_HD, LRU_HD), lambda c, b: (c, 0, 0)),
                  pl.BlockSpec((1, tc), lambda c, b: (0, c)),
                  pl.BlockSpec((1, tc), lambda c, b: (0, c))],
        out_specs=[pl.BlockSpec((SEQ, tc), lambda c, b: (b, c)),
                   pl.BlockSpec((1, LRU_CONV_K - 1, tc), lambda c, b: (_last_seq(b), 0, c)),
                   pl.BlockSpec((1, 1, tc), lambda c, b: (_last_seq(b), 0, c))],
        out_shape=[jax.ShapeDtypeStruct((N_TOK, C_LRU), BF16),
                   jax.ShapeDtypeStruct((BATCH, LRU_CONV_K - 1, C_LRU), F32),
                   jax.ShapeDtypeStruct((BATCH, 1, C_LRU), F32)],
        scratch_shapes=[pltpu.VMEM((LRU_PAD + SEQ + SUBLANES, tc), F32),
                        pltpu.VMEM((SEQ, tc), F32),
                        pltpu.VMEM((SEQ, tc), F32)],
        compiler_params=_cparams(("arbitrary", "arbitrary"), 12 * SEQ * tc * 4),
        name="lru_prompt",
    )(z, z, cw, vec(cb), wa, vec(ba), wi, vec(bi), vec(lam))


def _mixer_s_kernel(cv_ref, cg_ref, lx_ref, lg_ref, st_ref, lst_ref, h0_ref,
                    cw_ref, cb_ref, gng_ref, gnb_ref, lcw_ref, lcb_ref,
                    wa_ref, ba_ref, wi_ref, bi_ref, lam_ref, yc_in, yl_in,
                    yc_ref, yl_ref, nst_ref, nlst_ref, nh_ref):
    del yc_in, yl_in
    u = cv_ref[...] * _sigmoid(cg_ref[...])
    acc = u * cw_ref[CONV_K - 1:CONV_K, :] + cb_ref[...]
    for k in range(CONV_K - 1):
        row = st_ref[:, k, :]
        acc = acc + row * cw_ref[k:k + 1, :]
        if k > 0:
            nst_ref[:, k - 1, :] = row
    nst_ref[:, CONV_K - 2, :] = u

    def store_c(sl, v):
        yc_ref[:, sl] = v.astype(yc_ref.dtype)

    _group_norm_silu(acc, gng_ref[...], gnb_ref[...], store_c)

    lx = lx_ref[...]
    xc = lx * lcw_ref[LRU_CONV_K - 1:LRU_CONV_K, :] + lcb_ref[...]
    for k in range(LRU_CONV_K - 1):
        row = lst_ref[:, k, :]
        xc = xc + row * lcw_ref[k:k + 1, :]
        if k > 0:
            nlst_ref[:, k - 1, :] = row
    nlst_ref[:, LRU_CONV_K - 2, :] = lx

    log_sig_lam = _log_sigmoid(lam_ref[...])
    for hh in range(LRU_HEADS):
        sl = slice(hh * LRU_HD, (hh + 1) * LRU_HD)
        a, b = _lru_gates(xc[:, sl], wa_ref[hh].astype(BF16), ba_ref[:, sl],
                          wi_ref[hh].astype(BF16), bi_ref[:, sl], log_sig_lam[:, sl])
        h = a * h0_ref[:, sl] + b
        nh_ref[:, sl] = h
        yl_ref[:, sl] = (h * _gelu_tanh(lg_ref[:, sl])).astype(yl_ref.dtype)


def _mixer_sample(z, st, lst, h0, layer, yc, yl, cw, cb, gn_g, gn_b, lcw, lcb, wa, ba, wi, bi, lam):
    bt = S_BT
    rb0 = N_P // bt
    vec = lambda v: v.reshape(1, -1)
    zspec = lambda col: pl.BlockSpec((bt, C_CONV), lambda i: (i + rb0, col))
    full = lambda shape: pl.BlockSpec(shape, lambda i: (0,) * len(shape))
    any_spec = pl.BlockSpec(memory_space=pl.ANY)
    return pl.pallas_call(
        _mixer_s_kernel,
        grid=(N_S // bt,),
        in_specs=[zspec(0), zspec(1), zspec(2), zspec(3),
                  pl.BlockSpec((None, bt, CONV_K - 1, C_CONV), lambda i: (layer, i, 0, 0)),
                  pl.BlockSpec((None, bt, LRU_CONV_K - 1, C_LRU), lambda i: (layer, i, 0, 0)),
                  pl.BlockSpec((None, bt, C_LRU), lambda i: (layer, i, 0)),
                  full((CONV_K, C_CONV)), full((1, C_CONV)), full((1, C_CONV)), full((1, C_CONV)),
                  full((LRU_CONV_K, C_LRU)), full((1, C_LRU)),
                  full((LRU_HEADS, LRU_HD, LRU_HD)), full((1, C_LRU)),
                  full((LRU_HEADS, LRU_HD, LRU_HD)), full((1, C_LRU)), full((1, C_LRU)),
                  any_spec, any_spec],
        out_specs=[pl.BlockSpec((bt, C_CONV), lambda i: (i + rb0, 0)),
                   pl.BlockSpec((bt, C_LRU), lambda i: (i + rb0, 0)),
                   pl.BlockSpec((bt, CONV_K - 1, C_CONV), lambda i: (i, 0, 0)),
                   pl.BlockSpec((bt, LRU_CONV_K - 1, C_LRU), lambda i: (i, 0, 0)),
                   pl.BlockSpec((bt, C_LRU), lambda i: (i, 0))],
        out_shape=[jax.ShapeDtypeStruct(yc.shape, yc.dtype),
                   jax.ShapeDtypeStruct(yl.shape, yl.dtype),
                   jax.ShapeDtypeStruct(st.shape[1:], F32),
                   jax.ShapeDtypeStruct(lst.shape[1:], F32),
                   jax.ShapeDtypeStruct(h0.shape[1:], F32)],
        input_output_aliases={18: 0, 19: 1},
        compiler_params=_cparams(("arbitrary",), 48 << 20),
        name="mixer_sample",
    )(z, z, z, z, st, lst, h0, cw, vec(cb), vec(gn_g), vec(gn_b), lcw, vec(lcb),
      wa, vec(ba), wi, vec(bi), vec(lam), yc, yl)


def _attn_p_kernel(q_ref, k_ref, v_ref, o_ref):
    i = pl.program_id(0)

    @pl.when(i < N_P // ATT_TQ)
    def _():
        _attn_p_body(q_ref, k_ref, v_ref, o_ref)

    @pl.when(i == N_P // ATT_TQ)
    def _():
        o_ref[...] = jnp.zeros(o_ref.shape, o_ref.dtype)


def _attn_p_body(q_ref, k_ref, v_ref, o_ref):
    scale = MEM_HD ** -0.5
    for h in range(MEM_HEADS):
        sl = slice(h * MEM_HD, (h + 1) * MEM_HD)
        q = q_ref[:, sl].astype(BF16)
        k = k_ref[:, sl].astype(BF16)
        v = v_ref[:, sl].astype(BF16)
        s = lax.dot_general(q, k, (((1,), (1,)), ((), ())), preferred_element_type=F32) * scale
        p = jnp.exp(s - jnp.max(s, axis=-1, keepdims=True))
        l = jnp.sum(p, axis=-1, keepdims=True)
        o = jnp.dot(p.astype(BF16), v, preferred_element_type=F32) / l
        o_ref[:, sl] = o.astype(o_ref.dtype)


def _attn_prompt(q, k, v):
    nq = SEQ // ATT_TQ
    return pl.pallas_call(
        _attn_p_kernel,
        grid=(BATCH * nq + 1,),
        in_specs=[pl.BlockSpec((ATT_TQ, MEM_W), lambda i: (i, 0)),
                  pl.BlockSpec((MEM_LEN, MEM_W), lambda i: (_last_seq(i // nq), 0)),
                  pl.BlockSpec((MEM_LEN, MEM_W), lambda i: (_last_seq(i // nq), 0))],
        out_specs=pl.BlockSpec((ATT_TQ, MEM_W), lambda i: (i, 0)),
        out_shape=jax.ShapeDtypeStruct((N_TOK, MEM_W), BF16),
        compiler_params=_cparams(("arbitrary",), 32 << 20),
        name="attn_prompt",
    )(q, k, v)


def _attn_s_kernel(q_ref, k_ref, v_ref, o_in, o_ref, stage_ref):
    del o_in
    jj = pl.program_id(1)
    scale = MEM_HD ** -0.5
    for bb in range(ATT_S_BT):
        r = jj * ATT_S_BT + bb
        q = q_ref[pl.ds(r, 1)]
        s = jnp.sum(k_ref[bb] * q, axis=-1, keepdims=True) * scale
        p = jnp.exp(s - jnp.max(s, axis=0, keepdims=True))
        l = jnp.sum(p, axis=0)
        o = jnp.sum(p * v_ref[bb], axis=0) / l
        for h in range(MEM_HEADS):
            stage_ref[pl.ds(r, 1), h * MEM_HD:(h + 1) * MEM_HD] = o[h:h + 1, :]

    @pl.when(jj == pl.num_programs(1) - 1)
    def _():
        o_ref[...] = stage_ref[...].astype(o_ref.dtype)


def _attn_sample(q4, kc, vc, layer, o):
    bt = S_BT
    inner = bt // ATT_S_BT
    rb0 = N_P // bt
    cache_spec = pl.BlockSpec((None, ATT_S_BT, MEM_LEN, MEM_HEADS, MEM_HD),
                              lambda i, j: (layer, i * inner + j, 0, 0, 0))
    return pl.pallas_call(
        _attn_s_kernel,
        grid=(N_S // bt, inner),
        in_specs=[pl.BlockSpec((bt, MEM_HEADS, MEM_HD), lambda i, j: (i, 0, 0)),
                  cache_spec, cache_spec,
                  pl.BlockSpec(memory_space=pl.ANY)],
        out_specs=pl.BlockSpec((bt, MEM_W), lambda i, j: (i + rb0, 0)),
        out_shape=jax.ShapeDtypeStruct(o.shape, o.dtype),
        scratch_shapes=[pltpu.VMEM((bt, MEM_W), F32)],
        input_output_aliases={3: 0},
        compiler_params=_cparams(("arbitrary", "arbitrary"), 48 << 20),
        name="attn_sample",
    )(q4, kc, vc, o)


def _norm_router_kernel(x_ref, g_ref, wr_ref, br_ref, xp_ref, ids_ref, gates_ref):
    xn = _rms(x_ref[...], g_ref[...])
    xp_ref[...] = pltpu.pack_elementwise([xn[:, :HALF], xn[:, HALF:]], packed_dtype=BF16)
    logits = jnp.dot(xn.astype(BF16), wr_ref[...], preferred_element_type=F32) + br_ref[...]
    lane = lax.broadcasted_iota(I32, logits.shape, 1)
    neg = jnp.float32(-jnp.inf)

    def first_max(vals):
        m = jnp.max(vals, axis=-1, keepdims=True)
        idx = jnp.min(jnp.where(vals == m, lane, ROUTER_W), axis=-1, keepdims=True)
        return m, idx

    is_group = lane < N_GROUPS
    g_max, g_sel = first_max(jnp.where(is_group, logits, neg))
    p_g = 1.0 / jnp.sum(jnp.where(is_group, jnp.exp(logits - g_max), 0.0), axis=-1, keepdims=True)
    lo = N_GROUPS + EXPERTS_PER_GROUP * g_sel
    in_group = (lane >= lo) & (lane < lo + EXPERTS_PER_GROUP)
    e_logits = jnp.where(in_group, logits, neg)
    t1, i1 = first_max(e_logits)
    t2, i2 = first_max(jnp.where(lane == i1, neg, e_logits))
    e = jnp.exp(t2 - t1)
    w1 = p_g / (1.0 + e)
    w2 = p_g * e / (1.0 + e)
    ids_ref[...] = jnp.where(lane == 0, i1 - N_GROUPS, jnp.where(lane == 1, i2 - N_GROUPS, 0))
    gates_ref[...] = jnp.where(lane == 0, w1, jnp.where(lane == 1, w2, 0.0))


def _norm_router(x, g, wr, br):
    tm = TM_NORM
    return pl.pallas_call(
        _norm_router_kernel,
        grid=(N_TOK // tm,),
        in_specs=[pl.BlockSpec((tm, D_MODEL), lambda i: (i, 0)),
                  pl.BlockSpec((1, D_MODEL), lambda i: (0, 0)),
                  pl.BlockSpec((D_MODEL, ROUTER_W), lambda i: (0, 0)),
                  pl.BlockSpec((1, ROUTER_W), lambda i: (0, 0))],
        out_specs=[pl.BlockSpec((tm, HALF), lambda i: (i, 0)),
                   pl.BlockSpec((tm, ROUTER_W), lambda i: (i, 0)),
                   pl.BlockSpec((tm, ROUTER_W), lambda i: (i, 0))],
        out_shape=[jax.ShapeDtypeStruct((N_TOK, HALF), jnp.uint32),
                   jax.ShapeDtypeStruct((N_TOK, ROUTER_W), I32),
                   jax.ShapeDtypeStruct((N_TOK, ROUTER_W), F32)],
        compiler_params=_cparams(("arbitrary",), 8 * tm * D_MODEL * 4),
        name="norm_router",
    )(x, g.reshape(1, -1), wr, br)


def _unpack_x(xp):
    lo = pltpu.unpack_elementwise(xp, index=0, packed_dtype=BF16, unpacked_dtype=F32)
    hi = pltpu.unpack_elementwise(xp, index=1, packed_dtype=BF16, unpacked_dtype=F32)
    return lo.astype(BF16), hi.astype(BF16)


def _up_kernel(sbe_ref, sbb_ref, sbn_ref, tok_ref, xp_hbm, wg_ref, wu_ref, h_ref,
               xbuf_ref, wgb_ref, wub_ref, xsem):
    del sbe_ref
    sb = pl.program_id(0)
    j = pl.program_id(1)
    slot = sb % 2

    def row_copy(s, slot_, r):
        tok = tok_ref[sbb_ref[s] * MOE_BLOCK + r]
        return pltpu.make_async_copy(xp_hbm.at[pl.ds(tok, 1), :],
                                     xbuf_ref.at[slot_, pl.ds(r, 1), :], xsem.at[slot_])

    def start_blocks(s, slot_, lo, hi):
        def block(i, carry):
            def body(r, c):
                row_copy(s, slot_, i * MOE_BLOCK + r).start()
                return c

            return lax.fori_loop(0, MOE_BLOCK, body, carry, unroll=ISSUE_UNROLL)

        lax.fori_loop(lo, jnp.minimum(hi, sbn_ref[s]), block, 0)

    def wait_rows(s, slot_):
        def block(i, carry):
            r0 = pl.multiple_of(i * MOE_BLOCK, MOE_BLOCK)
            pltpu.make_async_copy(xp_hbm.at[pl.ds(0, MOE_BLOCK), :],
                                  xbuf_ref.at[slot_, pl.ds(r0, MOE_BLOCK), :], xsem.at[slot_]).wait()
            return carry

        lax.fori_loop(0, sbn_ref[s], block, 0)

    per_j = NSUB // (D_EXPERT // TF)

    @pl.when((sb == 0) & (j == 0))
    def _():
        start_blocks(sb, slot, 0, NSUB)

    @pl.when(sb + 1 < pl.num_programs(0))
    def _():
        start_blocks(sb + 1, 1 - slot, j * per_j, (j + 1) * per_j)

    @pl.when(j == 0)
    def _():
        wait_rows(sb, slot)

    nsub = sbn_ref[sb]

    @pl.when(nsub > 0)
    def _():
        wgb_ref[...] = wg_ref[...].astype(BF16)
        wub_ref[...] = wu_ref[...].astype(BF16)

    def sub(i, carry):
        r0 = pl.multiple_of(i * MOE_BLOCK, MOE_BLOCK)
        lo, hi = _unpack_x(xbuf_ref[slot, pl.ds(r0, MOE_BLOCK), :])
        g = (jnp.dot(lo, wgb_ref[:HALF, :], preferred_element_type=F32)
             + jnp.dot(hi, wgb_ref[HALF:, :], preferred_element_type=F32))
        u = (jnp.dot(lo, wub_ref[:HALF, :], preferred_element_type=F32)
             + jnp.dot(hi, wub_ref[HALF:, :], preferred_element_type=F32))
        h_ref[pl.ds(r0, MOE_BLOCK), :] = (_silu(g) * u).astype(h_ref.dtype)
        return carry

    lax.fori_loop(0, nsub, sub, 0)

    def zero(i, carry):
        r0 = pl.multiple_of(i * MOE_BLOCK, MOE_BLOCK)
        h_ref[pl.ds(r0, MOE_BLOCK), :] = jnp.zeros((MOE_BLOCK, h_ref.shape[1]), h_ref.dtype)
        return carry

    lax.fori_loop(nsub, NSUB, zero, 0)


def _moe_up(xp, plan, w_gate, w_up, layer):
    nj = D_EXPERT // TF

    def w_index(sb, j, sbe, sbb, sbn, tok):
        return (layer, sbe[sb], 0, jnp.where(sbn[sb] > 0, j, nj - 1))

    wspec = pl.BlockSpec((None, None, D_MODEL, TF), w_index)
    return pl.pallas_call(
        _up_kernel,
        grid_spec=pltpu.PrefetchScalarGridSpec(
            num_scalar_prefetch=4,
            grid=(N_SB, nj),
            in_specs=[pl.BlockSpec(memory_space=pl.ANY), wspec, wspec],
            out_specs=pl.BlockSpec((SB_ROWS, TF), lambda sb, j, *_: (sb, j)),
            scratch_shapes=[pltpu.VMEM((2, SB_ROWS, HALF), jnp.uint32),
                            pltpu.VMEM((D_MODEL, TF), BF16), pltpu.VMEM((D_MODEL, TF), BF16),
                            pltpu.SemaphoreType.DMA((2,))]),
        out_shape=jax.ShapeDtypeStruct((N_SB * SB_ROWS, D_EXPERT), BF16),
        compiler_params=_cparams(("arbitrary", "arbitrary"), 48 << 20),
        name="moe_up",
    )(plan["sb_e"], plan["sb_blk"], plan["sb_n"], plan["row_tok"], xp, w_gate, w_up)


def _down_kernel(sbe_ref, sbb_ref, sbn_ref, sbz_ref, h_ref, wd_ref, ys_hbm,
                 wdb_ref, stage_ref, ysem, count_ref):
    del sbe_ref
    sb = pl.program_id(0)
    n = pl.program_id(1)

    @pl.when((sb == 0) & (n == 0))
    def _():
        count_ref[0] = 0

    def out_copy(slot, blk):
        r = pl.multiple_of(blk * MOE_BLOCK, MOE_BLOCK)
        c = pl.multiple_of(n * TN_DOWN, TN_DOWN)
        return pltpu.make_async_copy(stage_ref.at[slot],
                                     ys_hbm.at[pl.ds(r, MOE_BLOCK), pl.ds(c, TN_DOWN)], ysem.at[slot])

    def emit(blk, value):
        k = count_ref[0]
        slot = k % N_STAGE

        @pl.when(k >= N_STAGE)
        def _():
            out_copy(slot, blk).wait()

        stage_ref[slot] = value()
        out_copy(slot, blk).start()
        count_ref[0] = k + 1

    nsub = sbn_ref[sb]

    @pl.when(nsub > 0)
    def _():
        wdb_ref[...] = wd_ref[...].astype(BF16)

    def sub(i, carry):
        r0 = pl.multiple_of(i * MOE_BLOCK, MOE_BLOCK)
        emit(sbb_ref[sb] + i,
             lambda: jnp.dot(h_ref[pl.ds(r0, MOE_BLOCK), :], wdb_ref[...], preferred_element_type=F32))
        return carry

    lax.fori_loop(0, nsub, sub, 0)

    def zero(i, carry):
        emit(sbb_ref[sb] + i, lambda: jnp.zeros((MOE_BLOCK, TN_DOWN), F32))
        return carry

    lax.fori_loop(0, sbz_ref[sb], zero, 0)

    @pl.when((sb == pl.num_programs(0) - 1) & (n == pl.num_programs(1) - 1))
    def _():
        k = count_ref[0]
        for back in range(1, N_STAGE + 1):
            @pl.when(k >= back)
            def _():
                out_copy((k - back) % N_STAGE, 0).wait()


def _moe_down(hid, plan, w_down, layer):
    nn = D_MODEL // TN_DOWN

    def w_index(sb, n, sbe, sbb, sbn, sbz):
        return (layer, sbe[sb], 0, jnp.where(sbn[sb] > 0, n, nn - 1))

    return pl.pallas_call(
        _down_kernel,
        grid_spec=pltpu.PrefetchScalarGridSpec(
            num_scalar_prefetch=4,
            grid=(N_SB, nn),
            in_specs=[pl.BlockSpec((SB_ROWS, D_EXPERT), lambda sb, n, *_: (sb, 0)),
                      pl.BlockSpec((None, None, D_EXPERT, TN_DOWN), w_index)],
            out_specs=pl.BlockSpec(memory_space=pl.ANY),
            scratch_shapes=[pltpu.VMEM((D_EXPERT, TN_DOWN), BF16),
                            pltpu.VMEM((N_STAGE, MOE_BLOCK, TN_DOWN), F32),
                            pltpu.SemaphoreType.DMA((N_STAGE,)),
                            pltpu.SMEM((1,), I32)]),
        out_shape=jax.ShapeDtypeStruct((N_ROWS, D_MODEL), F32),
        compiler_params=_cparams(("arbitrary", "arbitrary"), 32 << 20),
        name="moe_down",
    )(plan["sb_e"], plan["sb_blk"], plan["sb_n"], plan["sb_z"], hid, w_down)


def _combine_kernel(dest_ref, x_ref, gates_ref, g_ref, ys_hbm, out_a, out_b, buf_ref, sem, *, final):
    i = pl.program_id(0)
    n = pl.num_programs(0)

    def row_copy(step, slot, t, k):
        s = (step * COMB_TM + t) * TOP_K + k
        return pltpu.make_async_copy(ys_hbm.at[pl.ds(dest_ref[s], 1), :],
                                     buf_ref.at[slot, k, pl.ds(t, 1), :], sem.at[slot])

    def start_all(step, slot):
        def body(t, carry):
            for k in range(TOP_K):
                row_copy(step, slot, t, k).start()
            return carry

        lax.fori_loop(0, COMB_TM, body, 0, unroll=ISSUE_UNROLL)

    def wait_all(step, slot):
        for k in range(TOP_K):
            pltpu.make_async_copy(ys_hbm.at[pl.ds(0, COMB_TM), :], buf_ref.at[slot, k], sem.at[slot]).wait()

    slot = i % 2

    @pl.when(i == 0)
    def _():
        start_all(i, slot)

    @pl.when(i + 1 < n)
    def _():
        start_all(i + 1, 1 - slot)

    wait_all(i, slot)
    gates = gates_ref[...]
    y = x_ref[...] + gates[:, 0:1] * buf_ref[slot, 0] + gates[:, 1:2] * buf_ref[slot, 1]
    yn = _rms(y, g_ref[...])
    if final:
        @pl.when(i < n - 1)
        def _():
            out_a[...] = yn

        @pl.when(i == n - 1)
        def _():
            out_b[...] = yn
    else:
        out_a[...] = y
        out_b[...] = yn.astype(out_b.dtype)


def _combine(x, ys, dest, gates, g_next, *, final):
    tm = COMB_TM
    assert N_S == tm
    n_p_blocks = N_P // tm
    if final:
        out_specs = [pl.BlockSpec((tm, D_MODEL), lambda i, d: (jnp.minimum(i, n_p_blocks - 1), 0)),
                     pl.BlockSpec((tm, D_MODEL), lambda i, d: (0, 0))]
        out_shape = [jax.ShapeDtypeStruct((N_P, D_MODEL), F32),
                     jax.ShapeDtypeStruct((N_S, D_MODEL), F32)]
    else:
        out_specs = [pl.BlockSpec((tm, D_MODEL), lambda i, d: (i, 0)),
                     pl.BlockSpec((tm, D_MODEL), lambda i, d: (i, 0))]
        out_shape = [jax.ShapeDtypeStruct((N_TOK, D_MODEL), F32),
                     jax.ShapeDtypeStruct((N_TOK, D_MODEL), BF16)]
    return pl.pallas_call(
        functools.partial(_combine_kernel, final=final),
        grid_spec=pltpu.PrefetchScalarGridSpec(
            num_scalar_prefetch=1,
            grid=(N_TOK // tm,),
            in_specs=[pl.BlockSpec((tm, D_MODEL), lambda i, d: (i, 0)),
                      pl.BlockSpec((tm, ROUTER_W), lambda i, d: (i, 0)),
                      pl.BlockSpec((1, D_MODEL), lambda i, d: (0, 0)),
                      pl.BlockSpec(memory_space=pl.ANY)],
            out_specs=out_specs,
            scratch_shapes=[pltpu.VMEM((2, TOP_K, tm, D_MODEL), F32),
                            pltpu.SemaphoreType.DMA((2,))]),
        out_shape=out_shape,
        compiler_params=_cparams(("arbitrary",), 40 << 20),
        name="moe_combine",
    )(dest, x, gates, g_next.reshape(1, -1), ys)


def _dispatch_plan(ids):
    flat_e = ids[:, :TOP_K].reshape(-1)
    onehot = (flat_e[:, None] == jnp.arange(N_EXPERTS, dtype=I32)[None, :]).astype(I32)
    csum = jnp.cumsum(onehot, axis=0)
    rank = jnp.sum(onehot * csum, axis=1) - 1
    counts = csum[-1]
    nblk = (counts + MOE_BLOCK - 1) // MOE_BLOCK
    bend = jnp.cumsum(nblk)
    bstart = bend - nblk
    dest = (jnp.sum(onehot * bstart[None, :], axis=1) * MOE_BLOCK + rank).astype(I32)
    row_tok = jnp.zeros((N_ROWS,), I32).at[dest].set(jnp.arange(N_SLOT, dtype=I32) // TOP_K)

    nsup = (nblk + NSUB - 1) // NSUB
    sup_end = jnp.cumsum(nsup)
    sup_start = sup_end - nsup
    total_sup = sup_end[-1]
    sb = jnp.arange(N_SB, dtype=I32)
    e_of = jnp.minimum(jnp.sum((sup_end[None, :] <= sb[:, None]).astype(I32), axis=1), N_EXPERTS - 1)
    q = sb - sup_start[e_of]
    real = sb < total_sup
    n_real = jnp.clip(nblk[e_of] - NSUB * q, 0, NSUB)
    blk_real = bstart[e_of] + NSUB * q
    blk_tail = bend[-1] + NSUB * (sb - total_sup)
    n_tail = jnp.clip(N_BLOCKS - blk_tail, 0, NSUB)
    e_last = e_of[total_sup - 1]
    return dict(
        dest=dest, row_tok=row_tok,
        sb_e=jnp.where(real, e_of, e_last).astype(I32),
        sb_blk=jnp.where(real, blk_real, jnp.minimum(blk_tail, N_BLOCKS - 1)).astype(I32),
        sb_n=jnp.where(real, n_real, 0).astype(I32),
        sb_z=jnp.where(real, 0, n_tail).astype(I32))


def kernel(x_prompt, x_sample, mem_prompt, state_conv, state_lru_conv, state_lru_h, cache_mem_k, cache_mem_v, norm_mix, w_in, conv_w, conv_b, conv_gn_g, conv_gn_b, lru_conv_w, lru_conv_b, lru_wa, lru_ba, lru_wi, lru_bi, lru_lambda, w_out, norm_attn, norm_mem_kv, w_q, w_k, w_v, w_o, norm_ffn, w_router_g, b_router_g, w_router_e, b_router_e, w_gate, w_up, w_down, norm_final):
    x = jnp.concatenate([x_prompt.reshape(N_P, D_MODEL), x_sample.reshape(N_S, D_MODEL)], axis=0)
    mem = mem_prompt.reshape(BATCH * MEM_LEN, D_MODEL)
    xn = _norm(x, norm_mix[0], tm=TM_NORM)

    conv_p, lruc_p, h_p, mk_p, mv_p, conv_s, lruc_s, h_s = ([] for _ in range(8))
    for l in range(DEPTH):
        z = _mm([xn], w_in, l, tn=512, tm=TM, name="w_in")
        yc, cst = _conv_prompt(z, conv_w[l], conv_b[l], conv_gn_g[l], conv_gn_b[l])
        yl, lst, hl = _lru_prompt(z, lru_conv_w[l], lru_conv_b[l], lru_wa[l], lru_ba[l],
                                  lru_wi[l], lru_bi[l], lru_lambda[l])
        yc, yl, cst_s, lst_s, hl_s = _mixer_sample(
            z, state_conv, state_lru_conv, state_lru_h, l, yc, yl,
            conv_w[l], conv_b[l], conv_gn_g[l], conv_gn_b[l], lru_conv_w[l], lru_conv_b[l],
            lru_wa[l], lru_ba[l], lru_wi[l], lru_bi[l], lru_lambda[l])
        x = _mm([yc, yl], w_out, l, tn=512, tm=TM, res=x, name="w_out")
        conv_p.append(cst); lruc_p.append(lst); h_p.append(hl.reshape(BATCH, C_LRU))
        conv_s.append(cst_s); lruc_s.append(lst_s); h_s.append(hl_s)

        xn = _norm(x, norm_attn[l], tm=TM_NORM)
        q = _mm([xn], w_q, l, tn=512, tm=TM, name="w_q")
        mn = _norm(mem, norm_mem_kv[l], tm=256)
        k_p = _mm([mn], w_k, l, tn=512, tm=BATCH * MEM_LEN, name="w_k")
        v_p = _mm([mn], w_v, l, tn=512, tm=BATCH * MEM_LEN, name="w_v")
        o = _attn_prompt(q, k_p, v_p)
        o = _attn_sample(q[N_P:].reshape(N_S, MEM_HEADS, MEM_HD), cache_mem_k, cache_mem_v, l, o)
        x = _mm([o], w_o, l, tn=1024, tm=TM, res=x, name="w_o")
        mk_p.append(k_p.reshape(BATCH, MEM_LEN, MEM_HEADS, MEM_HD))
        mv_p.append(v_p.reshape(BATCH, MEM_LEN, MEM_HEADS, MEM_HD))

        wr = jnp.concatenate([w_router_g[l], w_router_e[l].reshape(D_MODEL, N_EXPERTS),
                              jnp.zeros((D_MODEL, ROUTER_W - N_GROUPS - N_EXPERTS), F32)],
                             axis=1).astype(BF16)
        br = jnp.concatenate([b_router_g[l], b_router_e[l].reshape(N_EXPERTS),
                              jnp.zeros((ROUTER_W - N_GROUPS - N_EXPERTS,), F32)]).reshape(1, ROUTER_W)
        xp, ids, gates = _norm_router(x, norm_ffn[l], wr, br)
        plan = _dispatch_plan(ids)
        hid = _moe_up(xp, plan, w_gate, w_up, l)
        ys = _moe_down(hid, plan, w_down, l)
        if l < DEPTH - 1:
            x, xn = _combine(x, ys, plan["dest"], gates, norm_mix[l + 1], final=False)
        else:
            y_p, y_s = _combine(x, ys, plan["dest"], gates, norm_final, final=True)

    y_prompt = y_p.reshape(BATCH, SEQ, D_MODEL)
    y_sample = y_s.reshape(DEC_BATCH, 1, D_MODEL)
    return (y_prompt, y_sample, jnp.stack(conv_p), jnp.stack(lruc_p), jnp.stack(h_p),
            jnp.stack(mk_p), jnp.stack(mv_p), jnp.stack(conv_s), jnp.stack(lruc_s), jnp.stack(h_s))
```

```python
import functools
import math

import jax
import jax.numpy as jnp
from jax import lax
from jax.experimental import pallas as pl
from jax.experimental.pallas import tpu as pltpu

F32 = jnp.float32
BF16 = jnp.bfloat16
I32 = jnp.int32

D_MODEL = 4096
BATCH = 4
SEQ = 2048
DEPTH = 2
DEC_BATCH = 128
C_CONV = D_MODEL // 2
C_LRU = D_MODEL // 2
D_IN = 2 * C_CONV + 2 * C_LRU
CONV_GROUPS = 16
GROUP_W = C_CONV // CONV_GROUPS
CONV_K = 31
LRU_HEADS = 16
LRU_HD = C_LRU // LRU_HEADS
LRU_CONV_K = 4
RG_C = 8.0
MEM_LEN = 256
MEM_HEADS = 4
MEM_HD = D_MODEL // 16
MEM_W = MEM_HEADS * MEM_HD
N_GROUPS = 4
EXPERTS_PER_GROUP = 8
N_EXPERTS = N_GROUPS * EXPERTS_PER_GROUP
TOP_K = 2
D_EXPERT = D_MODEL // 4
EPS = 1e-6

N_P = BATCH * SEQ
N_S = DEC_BATCH
N_TOK = N_P + N_S
N_SLOT = N_TOK * TOP_K

LANES = 128
SUBLANES = 8
VMEM_CAP = 56 * 1024 * 1024

TM = 1040
TM_NORM = 416
MOE_BLOCK = 128
N_BLOCKS = -(-(N_SLOT + N_EXPERTS * (MOE_BLOCK - 1)) // MOE_BLOCK)
N_ROWS = N_BLOCKS * MOE_BLOCK
TF = 256
ROUTER_W = LANES
HALF = D_MODEL // 2
CONV_TT = 64
CONV_TC = 256
LRU_TT = 256
COMB_TM = 128
ATT_TQ = 512
S_BT = 16
ATT_S_BT = 4
ISSUE_UNROLL = 8
NSUB = 6
SB_ROWS = NSUB * MOE_BLOCK
N_SB = (N_BLOCKS + (NSUB - 1) * (N_EXPERTS + 1)) // NSUB + 1


def _cparams(sem, vmem_bytes):
    limit = min(VMEM_CAP, max(32 * 1024 * 1024, int(vmem_bytes)))
    return pltpu.CompilerParams(dimension_semantics=sem, vmem_limit_bytes=limit)


def _sigmoid(x):
    return 1.0 / (1.0 + jnp.exp(-x))


def _silu(x):
    return x * _sigmoid(x)


def _gelu_tanh(x):
    c = math.sqrt(2.0 / math.pi)
    return 0.5 * x * (1.0 + jnp.tanh(c * (x + 0.044715 * (x * x * x))))


def _bf16_round(x):
    return x.astype(BF16).astype(F32)


def _rms(x, g):
    ms = jnp.mean(x * x, axis=-1, keepdims=True)
    return x * lax.rsqrt(ms + EPS) * g


def _norm_kernel(x_ref, g_ref, o_ref):
    o_ref[...] = _rms(x_ref[...], g_ref[...]).astype(o_ref.dtype)


def _norm(x, g, *, tm, out_dtype=BF16, row_block0=0, n_rows=None):
    n_rows = x.shape[0] if n_rows is None else n_rows
    d = x.shape[1]
    return pl.pallas_call(
        _norm_kernel,
        grid=(n_rows // tm,),
        in_specs=[pl.BlockSpec((tm, d), lambda i: (i + row_block0, 0)),
                  pl.BlockSpec((1, d), lambda i: (0, 0))],
        out_specs=pl.BlockSpec((tm, d), lambda i: (i, 0)),
        out_shape=jax.ShapeDtypeStruct((n_rows, d), out_dtype),
        compiler_params=_cparams(("arbitrary",), 6 * tm * d * 4),
        name="rmsnorm",
    )(x, g.reshape(1, d))


def _mm_kernel(*refs, n_a, has_res):
    a_refs = refs[:n_a]
    w_ref = refs[n_a]
    res_ref = refs[n_a + 1] if has_res else None
    o_ref = refs[n_a + 1 + int(has_res)]
    acc = None
    off = 0
    for a_ref in a_refs:
        k = a_ref.shape[1]
        part = lax.dot_general(a_ref[...].astype(BF16), w_ref[off:off + k, :], (((1,), (0,)), ((), ())),
                               preferred_element_type=F32)
        acc = part if acc is None else acc + part
        off += k
    if has_res:
        acc = acc + res_ref[...]
    o_ref[...] = acc.astype(o_ref.dtype)


def _mm(a_list, w, layer, *, tn, tm, res=None, out_dtype=F32, name="proj"):
    m = a_list[0].shape[0]
    _, k, n = w.shape
    assert sum(a.shape[1] for a in a_list) == k
    in_specs = [pl.BlockSpec((tm, a.shape[1]), lambda j, i: (i, 0)) for a in a_list]
    in_specs.append(pl.BlockSpec((None, k, tn), lambda j, i: (layer, 0, j)))
    args = list(a_list) + [w]
    if res is not None:
        in_specs.append(pl.BlockSpec((tm, tn), lambda j, i: (i, j)))
        args.append(res)
    a_bytes = sum(a.dtype.itemsize * a.shape[1] for a in a_list) * tm
    vmem = 2 * a_bytes + 2 * k * tn * 4 + k * tn * 2 + 6 * tm * tn * 4 + (4 << 20)
    return pl.pallas_call(
        functools.partial(_mm_kernel, n_a=len(a_list), has_res=res is not None),
        grid=(n // tn, m // tm),
        in_specs=in_specs,
        out_specs=pl.BlockSpec((tm, tn), lambda j, i: (i, j)),
        out_shape=jax.ShapeDtypeStruct((m, n), out_dtype),
        compiler_params=_cparams(("arbitrary", "arbitrary"), vmem),
        name=name,
    )(*args)


def _group_norm_silu(y, gn_g, gn_b, store):
    for g in range(y.shape[1] // GROUP_W):
        sl = slice(g * GROUP_W, (g + 1) * GROUP_W)
        yg = y[:, sl]
        mu = jnp.mean(yg, axis=-1, keepdims=True)
        d = yg - mu
        var = jnp.mean(d * d, axis=-1, keepdims=True)
        yn = d * lax.rsqrt(var + EPS) * gn_g[:, sl] + gn_b[:, sl]
        store(sl, _silu(yn))


CONV_PAD = 32


def _causal_taps(src_ref, w_ref, t0, tt, first_off, n_taps, init):
    acc = init
    for s in range(SUBLANES):
        part = None
        for k in range(n_taps):
            off = first_off + k
            if off % SUBLANES != s:
                continue
            base = pl.multiple_of(t0 + (off - s), SUBLANES)
            term = src_ref[pl.ds(base, tt + SUBLANES), :] * w_ref[k:k + 1, :]
            part = term if part is None else part + term
        if part is not None:
            acc = acc + part[s:s + tt, :]
    return acc


def _last_seq(b):
    return jnp.minimum(b, BATCH - 1)


def _prompt_or_zero(body, y_ref):
    b = pl.program_id(1)

    @pl.when(b < BATCH)
    def _():
        body()

    @pl.when(b == BATCH)
    def _():
        y_ref[...] = jnp.zeros(y_ref.shape, y_ref.dtype)


def _conv_p_kernel(*refs):
    _prompt_or_zero(functools.partial(_conv_p_body, *refs), refs[6])


def _conv_p_body(cv_ref, cg_ref, w_ref, b_ref, gng_ref, gnb_ref, y_ref, st_ref, upad_ref, wr_ref):
    t_len = cv_ref.shape[0]
    tc = cv_ref.shape[1]
    wr_ref[0:CONV_K, :] = _bf16_round(w_ref[...])
    upad_ref[0:CONV_PAD, :] = jnp.zeros((CONV_PAD, tc), F32)
    upad_ref[CONV_PAD + t_len:CONV_PAD + t_len + SUBLANES, :] = jnp.zeros((SUBLANES, tc), F32)

    def glu(c, carry):
        t0 = pl.multiple_of(c * LRU_TT, LRU_TT)
        u = cv_ref[pl.ds(t0, LRU_TT), :] * _sigmoid(cg_ref[pl.ds(t0, LRU_TT), :])
        upad_ref[pl.ds(CONV_PAD + t0, LRU_TT), :] = _bf16_round(u)
        return carry

    lax.fori_loop(0, t_len // LRU_TT, glu, 0)
    tail = slice(t_len - (CONV_K - 1), t_len)
    st_ref[0] = cv_ref[tail, :] * _sigmoid(cg_ref[tail, :])

    bias = b_ref[...]
    gn_g = gng_ref[...]
    gn_b = gnb_ref[...]
    shift = CONV_PAD - (CONV_K - 1)

    def chunk(c, carry):
        t0 = pl.multiple_of(c * CONV_TT, CONV_TT)
        acc = _causal_taps(upad_ref, wr_ref, t0, CONV_TT, shift, CONV_K,
                           jnp.zeros((CONV_TT, tc), F32) + bias)

        def store(sl, v):
            y_ref[pl.ds(t0, CONV_TT), sl] = v.astype(y_ref.dtype)

        _group_norm_silu(acc, gn_g, gn_b, store)
        return carry

    lax.fori_loop(0, t_len // CONV_TT, chunk, 0)


def _conv_prompt(z, conv_w, conv_b, gn_g, gn_b):
    tc = CONV_TC
    nct = C_CONV // tc
    return pl.pallas_call(
        _conv_p_kernel,
        grid=(nct, BATCH + 1),
        in_specs=[pl.BlockSpec((SEQ, tc), lambda c, b: (_last_seq(b), c)),
                  pl.BlockSpec((SEQ, tc), lambda c, b: (_last_seq(b), c + nct)),
                  pl.BlockSpec((CONV_K, tc), lambda c, b: (0, c)),
                  pl.BlockSpec((1, tc), lambda c, b: (0, c)),
                  pl.BlockSpec((1, tc), lambda c, b: (0, c)),
                  pl.BlockSpec((1, tc), lambda c, b: (0, c))],
        out_specs=[pl.BlockSpec((SEQ, tc), lambda c, b: (b, c)),
                   pl.BlockSpec((1, CONV_K - 1, tc), lambda c, b: (_last_seq(b), 0, c))],
        out_shape=[jax.ShapeDtypeStruct((N_TOK, C_CONV), BF16),
                   jax.ShapeDtypeStruct((BATCH, CONV_K - 1, C_CONV), F32)],
        scratch_shapes=[pltpu.VMEM((CONV_PAD + SEQ + SUBLANES, tc), F32),
                        pltpu.VMEM((CONV_PAD, tc), F32)],
        compiler_params=_cparams(("arbitrary", "arbitrary"), 8 * SEQ * tc * 4),
        name="conv_prompt",
    )(z, z, conv_w, conv_b.reshape(1, -1), gn_g.reshape(1, -1), gn_b.reshape(1, -1))


def _log_sigmoid(x):
    return -(jnp.maximum(-x, 0.0) + jnp.log(1.0 + jnp.exp(-jnp.abs(x))))


def _lru_gates(xc, wa, ba, wi, bi, log_sig_lam):
    xb = xc.astype(BF16)
    r = _sigmoid(jnp.dot(xb, wa, preferred_element_type=F32) + ba)
    i = _sigmoid(jnp.dot(xb, wi, preferred_element_type=F32) + bi)
    a = jnp.exp(RG_C * r * log_sig_lam)
    return a, jnp.sqrt(1.0 - a * a) * (i * xc)


LRU_PAD = 8


def _lru_p_kernel(*refs):
    _prompt_or_zero(functools.partial(_lru_p_body, *refs), refs[9])


def _lru_p_body(lx_ref, lg_ref, cw_ref, cb_ref, wa_ref, ba_ref, wi_ref, bi_ref, lam_ref,
                y_ref, buf_ref, h_ref, xpad_ref, a_ref, b_ref, cwr_ref):
    t_len = lx_ref.shape[0]
    tc = lx_ref.shape[1]
    n_heads = tc // LRU_HD
    xpad_ref[0:LRU_PAD, :] = jnp.zeros((LRU_PAD, tc), F32)
    xpad_ref[LRU_PAD + t_len:LRU_PAD + t_len + SUBLANES, :] = jnp.zeros((SUBLANES, tc), F32)

    def copy_in(c, carry):
        t0 = pl.multiple_of(c * LRU_TT, LRU_TT)
        xpad_ref[pl.ds(LRU_PAD + t0, LRU_TT), :] = _bf16_round(lx_ref[pl.ds(t0, LRU_TT), :])
        return carry

    lax.fori_loop(0, t_len // LRU_TT, copy_in, 0)
    buf_ref[0] = lx_ref[t_len - (LRU_CONV_K - 1):t_len, :]

    cb = cb_ref[...]
    cwr_ref[0:LRU_CONV_K, :] = _bf16_round(cw_ref[...])
    log_sig_lam = _log_sigmoid(lam_ref[...])
    shift = LRU_PAD - (LRU_CONV_K - 1)

    def gates(c, carry):
        t0 = pl.multiple_of(c * LRU_TT, LRU_TT)
        xc = _causal_taps(xpad_ref, cwr_ref, t0, LRU_TT, shift, LRU_CONV_K,
                          jnp.zeros((LRU_TT, tc), F32) + cb)
        for hh in range(n_heads):
            sl = slice(hh * LRU_HD, (hh + 1) * LRU_HD)
            a, b = _lru_gates(xc[:, sl], wa_ref[hh].astype(BF16), ba_ref[:, sl],
                              wi_ref[hh].astype(BF16), bi_ref[:, sl], log_sig_lam[:, sl])
            a_ref[pl.ds(t0, LRU_TT), sl] = a
            b_ref[pl.ds(t0, LRU_TT), sl] = b
        return carry

    lax.fori_loop(0, t_len // LRU_TT, gates, 0)

    row = lax.broadcasted_iota(I32, (SUBLANES, tc), 0)

    def scan(i, h):
        t0 = pl.multiple_of(i * SUBLANES, SUBLANES)
        a = a_ref[pl.ds(t0, SUBLANES), :]
        b = b_ref[pl.ds(t0, SUBLANES), :]
        for d in (1, 2, 4):
            a_sh = pltpu.roll(a, d, axis=0)
            b_sh = pltpu.roll(b, d, axis=0)
            m = row >= d
            b = jnp.where(m, a * b_sh + b, b)
            a = jnp.where(m, a * a_sh, a)
        hb = a * h + b
        b_ref[pl.ds(t0, SUBLANES), :] = hb
        return hb[SUBLANES - 1:SUBLANES, :]

    h_last = lax.fori_loop(0, t_len // SUBLANES, scan, jnp.zeros((1, tc), F32), unroll=4)
    h_ref[0] = h_last

    def gate_out(c, carry):
        t0 = pl.multiple_of(c * LRU_TT, LRU_TT)
        y = b_ref[pl.ds(t0, LRU_TT), :] * _gelu_tanh(lg_ref[pl.ds(t0, LRU_TT), :])
        y_ref[pl.ds(t0, LRU_TT), :] = y.astype(y_ref.dtype)
        return carry

    lax.fori_loop(0, t_len // LRU_TT, gate_out, 0)


def _lru_prompt(z, cw, cb, wa, ba, wi, bi, lam):
    tc = CONV_TC
    nct = C_LRU // tc
    hpb = tc // LRU_HD
    col0 = 2 * C_CONV // tc
    vec = lambda v: v.reshape(1, -1)
    return pl.pallas_call(
        _lru_p_kernel,
        grid=(nct, BATCH + 1),
        in_specs=[pl.BlockSpec((SEQ, tc), lambda c, b: (_last_seq(b), c + col0)),
                  pl.BlockSpec((SEQ, tc), lambda c, b: (_last_seq(b), c + col0 + nct)),
                  pl.BlockSpec((LRU_CONV_K, tc), lambda c, b: (0, c)),
                  pl.BlockSpec((1, tc), lambda c, b: (0, c)),
                  pl.BlockSpec((hpb, LRU_HD, LRU_HD), lambda c, b: (c, 0, 0)),
                  pl.BlockSpec((1, tc), lambda c, b: (0, c)),
                  pl.BlockSpec((hpb, LRU_HD, LRU_HD), lambda c, b: (c, 0, 0)),
                  pl.BlockSpec((1, tc), lambda c, b: (0, c)),
                  pl.BlockSpec((1, tc), lambda c, b: (0, c))],
        out_specs=[pl.BlockSpec((SEQ, tc), lambda c, b: (b, c)),
                   pl.BlockSpec((1, LRU_CONV_K - 1, tc), lambda c, b: (_last_seq(b), 0, c)),
                   pl.BlockSpec((1, 1, tc), lambda c, b: (_last_seq(b), 0, c))],
        out_shape=[jax.ShapeDtypeStruct((N_TOK, C_LRU), BF16),
                   jax.ShapeDtypeStruct((BATCH, LRU_CONV_K - 1, C_LRU), F32),
                   jax.ShapeDtypeStruct((BATCH, 1, C_LRU), F32)],
        scratch_shapes=[pltpu.VMEM((LRU_PAD + SEQ + SUBLANES, tc), F32),
                        pltpu.VMEM((SEQ, tc), F32),
                        pltpu.VMEM((SEQ, tc), F32),
                        pltpu.VMEM((LRU_PAD, tc), F32)],
        compiler_params=_cparams(("arbitrary", "arbitrary"), 12 * SEQ * tc * 4),
        name="lru_prompt",
    )(z, z, cw, vec(cb), wa, vec(ba), wi, vec(bi), vec(lam))


def _mixer_s_kernel(cv_ref, cg_ref, lx_ref, lg_ref, st_ref, lst_ref, h0_ref,
                    cw_ref, cb_ref, gng_ref, gnb_ref, lcw_ref, lcb_ref,
                    wa_ref, ba_ref, wi_ref, bi_ref, lam_ref, yc_in, yl_in,
                    yc_ref, yl_ref, nst_ref, nlst_ref, nh_ref):
    del yc_in, yl_in
    u = cv_ref[...] * _sigmoid(cg_ref[...])
    cw = _bf16_round(cw_ref[...])
    acc = _bf16_round(u) * cw[CONV_K - 1:CONV_K, :] + cb_ref[...]
    for k in range(CONV_K - 1):
        row = st_ref[:, k, :]
        acc = acc + _bf16_round(row) * cw[k:k + 1, :]
        if k > 0:
            nst_ref[:, k - 1, :] = row
    nst_ref[:, CONV_K - 2, :] = u

    def store_c(sl, v):
        yc_ref[:, sl] = v.astype(yc_ref.dtype)

    _group_norm_silu(acc, gng_ref[...], gnb_ref[...], store_c)

    lx = lx_ref[...]
    lcw = _bf16_round(lcw_ref[...])
    xc = _bf16_round(lx) * lcw[LRU_CONV_K - 1:LRU_CONV_K, :] + lcb_ref[...]
    for k in range(LRU_CONV_K - 1):
        row = lst_ref[:, k, :]
        xc = xc + _bf16_round(row) * lcw[k:k + 1, :]
        if k > 0:
            nlst_ref[:, k - 1, :] = row
    nlst_ref[:, LRU_CONV_K - 2, :] = lx

    log_sig_lam = _log_sigmoid(lam_ref[...])
    for hh in range(LRU_HEADS):
        sl = slice(hh * LRU_HD, (hh + 1) * LRU_HD)
        a, b = _lru_gates(xc[:, sl], wa_ref[hh].astype(BF16), ba_ref[:, sl],
                          wi_ref[hh].astype(BF16), bi_ref[:, sl], log_sig_lam[:, sl])
        h = a * h0_ref[:, sl] + b
        nh_ref[:, sl] = h
        yl_ref[:, sl] = (h * _gelu_tanh(lg_ref[:, sl])).astype(yl_ref.dtype)


def _mixer_sample(z, st, lst, h0, layer, yc, yl, cw, cb, gn_g, gn_b, lcw, lcb, wa, ba, wi, bi, lam):
    bt = S_BT
    rb0 = N_P // bt
    vec = lambda v: v.reshape(1, -1)
    zspec = lambda col: pl.BlockSpec((bt, C_CONV), lambda i: (i + rb0, col))
    full = lambda shape: pl.BlockSpec(shape, lambda i: (0,) * len(shape))
    any_spec = pl.BlockSpec(memory_space=pl.ANY)
    return pl.pallas_call(
        _mixer_s_kernel,
        grid=(N_S // bt,),
        in_specs=[zspec(0), zspec(1), zspec(2), zspec(3),
                  pl.BlockSpec((None, bt, CONV_K - 1, C_CONV), lambda i: (layer, i, 0, 0)),
                  pl.BlockSpec((None, bt, LRU_CONV_K - 1, C_LRU), lambda i: (layer, i, 0, 0)),
                  pl.BlockSpec((None, bt, C_LRU), lambda i: (layer, i, 0)),
                  full((CONV_K, C_CONV)), full((1, C_CONV)), full((1, C_CONV)), full((1, C_CONV)),
                  full((LRU_CONV_K, C_LRU)), full((1, C_LRU)),
                  full((LRU_HEADS, LRU_HD, LRU_HD)), full((1, C_LRU)),
                  full((LRU_HEADS, LRU_HD, LRU_HD)), full((1, C_LRU)), full((1, C_LRU)),
                  any_spec, any_spec],
        out_specs=[pl.BlockSpec((bt, C_CONV), lambda i: (i + rb0, 0)),
                   pl.BlockSpec((bt, C_LRU), lambda i: (i + rb0, 0)),
                   pl.BlockSpec((bt, CONV_K - 1, C_CONV), lambda i: (i, 0, 0)),
                   pl.BlockSpec((bt, LRU_CONV_K - 1, C_LRU), lambda i: (i, 0, 0)),
                   pl.BlockSpec((bt, C_LRU), lambda i: (i, 0))],
        out_shape=[jax.ShapeDtypeStruct(yc.shape, yc.dtype),
                   jax.ShapeDtypeStruct(yl.shape, yl.dtype),
                   jax.ShapeDtypeStruct(st.shape[1:], F32),
                   jax.ShapeDtypeStruct(lst.shape[1:], F32),
                   jax.ShapeDtypeStruct(h0.shape[1:], F32)],
        input_output_aliases={18: 0, 19: 1},
        compiler_params=_cparams(("arbitrary",), 48 << 20),
        name="mixer_sample",
    )(z, z, z, z, st, lst, h0, cw, vec(cb), vec(gn_g), vec(gn_b), lcw, vec(lcb),
      wa, vec(ba), wi, vec(bi), vec(lam), yc, yl)


def _attn_p_kernel(q_ref, k_ref, v_ref, o_ref):
    i = pl.program_id(0)

    @pl.when(i < N_P // ATT_TQ)
    def _():
        _attn_p_body(q_ref, k_ref, v_ref, o_ref)

    @pl.when(i == N_P // ATT_TQ)
    def _():
        o_ref[...] = jnp.zeros(o_ref.shape, o_ref.dtype)


def _attn_p_body(q_ref, k_ref, v_ref, o_ref):
    scale = MEM_HD ** -0.5
    for h in range(MEM_HEADS):
        sl = slice(h * MEM_HD, (h + 1) * MEM_HD)
        q = q_ref[:, sl].astype(BF16)
        k = k_ref[:, sl].astype(BF16)
        v = v_ref[:, sl].astype(BF16)
        s = lax.dot_general(q, k, (((1,), (1,)), ((), ())), preferred_element_type=F32) * scale
        p = jnp.exp(s - jnp.max(s, axis=-1, keepdims=True))
        pr = p / jnp.sum(p, axis=-1, keepdims=True)
        o = jnp.dot(pr.astype(BF16), v, preferred_element_type=F32)
        o_ref[:, sl] = o.astype(o_ref.dtype)


def _attn_prompt(q, k, v):
    nq = SEQ // ATT_TQ
    return pl.pallas_call(
        _attn_p_kernel,
        grid=(BATCH * nq + 1,),
        in_specs=[pl.BlockSpec((ATT_TQ, MEM_W), lambda i: (i, 0)),
                  pl.BlockSpec((MEM_LEN, MEM_W), lambda i: (_last_seq(i // nq), 0)),
                  pl.BlockSpec((MEM_LEN, MEM_W), lambda i: (_last_seq(i // nq), 0))],
        out_specs=pl.BlockSpec((ATT_TQ, MEM_W), lambda i: (i, 0)),
        out_shape=jax.ShapeDtypeStruct((N_TOK, MEM_W), BF16),
        compiler_params=_cparams(("arbitrary",), 32 << 20),
        name="attn_prompt",
    )(q, k, v)


def _attn_s_kernel(q_ref, k_ref, v_ref, o_in, o_ref, stage_ref):
    del o_in
    jj = pl.program_id(1)
    scale = MEM_HD ** -0.5
    for bb in range(ATT_S_BT):
        r = jj * ATT_S_BT + bb
        q = _bf16_round(q_ref[pl.ds(r, 1)])
        s = jnp.sum(_bf16_round(k_ref[bb]) * q, axis=-1, keepdims=True) * scale
        p = jnp.exp(s - jnp.max(s, axis=0, keepdims=True))
        pr = _bf16_round(p / jnp.sum(p, axis=0, keepdims=True))
        o = jnp.sum(pr * _bf16_round(v_ref[bb]), axis=0)
        for h in range(MEM_HEADS):
            stage_ref[pl.ds(r, 1), h * MEM_HD:(h + 1) * MEM_HD] = o[h:h + 1, :]

    @pl.when(jj == pl.num_programs(1) - 1)
    def _():
        o_ref[...] = stage_ref[...].astype(o_ref.dtype)


def _attn_sample(q4, kc, vc, layer, o):
    bt = S_BT
    inner = bt // ATT_S_BT
    rb0 = N_P // bt
    cache_spec = pl.BlockSpec((None, ATT_S_BT, MEM_LEN, MEM_HEADS, MEM_HD),
                              lambda i, j: (layer, i * inner + j, 0, 0, 0))
    return pl.pallas_call(
        _attn_s_kernel,
        grid=(N_S // bt, inner),
        in_specs=[pl.BlockSpec((bt, MEM_HEADS, MEM_HD), lambda i, j: (i, 0, 0)),
                  cache_spec, cache_spec,
                  pl.BlockSpec(memory_space=pl.ANY)],
        out_specs=pl.BlockSpec((bt, MEM_W), lambda i, j: (i + rb0, 0)),
        out_shape=jax.ShapeDtypeStruct(o.shape, o.dtype),
        scratch_shapes=[pltpu.VMEM((bt, MEM_W), F32)],
        input_output_aliases={3: 0},
        compiler_params=_cparams(("arbitrary", "arbitrary"), 48 << 20),
        name="attn_sample",
    )(q4, kc, vc, o)


def _norm_router_kernel(x_ref, g_ref, wr_ref, br_ref, xp_ref, ids_ref, gates_ref):
    xn = _rms(x_ref[...], g_ref[...])
    xp_ref[...] = pltpu.pack_elementwise([xn[:, :HALF], xn[:, HALF:]], packed_dtype=BF16)
    logits = jnp.dot(xn.astype(BF16), wr_ref[...], preferred_element_type=F32) + br_ref[...]
    lane = lax.broadcasted_iota(I32, logits.shape, 1)
    neg = jnp.float32(-jnp.inf)

    def first_max(vals):
        m = jnp.max(vals, axis=-1, keepdims=True)
        idx = jnp.min(jnp.where(vals == m, lane, ROUTER_W), axis=-1, keepdims=True)
        return m, idx

    is_group = lane < N_GROUPS
    g_max, g_sel = first_max(jnp.where(is_group, logits, neg))
    p_g = 1.0 / jnp.sum(jnp.where(is_group, jnp.exp(logits - g_max), 0.0), axis=-1, keepdims=True)
    lo = N_GROUPS + EXPERTS_PER_GROUP * g_sel
    in_group = (lane >= lo) & (lane < lo + EXPERTS_PER_GROUP)
    e_logits = jnp.where(in_group, logits, neg)
    t1, i1 = first_max(e_logits)
    t2, i2 = first_max(jnp.where(lane == i1, neg, e_logits))
    e = jnp.exp(t2 - t1)
    w1 = p_g / (1.0 + e)
    w2 = p_g * e / (1.0 + e)
    ids_ref[...] = jnp.where(lane == 0, i1 - N_GROUPS, jnp.where(lane == 1, i2 - N_GROUPS, 0))
    gates_ref[...] = jnp.where(lane == 0, w1, jnp.where(lane == 1, w2, 0.0))


def _norm_router(x, g, wr, br):
    tm = TM_NORM
    return pl.pallas_call(
        _norm_router_kernel,
        grid=(N_TOK // tm,),
        in_specs=[pl.BlockSpec((tm, D_MODEL), lambda i: (i, 0)),
                  pl.BlockSpec((1, D_MODEL), lambda i: (0, 0)),
                  pl.BlockSpec((D_MODEL, ROUTER_W), lambda i: (0, 0)),
                  pl.BlockSpec((1, ROUTER_W), lambda i: (0, 0))],
        out_specs=[pl.BlockSpec((tm, HALF), lambda i: (i, 0)),
                   pl.BlockSpec((tm, ROUTER_W), lambda i: (i, 0)),
                   pl.BlockSpec((tm, ROUTER_W), lambda i: (i, 0))],
        out_shape=[jax.ShapeDtypeStruct((N_TOK, HALF), jnp.uint32),
                   jax.ShapeDtypeStruct((N_TOK, ROUTER_W), I32),
                   jax.ShapeDtypeStruct((N_TOK, ROUTER_W), F32)],
        compiler_params=_cparams(("arbitrary",), 8 * tm * D_MODEL * 4),
        name="norm_router",
    )(x, g.reshape(1, -1), wr, br)


def _unpack_x(xp):
    lo = pltpu.unpack_elementwise(xp, index=0, packed_dtype=BF16, unpacked_dtype=F32)
    hi = pltpu.unpack_elementwise(xp, index=1, packed_dtype=BF16, unpacked_dtype=F32)
    return lo.astype(BF16), hi.astype(BF16)


def _dot(a, w):
    return lax.dot_general(a, w, (((1,), (0,)), ((), ())), preferred_element_type=F32)


def _ffn_kernel(sbe_ref, sbb_ref, sbn_ref, sbz_ref, tok_ref, xp_hbm, wg_ref, wu_ref, wd_ref, ys_hbm,
                xbuf_ref, acc_ref, xsem, ysem, pend_ref):
    del sbe_ref
    sb = pl.program_id(0)
    j = pl.program_id(1)
    nj = pl.num_programs(1)
    slot = sb % 2

    @pl.when((sb == 0) & (j == 0))
    def _():
        pend_ref[0] = 0

    def row_copy(s, slot_, r):
        tok = tok_ref[sbb_ref[s] * MOE_BLOCK + r]
        return pltpu.make_async_copy(xp_hbm.at[pl.ds(tok, 1), :],
                                     xbuf_ref.at[slot_, pl.ds(r, 1), :], xsem.at[slot_])

    def start_blocks(s, slot_, lo, hi):
        def block(i, carry):
            def body(r, c):
                row_copy(s, slot_, i * MOE_BLOCK + r).start()
                return c

            return lax.fori_loop(0, MOE_BLOCK, body, carry, unroll=ISSUE_UNROLL)

        lax.fori_loop(lo, jnp.minimum(hi, sbn_ref[s]), block, 0)

    def wait_rows(s, slot_):
        def block(i, carry):
            r0 = pl.multiple_of(i * MOE_BLOCK, MOE_BLOCK)
            pltpu.make_async_copy(xp_hbm.at[pl.ds(0, MOE_BLOCK), :],
                                  xbuf_ref.at[slot_, pl.ds(r0, MOE_BLOCK), :], xsem.at[slot_]).wait()
            return carry

        lax.fori_loop(0, sbn_ref[s], block, 0)

    def out_copy(i, blk):
        r0 = pl.multiple_of(i * MOE_BLOCK, MOE_BLOCK)
        r1 = pl.multiple_of(blk * MOE_BLOCK, MOE_BLOCK)
        return pltpu.make_async_copy(acc_ref.at[pl.ds(r0, MOE_BLOCK), :],
                                     ys_hbm.at[pl.ds(r1, MOE_BLOCK), :], ysem)

    def wait_out():
        def body(i, carry):
            out_copy(0, 0).wait()
            return carry

        lax.fori_loop(0, pend_ref[0], body, 0)
        pend_ref[0] = 0

    per_j = -(-NSUB // (D_EXPERT // TF))

    @pl.when((sb == 0) & (j == 0))
    def _():
        start_blocks(sb, slot, 0, NSUB)

    @pl.when(sb + 1 < pl.num_programs(0))
    def _():
        start_blocks(sb + 1, 1 - slot, j * per_j, (j + 1) * per_j)

    nsub = sbn_ref[sb]
    nzero = sbz_ref[sb]

    def zero_block(i, carry):
        r0 = pl.multiple_of(i * MOE_BLOCK, MOE_BLOCK)
        acc_ref[pl.ds(r0, MOE_BLOCK), :] = jnp.zeros((MOE_BLOCK, D_MODEL), F32)
        return carry

    def start_out(i, carry):
        out_copy(i, sbb_ref[sb] + i).start()
        return carry

    @pl.when(j == 0)
    def _():
        wait_rows(sb, slot)
        wait_out()
        lax.fori_loop(0, jnp.maximum(nsub, nzero), zero_block, 0)
        lax.fori_loop(0, nzero, start_out, 0)
        pend_ref[0] = nzero

    def sub(i, carry):
        r0 = pl.multiple_of(i * MOE_BLOCK, MOE_BLOCK)
        lo, hi = _unpack_x(xbuf_ref[slot, pl.ds(r0, MOE_BLOCK), :])
        g = _dot(lo, wg_ref[:HALF, :]) + _dot(hi, wg_ref[HALF:, :])
        u = _dot(lo, wu_ref[:HALF, :]) + _dot(hi, wu_ref[HALF:, :])
        h = (_silu(g) * u).astype(BF16)
        acc_ref[pl.ds(r0, MOE_BLOCK), :] += _dot(h, wd_ref[...])
        return carry

    lax.fori_loop(0, nsub, sub, 0)

    @pl.when((j == nj - 1) & (nsub > 0))
    def _():
        lax.fori_loop(0, nsub, start_out, 0)
        pend_ref[0] = nsub

    @pl.when((sb == pl.num_programs(0) - 1) & (j == nj - 1))
    def _():
        wait_out()


def _moe_ffn(xp, plan, w_gate, w_up, w_down, layer):
    nj = D_EXPERT // TF

    def hidden_slice(sb, j, sbn):
        return jnp.where(sbn[sb] > 0, j, nj - 1)

    up_spec = pl.BlockSpec((None, None, D_MODEL, TF),
                           lambda sb, j, sbe, sbb, sbn, sbz, tok: (layer, sbe[sb], 0, hidden_slice(sb, j, sbn)))
    down_spec = pl.BlockSpec((None, None, TF, D_MODEL),
                             lambda sb, j, sbe, sbb, sbn, sbz, tok: (layer, sbe[sb], hidden_slice(sb, j, sbn), 0))
    return pl.pallas_call(
        _ffn_kernel,
        grid_spec=pltpu.PrefetchScalarGridSpec(
            num_scalar_prefetch=5,
            grid=(N_SB, nj),
            in_specs=[pl.BlockSpec(memory_space=pl.ANY), up_spec, up_spec, down_spec],
            out_specs=pl.BlockSpec(memory_space=pl.ANY),
            scratch_shapes=[pltpu.VMEM((2, SB_ROWS, HALF), jnp.uint32),
                            pltpu.VMEM((SB_ROWS, D_MODEL), F32),
                            pltpu.SemaphoreType.DMA((2,)),
                            pltpu.SemaphoreType.DMA(()),
                            pltpu.SMEM((1,), I32)]),
        out_shape=jax.ShapeDtypeStruct((N_ROWS, D_MODEL), F32),
        compiler_params=_cparams(("arbitrary", "arbitrary"), VMEM_CAP),
        name="moe_ffn",
    )(plan["sb_e"], plan["sb_blk"], plan["sb_n"], plan["sb_z"], plan["row_tok"], xp, w_gate, w_up, w_down)


def _combine_kernel(dest_ref, x_ref, gates_ref, g_ref, ys_hbm, out_a, out_b, buf_ref, sem, *, final):
    i = pl.program_id(0)
    n = pl.num_programs(0)

    def row_copy(step, slot, t, k):
        s = (step * COMB_TM + t) * TOP_K + k
        return pltpu.make_async_copy(ys_hbm.at[pl.ds(dest_ref[s], 1), :],
                                     buf_ref.at[slot, k, pl.ds(t, 1), :], sem.at[slot])

    def start_all(step, slot):
        def body(t, carry):
            for k in range(TOP_K):
                row_copy(step, slot, t, k).start()
            return carry

        lax.fori_loop(0, COMB_TM, body, 0, unroll=ISSUE_UNROLL)

    def wait_all(step, slot):
        for k in range(TOP_K):
            pltpu.make_async_copy(ys_hbm.at[pl.ds(0, COMB_TM), :], buf_ref.at[slot, k], sem.at[slot]).wait()

    slot = i % 2

    @pl.when(i == 0)
    def _():
        start_all(i, slot)

    @pl.when(i + 1 < n)
    def _():
        start_all(i + 1, 1 - slot)

    wait_all(i, slot)
    gates = gates_ref[...]
    y = x_ref[...] + gates[:, 0:1] * buf_ref[slot, 0] + gates[:, 1:2] * buf_ref[slot, 1]
    yn = _rms(y, g_ref[...])
    if final:
        @pl.when(i < n - 1)
        def _():
            out_a[...] = yn

        @pl.when(i == n - 1)
        def _():
            out_b[...] = yn
    else:
        out_a[...] = y
        out_b[...] = yn.astype(out_b.dtype)


def _combine(x, ys, dest, gates, g_next, *, final):
    tm = COMB_TM
    assert N_S == tm
    n_p_blocks = N_P // tm
    if final:
        out_specs = [pl.BlockSpec((tm, D_MODEL), lambda i, d: (jnp.minimum(i, n_p_blocks - 1), 0)),
                     pl.BlockSpec((tm, D_MODEL), lambda i, d: (0, 0))]
        out_shape = [jax.ShapeDtypeStruct((N_P, D_MODEL), F32),
                     jax.ShapeDtypeStruct((N_S, D_MODEL), F32)]
    else:
        out_specs = [pl.BlockSpec((tm, D_MODEL), lambda i, d: (i, 0)),
                     pl.BlockSpec((tm, D_MODEL), lambda i, d: (i, 0))]
        out_shape = [jax.ShapeDtypeStruct((N_TOK, D_MODEL), F32),
                     jax.ShapeDtypeStruct((N_TOK, D_MODEL), BF16)]
    return pl.pallas_call(
        functools.partial(_combine_kernel, final=final),
        grid_spec=pltpu.PrefetchScalarGridSpec(
            num_scalar_prefetch=1,
            grid=(N_TOK // tm,),
            in_specs=[pl.BlockSpec((tm, D_MODEL), lambda i, d: (i, 0)),
                      pl.BlockSpec((tm, ROUTER_W), lambda i, d: (i, 0)),
                      pl.BlockSpec((1, D_MODEL), lambda i, d: (0, 0)),
                      pl.BlockSpec(memory_space=pl.ANY)],
            out_specs=out_specs,
            scratch_shapes=[pltpu.VMEM((2, TOP_K, tm, D_MODEL), F32),
                            pltpu.SemaphoreType.DMA((2,))]),
        out_shape=out_shape,
        compiler_params=_cparams(("arbitrary",), 40 << 20),
        name="moe_combine",
    )(dest, x, gates, g_next.reshape(1, -1), ys)


def _dispatch_plan(ids):
    flat_e = ids[:, :TOP_K].reshape(-1)
    onehot = (flat_e[:, None] == jnp.arange(N_EXPERTS, dtype=I32)[None, :]).astype(I32)
    csum = jnp.cumsum(onehot, axis=0)
    rank = jnp.sum(onehot * csum, axis=1) - 1
    counts = csum[-1]
    nblk = (counts + MOE_BLOCK - 1) // MOE_BLOCK
    bend = jnp.cumsum(nblk)
    bstart = bend - nblk
    dest = (jnp.sum(onehot * bstart[None, :], axis=1) * MOE_BLOCK + rank).astype(I32)
    row_tok = jnp.zeros((N_ROWS,), I32).at[dest].set(jnp.arange(N_SLOT, dtype=I32) // TOP_K)

    nsup = (nblk + NSUB - 1) // NSUB
    sup_end = jnp.cumsum(nsup)
    sup_start = sup_end - nsup
    total_sup = sup_end[-1]
    sb = jnp.arange(N_SB, dtype=I32)
    e_of = jnp.minimum(jnp.sum((sup_end[None, :] <= sb[:, None]).astype(I32), axis=1), N_EXPERTS - 1)
    q = sb - sup_start[e_of]
    real = sb < total_sup
    n_real = jnp.clip(nblk[e_of] - NSUB * q, 0, NSUB)
    blk_real = bstart[e_of] + NSUB * q
    blk_tail = bend[-1] + NSUB * (sb - total_sup)
    n_tail = jnp.clip(N_BLOCKS - blk_tail, 0, NSUB)
    e_last = e_of[total_sup - 1]
    return dict(
        dest=dest, row_tok=row_tok,
        sb_e=jnp.where(real, e_of, e_last).astype(I32),
        sb_blk=jnp.where(real, blk_real, jnp.minimum(blk_tail, N_BLOCKS - 1)).astype(I32),
        sb_n=jnp.where(real, n_real, 0).astype(I32),
        sb_z=jnp.where(real, 0, n_tail).astype(I32))


def kernel(x_prompt, x_sample, mem_prompt, state_conv, state_lru_conv, state_lru_h, cache_mem_k, cache_mem_v, norm_mix, w_in, conv_w, conv_b, conv_gn_g, conv_gn_b, lru_conv_w, lru_conv_b, lru_wa, lru_ba, lru_wi, lru_bi, lru_lambda, w_out, norm_attn, norm_mem_kv, w_q, w_k, w_v, w_o, norm_ffn, w_router_g, b_router_g, w_router_e, b_router_e, w_gate, w_up, w_down, norm_final):
    x = jnp.concatenate([x_prompt.reshape(N_P, D_MODEL), x_sample.reshape(N_S, D_MODEL)], axis=0)
    mem = mem_prompt.reshape(BATCH * MEM_LEN, D_MODEL)
    xn = _norm(x, norm_mix[0], tm=TM_NORM)

    conv_p, lruc_p, h_p, mk_p, mv_p, conv_s, lruc_s, h_s = ([] for _ in range(8))
    for l in range(DEPTH):
        z = _mm([xn], w_in, l, tn=512, tm=TM, name="w_in")
        yc, cst = _conv_prompt(z, conv_w[l], conv_b[l], conv_gn_g[l], conv_gn_b[l])
        yl, lst, hl = _lru_prompt(z, lru_conv_w[l], lru_conv_b[l], lru_wa[l], lru_ba[l],
                                  lru_wi[l], lru_bi[l], lru_lambda[l])
        yc, yl, cst_s, lst_s, hl_s = _mixer_sample(
            z, state_conv, state_lru_conv, state_lru_h, l, yc, yl,
            conv_w[l], conv_b[l], conv_gn_g[l], conv_gn_b[l], lru_conv_w[l], lru_conv_b[l],
            lru_wa[l], lru_ba[l], lru_wi[l], lru_bi[l], lru_lambda[l])
        x = _mm([yc, yl], w_out, l, tn=512, tm=TM, res=x, name="w_out")
        conv_p.append(cst); lruc_p.append(lst); h_p.append(hl.reshape(BATCH, C_LRU))
        conv_s.append(cst_s); lruc_s.append(lst_s); h_s.append(hl_s)

        xn = _norm(x, norm_attn[l], tm=TM_NORM)
        q = _mm([xn], w_q, l, tn=512, tm=TM, name="w_q")
        mn = _norm(mem, norm_mem_kv[l], tm=256)
        k_p = _mm([mn], w_k, l, tn=512, tm=BATCH * MEM_LEN, name="w_k")
        v_p = _mm([mn], w_v, l, tn=512, tm=BATCH * MEM_LEN, name="w_v")
        o = _attn_prompt(q, k_p, v_p)
        o = _attn_sample(q[N_P:].reshape(N_S, MEM_HEADS, MEM_HD), cache_mem_k, cache_mem_v, l, o)
        x = _mm([o], w_o, l, tn=1024, tm=TM, res=x, name="w_o")
        mk_p.append(k_p.reshape(BATCH, MEM_LEN, MEM_HEADS, MEM_HD))
        mv_p.append(v_p.reshape(BATCH, MEM_LEN, MEM_HEADS, MEM_HD))

        wr = jnp.concatenate([w_router_g[l], w_router_e[l].reshape(D_MODEL, N_EXPERTS),
                              jnp.zeros((D_MODEL, ROUTER_W - N_GROUPS - N_EXPERTS), F32)],
                             axis=1).astype(BF16)
        br = jnp.concatenate([b_router_g[l], b_router_e[l].reshape(N_EXPERTS),
                              jnp.zeros((ROUTER_W - N_GROUPS - N_EXPERTS,), F32)]).reshape(1, ROUTER_W)
        xp, ids, gates = _norm_router(x, norm_ffn[l], wr, br)
        plan = _dispatch_plan(ids)
        ys = _moe_ffn(xp, plan, w_gate, w_up, w_down, l)
        if l < DEPTH - 1:
            x, xn = _combine(x, ys, plan["dest"], gates, norm_mix[l + 1], final=False)
        else:
            y_p, y_s = _combine(x, ys, plan["dest"], gates, norm_final, final=True)

    y_prompt = y_p.reshape(BATCH, SEQ, D_MODEL)
    y_sample = y_s.reshape(DEC_BATCH, 1, D_MODEL)
    return (y_prompt, y_sample, jnp.stack(conv_p), jnp.stack(lruc_p), jnp.stack(h_p),
            jnp.stack(mk_p), jnp.stack(mv_p), jnp.stack(conv_s), jnp.stack(lruc_s), jnp.stack(h_s))
```

```python
import functools
import math

import jax
import jax.numpy as jnp
from jax import lax
from jax.experimental import pallas as pl
from jax.experimental.pallas import tpu as pltpu

F32 = jnp.float32
BF16 = jnp.bfloat16
I32 = jnp.int32

D_MODEL = 4096
BATCH = 4
SEQ = 2048
DEPTH = 2
DEC_BATCH = 128
C_CONV = D_MODEL // 2
C_LRU = D_MODEL // 2
D_IN = 2 * C_CONV + 2 * C_LRU
CONV_GROUPS = 16
GROUP_W = C_CONV // CONV_GROUPS
CONV_K = 31
LRU_HEADS = 16
LRU_HD = C_LRU // LRU_HEADS
LRU_CONV_K = 4
RG_C = 8.0
MEM_LEN = 256
MEM_HEADS = 4
MEM_HD = D_MODEL // 16
MEM_W = MEM_HEADS * MEM_HD
N_GROUPS = 4
EXPERTS_PER_GROUP = 8
N_EXPERTS = N_GROUPS * EXPERTS_PER_GROUP
TOP_K = 2
D_EXPERT = D_MODEL // 4
EPS = 1e-6

N_P = BATCH * SEQ
N_S = DEC_BATCH
N_TOK = N_P + N_S
N_SLOT = N_TOK * TOP_K

LANES = 128
SUBLANES = 8
VMEM_CAP = 56 * 1024 * 1024

TM = 1040
TM_NORM = 416
MOE_BLOCK = 128
N_BLOCKS = -(-(N_SLOT + N_EXPERTS * (MOE_BLOCK - 1)) // MOE_BLOCK)
N_ROWS = N_BLOCKS * MOE_BLOCK
TF = 256
ROUTER_W = LANES
HALF = D_MODEL // 2
CONV_TT = 64
CONV_TC = 256
LRU_TT = 256
COMB_TM = 128
ATT_TQ = 512
S_BT = 16
ATT_S_BT = 4
ISSUE_UNROLL = 8
GATHER_PRIORITY = 1
NSUB = 6
SB_ROWS = NSUB * MOE_BLOCK
N_SB = (N_BLOCKS + (NSUB - 1) * (N_EXPERTS + 1)) // NSUB + 1


def _cparams(sem, vmem_bytes):
    limit = min(VMEM_CAP, max(32 * 1024 * 1024, int(vmem_bytes)))
    return pltpu.CompilerParams(dimension_semantics=sem, vmem_limit_bytes=limit)


def _sigmoid(x):
    return 1.0 / (1.0 + jnp.exp(-x))


def _silu(x):
    return x * _sigmoid(x)


def _gelu_tanh(x):
    c = math.sqrt(2.0 / math.pi)
    return 0.5 * x * (1.0 + jnp.tanh(c * (x + 0.044715 * (x * x * x))))


def _rms(x, g):
    ms = jnp.mean(x * x, axis=-1, keepdims=True)
    return x * lax.rsqrt(ms + EPS) * g


def _norm_kernel(x_ref, g_ref, o_ref):
    o_ref[...] = _rms(x_ref[...], g_ref[...]).astype(o_ref.dtype)


def _norm(x, g, *, tm, out_dtype=BF16, row_block0=0, n_rows=None):
    n_rows = x.shape[0] if n_rows is None else n_rows
    d = x.shape[1]
    return pl.pallas_call(
        _norm_kernel,
        grid=(n_rows // tm,),
        in_specs=[pl.BlockSpec((tm, d), lambda i: (i + row_block0, 0)),
                  pl.BlockSpec((1, d), lambda i: (0, 0))],
        out_specs=pl.BlockSpec((tm, d), lambda i: (i, 0)),
        out_shape=jax.ShapeDtypeStruct((n_rows, d), out_dtype),
        compiler_params=_cparams(("arbitrary",), 6 * tm * d * 4),
        name="rmsnorm",
    )(x, g.reshape(1, d))


def _mm_kernel(*refs, n_a, has_res):
    a_refs = refs[:n_a]
    w_ref = refs[n_a]
    res_ref = refs[n_a + 1] if has_res else None
    o_ref = refs[n_a + 1 + int(has_res)]
    acc = None
    off = 0
    for a_ref in a_refs:
        k = a_ref.shape[1]
        part = lax.dot_general(a_ref[...].astype(BF16), w_ref[off:off + k, :], (((1,), (0,)), ((), ())),
                               preferred_element_type=F32)
        acc = part if acc is None else acc + part
        off += k
    if has_res:
        acc = acc + res_ref[...]
    o_ref[...] = acc.astype(o_ref.dtype)


def _mm(a_list, w, layer, *, tn, tm, res=None, out_dtype=F32, name="proj"):
    m = a_list[0].shape[0]
    _, k, n = w.shape
    assert sum(a.shape[1] for a in a_list) == k
    in_specs = [pl.BlockSpec((tm, a.shape[1]), lambda j, i: (i, 0)) for a in a_list]
    in_specs.append(pl.BlockSpec((None, k, tn), lambda j, i: (layer, 0, j)))
    args = list(a_list) + [w]
    if res is not None:
        in_specs.append(pl.BlockSpec((tm, tn), lambda j, i: (i, j)))
        args.append(res)
    a_bytes = sum(a.dtype.itemsize * a.shape[1] for a in a_list) * tm
    vmem = 2 * a_bytes + 2 * k * tn * 4 + k * tn * 2 + 6 * tm * tn * 4 + (4 << 20)
    return pl.pallas_call(
        functools.partial(_mm_kernel, n_a=len(a_list), has_res=res is not None),
        grid=(n // tn, m // tm),
        in_specs=in_specs,
        out_specs=pl.BlockSpec((tm, tn), lambda j, i: (i, j)),
        out_shape=jax.ShapeDtypeStruct((m, n), out_dtype),
        compiler_params=_cparams(("arbitrary", "arbitrary"), vmem),
        name=name,
    )(*args)


def _group_norm_silu(y, gn_g, gn_b, store):
    for g in range(y.shape[1] // GROUP_W):
        sl = slice(g * GROUP_W, (g + 1) * GROUP_W)
        yg = y[:, sl]
        mu = jnp.mean(yg, axis=-1, keepdims=True)
        d = yg - mu
        var = jnp.mean(d * d, axis=-1, keepdims=True)
        yn = d * lax.rsqrt(var + EPS) * gn_g[:, sl] + gn_b[:, sl]
        store(sl, _silu(yn))


CONV_PAD = 32


def _causal_taps(src_ref, w_ref, t0, tt, first_off, n_taps, init):
    acc = init
    for s in range(SUBLANES):
        part = None
        for k in range(n_taps):
            off = first_off + k
            if off % SUBLANES != s:
                continue
            base = pl.multiple_of(t0 + (off - s), SUBLANES)
            term = src_ref[pl.ds(base, tt + SUBLANES), :] * w_ref[k:k + 1, :]
            part = term if part is None else part + term
        if part is not None:
            acc = acc + part[s:s + tt, :]
    return acc


def _last_seq(b):
    return jnp.minimum(b, BATCH - 1)


def _prompt_or_zero(body, y_ref):
    b = pl.program_id(1)

    @pl.when(b < BATCH)
    def _():
        body()

    @pl.when(b == BATCH)
    def _():
        y_ref[...] = jnp.zeros(y_ref.shape, y_ref.dtype)


def _conv_p_kernel(*refs):
    _prompt_or_zero(functools.partial(_conv_p_body, *refs), refs[6])


def _conv_p_body(cv_ref, cg_ref, w_ref, b_ref, gng_ref, gnb_ref, y_ref, st_ref, upad_ref):
    t_len = cv_ref.shape[0]
    tc = cv_ref.shape[1]
    upad_ref[0:CONV_PAD, :] = jnp.zeros((CONV_PAD, tc), F32)
    upad_ref[CONV_PAD + t_len:CONV_PAD + t_len + SUBLANES, :] = jnp.zeros((SUBLANES, tc), F32)

    def glu(c, carry):
        t0 = pl.multiple_of(c * LRU_TT, LRU_TT)
        u = cv_ref[pl.ds(t0, LRU_TT), :] * _sigmoid(cg_ref[pl.ds(t0, LRU_TT), :])
        upad_ref[pl.ds(CONV_PAD + t0, LRU_TT), :] = u
        return carry

    lax.fori_loop(0, t_len // LRU_TT, glu, 0)
    st_ref[0] = upad_ref[CONV_PAD + t_len - (CONV_K - 1):CONV_PAD + t_len, :]

    bias = b_ref[...]
    gn_g = gng_ref[...]
    gn_b = gnb_ref[...]
    shift = CONV_PAD - (CONV_K - 1)

    def chunk(c, carry):
        t0 = pl.multiple_of(c * CONV_TT, CONV_TT)
        acc = _causal_taps(upad_ref, w_ref, t0, CONV_TT, shift, CONV_K,
                           jnp.zeros((CONV_TT, tc), F32) + bias)

        def store(sl, v):
            y_ref[pl.ds(t0, CONV_TT), sl] = v.astype(y_ref.dtype)

        _group_norm_silu(acc, gn_g, gn_b, store)
        return carry

    lax.fori_loop(0, t_len // CONV_TT, chunk, 0)


def _conv_prompt(z, conv_w, conv_b, gn_g, gn_b):
    tc = CONV_TC
    nct = C_CONV // tc
    return pl.pallas_call(
        _conv_p_kernel,
        grid=(nct, BATCH + 1),
        in_specs=[pl.BlockSpec((SEQ, tc), lambda c, b: (_last_seq(b), c)),
                  pl.BlockSpec((SEQ, tc), lambda c, b: (_last_seq(b), c + nct)),
                  pl.BlockSpec((CONV_K, tc), lambda c, b: (0, c)),
                  pl.BlockSpec((1, tc), lambda c, b: (0, c)),
                  pl.BlockSpec((1, tc), lambda c, b: (0, c)),
                  pl.BlockSpec((1, tc), lambda c, b: (0, c))],
        out_specs=[pl.BlockSpec((SEQ, tc), lambda c, b: (b, c)),
                   pl.BlockSpec((1, CONV_K - 1, tc), lambda c, b: (_last_seq(b), 0, c))],
        out_shape=[jax.ShapeDtypeStruct((N_TOK, C_CONV), BF16),
                   jax.ShapeDtypeStruct((BATCH, CONV_K - 1, C_CONV), F32)],
        scratch_shapes=[pltpu.VMEM((CONV_PAD + SEQ + SUBLANES, tc), F32)],
        compiler_params=_cparams(("arbitrary", "arbitrary"), 8 * SEQ * tc * 4),
        name="conv_prompt",
    )(z, z, conv_w, conv_b.reshape(1, -1), gn_g.reshape(1, -1), gn_b.reshape(1, -1))


def _log_sigmoid(x):
    return -(jnp.maximum(-x, 0.0) + jnp.log(1.0 + jnp.exp(-jnp.abs(x))))


def _lru_gates(xc, wa, ba, wi, bi, log_sig_lam):
    xb = xc.astype(BF16)
    r = _sigmoid(jnp.dot(xb, wa, preferred_element_type=F32) + ba)
    i = _sigmoid(jnp.dot(xb, wi, preferred_element_type=F32) + bi)
    a = jnp.exp(RG_C * r * log_sig_lam)
    return a, jnp.sqrt(1.0 - a * a) * (i * xc)


LRU_PAD = 8


def _lru_p_kernel(*refs):
    _prompt_or_zero(functools.partial(_lru_p_body, *refs), refs[9])


def _lru_p_body(lx_ref, lg_ref, cw_ref, cb_ref, wa_ref, ba_ref, wi_ref, bi_ref, lam_ref,
                y_ref, buf_ref, h_ref, xpad_ref, a_ref, b_ref):
    t_len = lx_ref.shape[0]
    tc = lx_ref.shape[1]
    n_heads = tc // LRU_HD
    xpad_ref[0:LRU_PAD, :] = jnp.zeros((LRU_PAD, tc), F32)
    xpad_ref[LRU_PAD + t_len:LRU_PAD + t_len + SUBLANES, :] = jnp.zeros((SUBLANES, tc), F32)

    def copy_in(c, carry):
        t0 = pl.multiple_of(c * LRU_TT, LRU_TT)
        xpad_ref[pl.ds(LRU_PAD + t0, LRU_TT), :] = lx_ref[pl.ds(t0, LRU_TT), :]
        return carry

    lax.fori_loop(0, t_len // LRU_TT, copy_in, 0)
    buf_ref[0] = lx_ref[t_len - (LRU_CONV_K - 1):t_len, :]

    cb = cb_ref[...]
    log_sig_lam = _log_sigmoid(lam_ref[...])
    shift = LRU_PAD - (LRU_CONV_K - 1)

    def gates(c, carry):
        t0 = pl.multiple_of(c * LRU_TT, LRU_TT)
        xc = _causal_taps(xpad_ref, cw_ref, t0, LRU_TT, shift, LRU_CONV_K,
                          jnp.zeros((LRU_TT, tc), F32) + cb)
        for hh in range(n_heads):
            sl = slice(hh * LRU_HD, (hh + 1) * LRU_HD)
            a, b = _lru_gates(xc[:, sl], wa_ref[hh].astype(BF16), ba_ref[:, sl],
                              wi_ref[hh].astype(BF16), bi_ref[:, sl], log_sig_lam[:, sl])
            a_ref[pl.ds(t0, LRU_TT), sl] = a
            b_ref[pl.ds(t0, LRU_TT), sl] = b
        return carry

    lax.fori_loop(0, t_len // LRU_TT, gates, 0)

    row = lax.broadcasted_iota(I32, (SUBLANES, tc), 0)

    def scan(i, h):
        t0 = pl.multiple_of(i * SUBLANES, SUBLANES)
        a = a_ref[pl.ds(t0, SUBLANES), :]
        b = b_ref[pl.ds(t0, SUBLANES), :]
        for d in (1, 2, 4):
            a_sh = pltpu.roll(a, d, axis=0)
            b_sh = pltpu.roll(b, d, axis=0)
            m = row >= d
            b = jnp.where(m, a * b_sh + b, b)
            a = jnp.where(m, a * a_sh, a)
        hb = a * h + b
        b_ref[pl.ds(t0, SUBLANES), :] = hb
        return hb[SUBLANES - 1:SUBLANES, :]

    h_last = lax.fori_loop(0, t_len // SUBLANES, scan, jnp.zeros((1, tc), F32), unroll=4)
    h_ref[0] = h_last

    def gate_out(c, carry):
        t0 = pl.multiple_of(c * LRU_TT, LRU_TT)
        y = b_ref[pl.ds(t0, LRU_TT), :] * _gelu_tanh(lg_ref[pl.ds(t0, LRU_TT), :])
        y_ref[pl.ds(t0, LRU_TT), :] = y.astype(y_ref.dtype)
        return carry

    lax.fori_loop(0, t_len // LRU_TT, gate_out, 0)


def _lru_prompt(z, cw, cb, wa, ba, wi, bi, lam):
    tc = CONV_TC
    nct = C_LRU // tc
    hpb = tc // LRU_HD
    col0 = 2 * C_CONV // tc
    vec = lambda v: v.reshape(1, -1)
    return pl.pallas_call(
        _lru_p_kernel,
        grid=(nct, BATCH + 1),
        in_specs=[pl.BlockSpec((SEQ, tc), lambda c, b: (_last_seq(b), c + col0)),
                  pl.BlockSpec((SEQ, tc), lambda c, b: (_last_seq(b), c + col0 + nct)),
                  pl.BlockSpec((LRU_CONV_K, tc), lambda c, b: (0, c)),
                  pl.BlockSpec((1, tc), lambda c, b: (0, c)),
                  pl.BlockSpec((hpb, LRU_HD, LRU_HD), lambda c, b: (c, 0, 0)),
                  pl.BlockSpec((1, tc), lambda c, b: (0, c)),
                  pl.BlockSpec((hpb, LRU_HD, LRU_HD), lambda c, b: (c, 0, 0)),
                  pl.BlockSpec((1, tc), lambda c, b: (0, c)),
                  pl.BlockSpec((1, tc), lambda c, b: (0, c))],
        out_specs=[pl.BlockSpec((SEQ, tc), lambda c, b: (b, c)),
                   pl.BlockSpec((1, LRU_CONV_K - 1, tc), lambda c, b: (_last_seq(b), 0, c)),
                   pl.BlockSpec((1, 1, tc), lambda c, b: (_last_seq(b), 0, c))],
        out_shape=[jax.ShapeDtypeStruct((N_TOK, C_LRU), BF16),
                   jax.ShapeDtypeStruct((BATCH, LRU_CONV_K - 1, C_LRU), F32),
                   jax.ShapeDtypeStruct((BATCH, 1, C_LRU), F32)],
        scratch_shapes=[pltpu.VMEM((LRU_PAD + SEQ + SUBLANES, tc), F32),
                        pltpu.VMEM((SEQ, tc), F32),
                        pltpu.VMEM((SEQ, tc), F32)],
        compiler_params=_cparams(("arbitrary", "arbitrary"), 12 * SEQ * tc * 4),
        name="lru_prompt",
    )(z, z, cw, vec(cb), wa, vec(ba), wi, vec(bi), vec(lam))


def _mixer_s_kernel(cv_ref, cg_ref, lx_ref, lg_ref, st_ref, lst_ref, h0_ref,
                    cw_ref, cb_ref, gng_ref, gnb_ref, lcw_ref, lcb_ref,
                    wa_ref, ba_ref, wi_ref, bi_ref, lam_ref, yc_in, yl_in,
                    yc_ref, yl_ref, nst_ref, nlst_ref, nh_ref):
    del yc_in, yl_in
    u = cv_ref[...] * _sigmoid(cg_ref[...])
    acc = u * cw_ref[CONV_K - 1:CONV_K, :] + cb_ref[...]
    for k in range(CONV_K - 1):
        row = st_ref[:, k, :]
        acc = acc + row * cw_ref[k:k + 1, :]
        if k > 0:
            nst_ref[:, k - 1, :] = row
    nst_ref[:, CONV_K - 2, :] = u

    def store_c(sl, v):
        yc_ref[:, sl] = v.astype(yc_ref.dtype)

    _group_norm_silu(acc, gng_ref[...], gnb_ref[...], store_c)

    lx = lx_ref[...]
    xc = lx * lcw_ref[LRU_CONV_K - 1:LRU_CONV_K, :] + lcb_ref[...]
    for k in range(LRU_CONV_K - 1):
        row = lst_ref[:, k, :]
        xc = xc + row * lcw_ref[k:k + 1, :]
        if k > 0:
            nlst_ref[:, k - 1, :] = row
    nlst_ref[:, LRU_CONV_K - 2, :] = lx

    log_sig_lam = _log_sigmoid(lam_ref[...])
    for hh in range(LRU_HEADS):
        sl = slice(hh * LRU_HD, (hh + 1) * LRU_HD)
        a, b = _lru_gates(xc[:, sl], wa_ref[hh].astype(BF16), ba_ref[:, sl],
                          wi_ref[hh].astype(BF16), bi_ref[:, sl], log_sig_lam[:, sl])
        h = a * h0_ref[:, sl] + b
        nh_ref[:, sl] = h
        yl_ref[:, sl] = (h * _gelu_tanh(lg_ref[:, sl])).astype(yl_ref.dtype)


def _mixer_sample(z, st, lst, h0, layer, yc, yl, cw, cb, gn_g, gn_b, lcw, lcb, wa, ba, wi, bi, lam):
    bt = S_BT
    rb0 = N_P // bt
    vec = lambda v: v.reshape(1, -1)
    zspec = lambda col: pl.BlockSpec((bt, C_CONV), lambda i: (i + rb0, col))
    full = lambda shape: pl.BlockSpec(shape, lambda i: (0,) * len(shape))
    any_spec = pl.BlockSpec(memory_space=pl.ANY)
    return pl.pallas_call(
        _mixer_s_kernel,
        grid=(N_S // bt,),
        in_specs=[zspec(0), zspec(1), zspec(2), zspec(3),
                  pl.BlockSpec((None, bt, CONV_K - 1, C_CONV), lambda i: (layer, i, 0, 0)),
                  pl.BlockSpec((None, bt, LRU_CONV_K - 1, C_LRU), lambda i: (layer, i, 0, 0)),
                  pl.BlockSpec((None, bt, C_LRU), lambda i: (layer, i, 0)),
                  full((CONV_K, C_CONV)), full((1, C_CONV)), full((1, C_CONV)), full((1, C_CONV)),
                  full((LRU_CONV_K, C_LRU)), full((1, C_LRU)),
                  full((LRU_HEADS, LRU_HD, LRU_HD)), full((1, C_LRU)),
                  full((LRU_HEADS, LRU_HD, LRU_HD)), full((1, C_LRU)), full((1, C_LRU)),
                  any_spec, any_spec],
        out_specs=[pl.BlockSpec((bt, C_CONV), lambda i: (i + rb0, 0)),
                   pl.BlockSpec((bt, C_LRU), lambda i: (i + rb0, 0)),
                   pl.BlockSpec((bt, CONV_K - 1, C_CONV), lambda i: (i, 0, 0)),
                   pl.BlockSpec((bt, LRU_CONV_K - 1, C_LRU), lambda i: (i, 0, 0)),
                   pl.BlockSpec((bt, C_LRU), lambda i: (i, 0))],
        out_shape=[jax.ShapeDtypeStruct(yc.shape, yc.dtype),
                   jax.ShapeDtypeStruct(yl.shape, yl.dtype),
                   jax.ShapeDtypeStruct(st.shape[1:], F32),
                   jax.ShapeDtypeStruct(lst.shape[1:], F32),
                   jax.ShapeDtypeStruct(h0.shape[1:], F32)],
        input_output_aliases={18: 0, 19: 1},
        compiler_params=_cparams(("arbitrary",), 48 << 20),
        name="mixer_sample",
    )(z, z, z, z, st, lst, h0, cw, vec(cb), vec(gn_g), vec(gn_b), lcw, vec(lcb),
      wa, vec(ba), wi, vec(bi), vec(lam), yc, yl)


def _attn_p_kernel(q_ref, k_ref, v_ref, o_ref):
    i = pl.program_id(0)

    @pl.when(i < N_P // ATT_TQ)
    def _():
        _attn_p_body(q_ref, k_ref, v_ref, o_ref)

    @pl.when(i == N_P // ATT_TQ)
    def _():
        o_ref[...] = jnp.zeros(o_ref.shape, o_ref.dtype)


def _attn_p_body(q_ref, k_ref, v_ref, o_ref):
    scale = MEM_HD ** -0.5
    for h in range(MEM_HEADS):
        sl = slice(h * MEM_HD, (h + 1) * MEM_HD)
        q = q_ref[:, sl].astype(BF16)
        k = k_ref[:, sl].astype(BF16)
        v = v_ref[:, sl].astype(BF16)
        s = lax.dot_general(q, k, (((1,), (1,)), ((), ())), preferred_element_type=F32) * scale
        p = jnp.exp(s - jnp.max(s, axis=-1, keepdims=True))
        pr = p / jnp.sum(p, axis=-1, keepdims=True)
        o = jnp.dot(pr.astype(BF16), v, preferred_element_type=F32)
        o_ref[:, sl] = o.astype(o_ref.dtype)


def _attn_prompt(q, k, v):
    nq = SEQ // ATT_TQ
    return pl.pallas_call(
        _attn_p_kernel,
        grid=(BATCH * nq + 1,),
        in_specs=[pl.BlockSpec((ATT_TQ, MEM_W), lambda i: (i, 0)),
                  pl.BlockSpec((MEM_LEN, MEM_W), lambda i: (_last_seq(i // nq), 0)),
                  pl.BlockSpec((MEM_LEN, MEM_W), lambda i: (_last_seq(i // nq), 0))],
        out_specs=pl.BlockSpec((ATT_TQ, MEM_W), lambda i: (i, 0)),
        out_shape=jax.ShapeDtypeStruct((N_TOK, MEM_W), BF16),
        compiler_params=_cparams(("arbitrary",), 32 << 20),
        name="attn_prompt",
    )(q, k, v)


def _attn_s_kernel(q_ref, k_ref, v_ref, o_in, o_ref, stage_ref):
    del o_in
    jj = pl.program_id(1)
    scale = MEM_HD ** -0.5
    for bb in range(ATT_S_BT):
        r = jj * ATT_S_BT + bb
        q = q_ref[pl.ds(r, 1)]
        s = jnp.sum(k_ref[bb] * q, axis=-1, keepdims=True) * scale
        p = jnp.exp(s - jnp.max(s, axis=0, keepdims=True))
        l = jnp.sum(p, axis=0)
        o = jnp.sum(p * v_ref[bb], axis=0) / l
        for h in range(MEM_HEADS):
            stage_ref[pl.ds(r, 1), h * MEM_HD:(h + 1) * MEM_HD] = o[h:h + 1, :]

    @pl.when(jj == pl.num_programs(1) - 1)
    def _():
        o_ref[...] = stage_ref[...].astype(o_ref.dtype)


def _attn_sample(q4, kc, vc, layer, o):
    bt = S_BT
    inner = bt // ATT_S_BT
    rb0 = N_P // bt
    cache_spec = pl.BlockSpec((None, ATT_S_BT, MEM_LEN, MEM_HEADS, MEM_HD),
                              lambda i, j: (layer, i * inner + j, 0, 0, 0))
    return pl.pallas_call(
        _attn_s_kernel,
        grid=(N_S // bt, inner),
        in_specs=[pl.BlockSpec((bt, MEM_HEADS, MEM_HD), lambda i, j: (i, 0, 0)),
                  cache_spec, cache_spec,
                  pl.BlockSpec(memory_space=pl.ANY)],
        out_specs=pl.BlockSpec((bt, MEM_W), lambda i, j: (i + rb0, 0)),
        out_shape=jax.ShapeDtypeStruct(o.shape, o.dtype),
        scratch_shapes=[pltpu.VMEM((bt, MEM_W), F32)],
        input_output_aliases={3: 0},
        compiler_params=_cparams(("arbitrary", "arbitrary"), 48 << 20),
        name="attn_sample",
    )(q4, kc, vc, o)


def _norm_router_kernel(x_ref, g_ref, wr_ref, br_ref, xp_ref, ids_ref, gates_ref):
    xn = _rms(x_ref[...], g_ref[...])
    xp_ref[...] = pltpu.pack_elementwise([xn[:, :HALF], xn[:, HALF:]], packed_dtype=BF16)
    logits = jnp.dot(xn.astype(BF16), wr_ref[...], preferred_element_type=F32) + br_ref[...]
    lane = lax.broadcasted_iota(I32, logits.shape, 1)
    neg = jnp.float32(-jnp.inf)

    def first_max(vals):
        m = jnp.max(vals, axis=-1, keepdims=True)
        idx = jnp.min(jnp.where(vals == m, lane, ROUTER_W), axis=-1, keepdims=True)
        return m, idx

    is_group = lane < N_GROUPS
    g_max, g_sel = first_max(jnp.where(is_group, logits, neg))
    p_g = 1.0 / jnp.sum(jnp.where(is_group, jnp.exp(logits - g_max), 0.0), axis=-1, keepdims=True)
    lo = N_GROUPS + EXPERTS_PER_GROUP * g_sel
    in_group = (lane >= lo) & (lane < lo + EXPERTS_PER_GROUP)
    e_logits = jnp.where(in_group, logits, neg)
    t1, i1 = first_max(e_logits)
    t2, i2 = first_max(jnp.where(lane == i1, neg, e_logits))
    e = jnp.exp(t2 - t1)
    w1 = p_g / (1.0 + e)
    w2 = p_g * e / (1.0 + e)
    ids_ref[...] = jnp.where(lane == 0, i1 - N_GROUPS, jnp.where(lane == 1, i2 - N_GROUPS, 0))
    gates_ref[...] = jnp.where(lane == 0, w1, jnp.where(lane == 1, w2, 0.0))


def _norm_router(x, g, wr, br):
    tm = TM_NORM
    return pl.pallas_call(
        _norm_router_kernel,
        grid=(N_TOK // tm,),
        in_specs=[pl.BlockSpec((tm, D_MODEL), lambda i: (i, 0)),
                  pl.BlockSpec((1, D_MODEL), lambda i: (0, 0)),
                  pl.BlockSpec((D_MODEL, ROUTER_W), lambda i: (0, 0)),
                  pl.BlockSpec((1, ROUTER_W), lambda i: (0, 0))],
        out_specs=[pl.BlockSpec((tm, HALF), lambda i: (i, 0)),
                   pl.BlockSpec((tm, ROUTER_W), lambda i: (i, 0)),
                   pl.BlockSpec((tm, ROUTER_W), lambda i: (i, 0))],
        out_shape=[jax.ShapeDtypeStruct((N_TOK, HALF), jnp.uint32),
                   jax.ShapeDtypeStruct((N_TOK, ROUTER_W), I32),
                   jax.ShapeDtypeStruct((N_TOK, ROUTER_W), F32)],
        compiler_params=_cparams(("arbitrary",), 8 * tm * D_MODEL * 4),
        name="norm_router",
    )(x, g.reshape(1, -1), wr, br)


def _unpack_x(xp):
    lo = pltpu.unpack_elementwise(xp, index=0, packed_dtype=BF16, unpacked_dtype=F32)
    hi = pltpu.unpack_elementwise(xp, index=1, packed_dtype=BF16, unpacked_dtype=F32)
    return lo.astype(BF16), hi.astype(BF16)


def _dot(a, w):
    return lax.dot_general(a, w, (((1,), (0,)), ((), ())), preferred_element_type=F32)


def _ffn_kernel(sbe_ref, sbb_ref, sbn_ref, sbz_ref, tok_ref, xp_hbm, wg_ref, wu_ref, wd_ref, ys_hbm,
                xbuf_ref, acc_ref, xsem, ysem, pend_ref):
    del sbe_ref
    sb = pl.program_id(0)
    j = pl.program_id(1)
    nj = pl.num_programs(1)
    slot = sb % 2

    @pl.when((sb == 0) & (j == 0))
    def _():
        pend_ref[0] = 0

    def row_copy(s, slot_, r):
        tok = tok_ref[sbb_ref[s] * MOE_BLOCK + r]
        return pltpu.make_async_copy(xp_hbm.at[pl.ds(tok, 1), :],
                                     xbuf_ref.at[slot_, pl.ds(r, 1), :], xsem.at[slot_])

    def start_blocks(s, slot_, lo, hi):
        def block(i, carry):
            def body(r, c):
                row_copy(s, slot_, i * MOE_BLOCK + r).start(priority=GATHER_PRIORITY)
                return c

            return lax.fori_loop(0, MOE_BLOCK, body, carry, unroll=ISSUE_UNROLL)

        lax.fori_loop(lo, jnp.minimum(hi, sbn_ref[s]), block, 0)

    def wait_rows(s, slot_):
        def block(i, carry):
            r0 = pl.multiple_of(i * MOE_BLOCK, MOE_BLOCK)
            pltpu.make_async_copy(xp_hbm.at[pl.ds(0, MOE_BLOCK), :],
                                  xbuf_ref.at[slot_, pl.ds(r0, MOE_BLOCK), :], xsem.at[slot_]).wait()
            return carry

        lax.fori_loop(0, sbn_ref[s], block, 0)

    def out_copy(i, blk):
        r0 = pl.multiple_of(i * MOE_BLOCK, MOE_BLOCK)
        r1 = pl.multiple_of(blk * MOE_BLOCK, MOE_BLOCK)
        return pltpu.make_async_copy(acc_ref.at[pl.ds(r0, MOE_BLOCK), :],
                                     ys_hbm.at[pl.ds(r1, MOE_BLOCK), :], ysem)

    def wait_out():
        def body(i, carry):
            out_copy(0, 0).wait()
            return carry

        lax.fori_loop(0, pend_ref[0], body, 0)
        pend_ref[0] = 0

    per_j = -(-NSUB // (D_EXPERT // TF))

    @pl.when((sb == 0) & (j == 0))
    def _():
        start_blocks(sb, slot, 0, NSUB)

    @pl.when(sb + 1 < pl.num_programs(0))
    def _():
        start_blocks(sb + 1, 1 - slot, j * per_j, (j + 1) * per_j)

    nsub = sbn_ref[sb]
    nzero = sbz_ref[sb]

    def zero_block(i, carry):
        r0 = pl.multiple_of(i * MOE_BLOCK, MOE_BLOCK)
        acc_ref[pl.ds(r0, MOE_BLOCK), :] = jnp.zeros((MOE_BLOCK, D_MODEL), F32)
        return carry

    def start_out(i, carry):
        out_copy(i, sbb_ref[sb] + i).start()
        return carry

    @pl.when(j == 0)
    def _():
        wait_rows(sb, slot)
        wait_out()
        lax.fori_loop(0, jnp.maximum(nsub, nzero), zero_block, 0)
        lax.fori_loop(0, nzero, start_out, 0)
        pend_ref[0] = nzero

    def sub(i, carry):
        r0 = pl.multiple_of(i * MOE_BLOCK, MOE_BLOCK)
        lo, hi = _unpack_x(xbuf_ref[slot, pl.ds(r0, MOE_BLOCK), :])
        g = _dot(lo, wg_ref[:HALF, :]) + _dot(hi, wg_ref[HALF:, :])
        u = _dot(lo, wu_ref[:HALF, :]) + _dot(hi, wu_ref[HALF:, :])
        h = (_silu(g) * u).astype(BF16)
        acc_ref[pl.ds(r0, MOE_BLOCK), :] += _dot(h, wd_ref[...])
        return carry

    lax.fori_loop(0, nsub, sub, 0)

    @pl.when((j == nj - 1) & (nsub > 0))
    def _():
        lax.fori_loop(0, nsub, start_out, 0)
        pend_ref[0] = nsub

    @pl.when((sb == pl.num_programs(0) - 1) & (j == nj - 1))
    def _():
        wait_out()


def _moe_ffn(xp, plan, w_gate, w_up, w_down, layer):
    nj = D_EXPERT // TF

    def hidden_slice(sb, j, sbn):
        return jnp.where(sbn[sb] > 0, j, nj - 1)

    up_spec = pl.BlockSpec((None, None, D_MODEL, TF),
                           lambda sb, j, sbe, sbb, sbn, sbz, tok: (layer, sbe[sb], 0, hidden_slice(sb, j, sbn)))
    down_spec = pl.BlockSpec((None, None, TF, D_MODEL),
                             lambda sb, j, sbe, sbb, sbn, sbz, tok: (layer, sbe[sb], hidden_slice(sb, j, sbn), 0))
    return pl.pallas_call(
        _ffn_kernel,
        grid_spec=pltpu.PrefetchScalarGridSpec(
            num_scalar_prefetch=5,
            grid=(N_SB, nj),
            in_specs=[pl.BlockSpec(memory_space=pl.ANY), up_spec, up_spec, down_spec],
            out_specs=pl.BlockSpec(memory_space=pl.ANY),
            scratch_shapes=[pltpu.VMEM((2, SB_ROWS, HALF), jnp.uint32),
                            pltpu.VMEM((SB_ROWS, D_MODEL), F32),
                            pltpu.SemaphoreType.DMA((2,)),
                            pltpu.SemaphoreType.DMA(()),
                            pltpu.SMEM((1,), I32)]),
        out_shape=jax.ShapeDtypeStruct((N_ROWS, D_MODEL), F32),
        compiler_params=_cparams(("arbitrary", "arbitrary"), VMEM_CAP),
        name="moe_ffn",
    )(plan["sb_e"], plan["sb_blk"], plan["sb_n"], plan["sb_z"], plan["row_tok"], xp, w_gate, w_up, w_down)


def _combine_kernel(dest_ref, x_ref, gates_ref, g_ref, ys_hbm, out_a, out_b, buf_ref, sem, *, final):
    i = pl.program_id(0)
    n = pl.num_programs(0)

    def row_copy(step, slot, t, k):
        s = (step * COMB_TM + t) * TOP_K + k
        return pltpu.make_async_copy(ys_hbm.at[pl.ds(dest_ref[s], 1), :],
                                     buf_ref.at[slot, k, pl.ds(t, 1), :], sem.at[slot])

    def start_all(step, slot):
        def body(t, carry):
            for k in range(TOP_K):
                row_copy(step, slot, t, k).start(priority=k % 2)
            return carry

        lax.fori_loop(0, COMB_TM, body, 0, unroll=ISSUE_UNROLL)

    def wait_all(step, slot):
        for k in range(TOP_K):
            pltpu.make_async_copy(ys_hbm.at[pl.ds(0, COMB_TM), :], buf_ref.at[slot, k], sem.at[slot]).wait()

    slot = i % 2

    @pl.when(i == 0)
    def _():
        start_all(i, slot)

    @pl.when(i + 1 < n)
    def _():
        start_all(i + 1, 1 - slot)

    wait_all(i, slot)
    gates = gates_ref[...]
    y = x_ref[...] + gates[:, 0:1] * buf_ref[slot, 0] + gates[:, 1:2] * buf_ref[slot, 1]
    yn = _rms(y, g_ref[...])
    if final:
        @pl.when(i < n - 1)
        def _():
            out_a[...] = yn

        @pl.when(i == n - 1)
        def _():
            out_b[...] = yn
    else:
        out_a[...] = y
        out_b[...] = yn.astype(out_b.dtype)


def _combine(x, ys, dest, gates, g_next, *, final):
    tm = COMB_TM
    assert N_S == tm
    n_p_blocks = N_P // tm
    if final:
        out_specs = [pl.BlockSpec((tm, D_MODEL), lambda i, d: (jnp.minimum(i, n_p_blocks - 1), 0)),
                     pl.BlockSpec((tm, D_MODEL), lambda i, d: (0, 0))]
        out_shape = [jax.ShapeDtypeStruct((N_P, D_MODEL), F32),
                     jax.ShapeDtypeStruct((N_S, D_MODEL), F32)]
    else:
        out_specs = [pl.BlockSpec((tm, D_MODEL), lambda i, d: (i, 0)),
                     pl.BlockSpec((tm, D_MODEL), lambda i, d: (i, 0))]
        out_shape = [jax.ShapeDtypeStruct((N_TOK, D_MODEL), F32),
                     jax.ShapeDtypeStruct((N_TOK, D_MODEL), BF16)]
    return pl.pallas_call(
        functools.partial(_combine_kernel, final=final),
        grid_spec=pltpu.PrefetchScalarGridSpec(
            num_scalar_prefetch=1,
            grid=(N_TOK // tm,),
            in_specs=[pl.BlockSpec((tm, D_MODEL), lambda i, d: (i, 0)),
                      pl.BlockSpec((tm, ROUTER_W), lambda i, d: (i, 0)),
                      pl.BlockSpec((1, D_MODEL), lambda i, d: (0, 0)),
                      pl.BlockSpec(memory_space=pl.ANY)],
            out_specs=out_specs,
            scratch_shapes=[pltpu.VMEM((2, TOP_K, tm, D_MODEL), F32),
                            pltpu.SemaphoreType.DMA((2,))]),
        out_shape=out_shape,
        compiler_params=_cparams(("arbitrary",), 40 << 20),
        name="moe_combine",
    )(dest, x, gates, g_next.reshape(1, -1), ys)


def _dispatch_plan(ids):
    flat_e = ids[:, :TOP_K].reshape(-1)
    onehot = (flat_e[:, None] == jnp.arange(N_EXPERTS, dtype=I32)[None, :]).astype(I32)
    csum = jnp.cumsum(onehot, axis=0)
    rank = jnp.sum(onehot * csum, axis=1) - 1
    counts = csum[-1]
    nblk = (counts + MOE_BLOCK - 1) // MOE_BLOCK
    bend = jnp.cumsum(nblk)
    bstart = bend - nblk
    dest = (jnp.sum(onehot * bstart[None, :], axis=1) * MOE_BLOCK + rank).astype(I32)
    row_tok = jnp.zeros((N_ROWS,), I32).at[dest].set(jnp.arange(N_SLOT, dtype=I32) // TOP_K)

    nsup = (nblk + NSUB - 1) // NSUB
    sup_end = jnp.cumsum(nsup)
    sup_start = sup_end - nsup
    total_sup = sup_end[-1]
    sb = jnp.arange(N_SB, dtype=I32)
    e_of = jnp.minimum(jnp.sum((sup_end[None, :] <= sb[:, None]).astype(I32), axis=1), N_EXPERTS - 1)
    q = sb - sup_start[e_of]
    real = sb < total_sup
    n_real = jnp.clip(nblk[e_of] - NSUB * q, 0, NSUB)
    blk_real = bstart[e_of] + NSUB * q
    blk_tail = bend[-1] + NSUB * (sb - total_sup)
    n_tail = jnp.clip(N_BLOCKS - blk_tail, 0, NSUB)
    e_last = e_of[total_sup - 1]
    return dict(
        dest=dest, row_tok=row_tok,
        sb_e=jnp.where(real, e_of, e_last).astype(I32),
        sb_blk=jnp.where(real, blk_real, jnp.minimum(blk_tail, N_BLOCKS - 1)).astype(I32),
        sb_n=jnp.where(real, n_real, 0).astype(I32),
        sb_z=jnp.where(real, 0, n_tail).astype(I32))


def kernel(x_prompt, x_sample, mem_prompt, state_conv, state_lru_conv, state_lru_h, cache_mem_k, cache_mem_v, norm_mix, w_in, conv_w, conv_b, conv_gn_g, conv_gn_b, lru_conv_w, lru_conv_b, lru_wa, lru_ba, lru_wi, lru_bi, lru_lambda, w_out, norm_attn, norm_mem_kv, w_q, w_k, w_v, w_o, norm_ffn, w_router_g, b_router_g, w_router_e, b_router_e, w_gate, w_up, w_down, norm_final):
    x = jnp.concatenate([x_prompt.reshape(N_P, D_MODEL), x_sample.reshape(N_S, D_MODEL)], axis=0)
    mem = mem_prompt.reshape(BATCH * MEM_LEN, D_MODEL)
    xn = _norm(x, norm_mix[0], tm=TM_NORM)

    conv_p, lruc_p, h_p, mk_p, mv_p, conv_s, lruc_s, h_s = ([] for _ in range(8))
    for l in range(DEPTH):
        z = _mm([xn], w_in, l, tn=512, tm=TM, name="w_in")
        yc, cst = _conv_prompt(z, conv_w[l], conv_b[l], conv_gn_g[l], conv_gn_b[l])
        yl, lst, hl = _lru_prompt(z, lru_conv_w[l], lru_conv_b[l], lru_wa[l], lru_ba[l],
                                  lru_wi[l], lru_bi[l], lru_lambda[l])
        yc, yl, cst_s, lst_s, hl_s = _mixer_sample(
            z, state_conv, state_lru_conv, state_lru_h, l, yc, yl,
            conv_w[l], conv_b[l], conv_gn_g[l], conv_gn_b[l], lru_conv_w[l], lru_conv_b[l],
            lru_wa[l], lru_ba[l], lru_wi[l], lru_bi[l], lru_lambda[l])
        x = _mm([yc, yl], w_out, l, tn=512, tm=TM, res=x, name="w_out")
        conv_p.append(cst); lruc_p.append(lst); h_p.append(hl.reshape(BATCH, C_LRU))
        conv_s.append(cst_s); lruc_s.append(lst_s); h_s.append(hl_s)

        xn = _norm(x, norm_attn[l], tm=TM_NORM)
        q = _mm([xn], w_q, l, tn=512, tm=TM, name="w_q")
        mn = _norm(mem, norm_mem_kv[l], tm=256)
        k_p = _mm([mn], w_k, l, tn=512, tm=BATCH * MEM_LEN, name="w_k")
        v_p = _mm([mn], w_v, l, tn=512, tm=BATCH * MEM_LEN, name="w_v")
        o = _attn_prompt(q, k_p, v_p)
        o = _attn_sample(q[N_P:].reshape(N_S, MEM_HEADS, MEM_HD), cache_mem_k, cache_mem_v, l, o)
        x = _mm([o], w_o, l, tn=1024, tm=TM, res=x, name="w_o")
        mk_p.append(k_p.reshape(BATCH, MEM_LEN, MEM_HEADS, MEM_HD))
        mv_p.append(v_p.reshape(BATCH, MEM_LEN, MEM_HEADS, MEM_HD))

        wr = jnp.concatenate([w_router_g[l], w_router_e[l].reshape(D_MODEL, N_EXPERTS),
                              jnp.zeros((D_MODEL, ROUTER_W - N_GROUPS - N_EXPERTS), F32)],
                             axis=1).astype(BF16)
        br = jnp.concatenate([b_router_g[l], b_router_e[l].reshape(N_EXPERTS),
                              jnp.zeros((ROUTER_W - N_GROUPS - N_EXPERTS,), F32)]).reshape(1, ROUTER_W)
        xp, ids, gates = _norm_router(x, norm_ffn[l], wr, br)
        plan = _dispatch_plan(ids)
        ys = _moe_ffn(xp, plan, w_gate, w_up, w_down, l)
        if l < DEPTH - 1:
            x, xn = _combine(x, ys, plan["dest"], gates, norm_mix[l + 1], final=False)
        else:
            y_p, y_s = _combine(x, ys, plan["dest"], gates, norm_final, final=True)

    y_prompt = y_p.reshape(BATCH, SEQ, D_MODEL)
    y_sample = y_s.reshape(DEC_BATCH, 1, D_MODEL)
    return (y_prompt, y_sample, jnp.stack(conv_p), jnp.stack(lruc_p), jnp.stack(h_p),
            jnp.stack(mk_p), jnp.stack(mv_p), jnp.stack(conv_s), jnp.stack(lruc_s), jnp.stack(h_s))
```

```python
import functools
import math

import jax
import jax.numpy as jnp
from jax import lax
from jax.experimental import pallas as pl
from jax.experimental.pallas import tpu as pltpu

F32 = jnp.float32
BF16 = jnp.bfloat16
I32 = jnp.int32

D_MODEL = 4096
BATCH = 4
SEQ = 2048
DEPTH = 2
DEC_BATCH = 128
C_CONV = D_MODEL // 2
C_LRU = D_MODEL // 2
D_IN = 2 * C_CONV + 2 * C_LRU
CONV_GROUPS = 16
GROUP_W = C_CONV // CONV_GROUPS
CONV_K = 31
LRU_HEADS = 16
LRU_HD = C_LRU // LRU_HEADS
LRU_CONV_K = 4
RG_C = 8.0
MEM_LEN = 256
MEM_HEADS = 4
MEM_HD = D_MODEL // 16
MEM_W = MEM_HEADS * MEM_HD
N_GROUPS = 4
EXPERTS_PER_GROUP = 8
N_EXPERTS = N_GROUPS * EXPERTS_PER_GROUP
TOP_K = 2
D_EXPERT = D_MODEL // 4
EPS = 1e-6

N_P = BATCH * SEQ
N_S = DEC_BATCH
N_TOK = N_P + N_S
N_SLOT = N_TOK * TOP_K

LANES = 128
SUBLANES = 8
VMEM_CAP = 56 * 1024 * 1024

TM = 1040
TM_NORM = 416
MOE_BLOCK = 128
N_BLOCKS = -(-(N_SLOT + N_EXPERTS * (MOE_BLOCK - 1)) // MOE_BLOCK)
N_ROWS = N_BLOCKS * MOE_BLOCK
TF = 256
ROUTER_W = LANES
HALF = D_MODEL // 2
CONV_TT = 64
CONV_TC = 256
LRU_TT = 256
COMB_TM = 128
ATT_TQ = 512
S_BT = 16
ATT_S_BT = 4
ISSUE_UNROLL = 8
GATHER_PRIORITY = 1
NSUB = 6
SB_ROWS = NSUB * MOE_BLOCK
N_SB = (N_BLOCKS + (NSUB - 1) * (N_EXPERTS + 1)) // NSUB + 1


def _cparams(sem, vmem_bytes):
    limit = min(VMEM_CAP, max(32 * 1024 * 1024, int(vmem_bytes)))
    return pltpu.CompilerParams(dimension_semantics=sem, vmem_limit_bytes=limit)


def _sigmoid(x):
    return 1.0 / (1.0 + jnp.exp(-x))


def _silu(x):
    return x * _sigmoid(x)


def _gelu_tanh(x):
    c = math.sqrt(2.0 / math.pi)
    return 0.5 * x * (1.0 + jnp.tanh(c * (x + 0.044715 * (x * x * x))))


def _rms(x, g):
    ms = jnp.mean(x * x, axis=-1, keepdims=True)
    return x * lax.rsqrt(ms + EPS) * g


def _norm_kernel(x_ref, g_ref, o_ref):
    o_ref[...] = _rms(x_ref[...], g_ref[...]).astype(o_ref.dtype)


def _norm(x, g, *, tm, out_dtype=BF16, row_block0=0, n_rows=None):
    n_rows = x.shape[0] if n_rows is None else n_rows
    d = x.shape[1]
    return pl.pallas_call(
        _norm_kernel,
        grid=(n_rows // tm,),
        in_specs=[pl.BlockSpec((tm, d), lambda i: (i + row_block0, 0)),
                  pl.BlockSpec((1, d), lambda i: (0, 0))],
        out_specs=pl.BlockSpec((tm, d), lambda i: (i, 0)),
        out_shape=jax.ShapeDtypeStruct((n_rows, d), out_dtype),
        compiler_params=_cparams(("arbitrary",), 6 * tm * d * 4),
        name="rmsnorm",
    )(x, g.reshape(1, d))


def _mm_kernel(*refs, n_a, has_res):
    a_refs = refs[:n_a]
    w_ref = refs[n_a]
    res_ref = refs[n_a + 1] if has_res else None
    o_ref = refs[n_a + 1 + int(has_res)]
    acc = None
    off = 0
    for a_ref in a_refs:
        k = a_ref.shape[1]
        part = lax.dot_general(a_ref[...].astype(BF16), w_ref[off:off + k, :], (((1,), (0,)), ((), ())),
                               preferred_element_type=F32)
        acc = part if acc is None else acc + part
        off += k
    if has_res:
        acc = acc + res_ref[...]
    o_ref[...] = acc.astype(o_ref.dtype)


def _mm(a_list, w, layer, *, tn, tm, res=None, out_dtype=F32, name="proj"):
    m = a_list[0].shape[0]
    _, k, n = w.shape
    assert sum(a.shape[1] for a in a_list) == k
    in_specs = [pl.BlockSpec((tm, a.shape[1]), lambda j, i: (i, 0)) for a in a_list]
    in_specs.append(pl.BlockSpec((None, k, tn), lambda j, i: (layer, 0, j), pipeline_mode=pl.Buffered(1)))
    args = list(a_list) + [w]
    if res is not None:
        in_specs.append(pl.BlockSpec((tm, tn), lambda j, i: (i, j)))
        args.append(res)
    a_bytes = sum(a.dtype.itemsize * a.shape[1] for a in a_list) * tm
    vmem = 2 * a_bytes + k * tn * 4 + 6 * tm * tn * 4 + (4 << 20)
    return pl.pallas_call(
        functools.partial(_mm_kernel, n_a=len(a_list), has_res=res is not None),
        grid=(n // tn, m // tm),
        in_specs=in_specs,
        out_specs=pl.BlockSpec((tm, tn), lambda j, i: (i, j)),
        out_shape=jax.ShapeDtypeStruct((m, n), out_dtype),
        compiler_params=_cparams(("arbitrary", "arbitrary"), vmem),
        name=name,
    )(*args)


def _group_norm_silu(y, gn_g, gn_b, store):
    for g in range(y.shape[1] // GROUP_W):
        sl = slice(g * GROUP_W, (g + 1) * GROUP_W)
        yg = y[:, sl]
        mu = jnp.mean(yg, axis=-1, keepdims=True)
        d = yg - mu
        var = jnp.mean(d * d, axis=-1, keepdims=True)
        yn = d * lax.rsqrt(var + EPS) * gn_g[:, sl] + gn_b[:, sl]
        store(sl, _silu(yn))


CONV_PAD = 32


def _causal_taps(src_ref, w_ref, t0, tt, first_off, n_taps, init):
    acc = init
    for s in range(SUBLANES):
        part = None
        for k in range(n_taps):
            off = first_off + k
            if off % SUBLANES != s:
                continue
            base = pl.multiple_of(t0 + (off - s), SUBLANES)
            term = src_ref[pl.ds(base, tt + SUBLANES), :] * w_ref[k:k + 1, :]
            part = term if part is None else part + term
        if part is not None:
            acc = acc + part[s:s + tt, :]
    return acc


def _last_seq(b):
    return jnp.minimum(b, BATCH - 1)


def _prompt_or_zero(body, y_ref):
    b = pl.program_id(1)

    @pl.when(b < BATCH)
    def _():
        body()

    @pl.when(b == BATCH)
    def _():
        y_ref[...] = jnp.zeros(y_ref.shape, y_ref.dtype)


def _conv_p_kernel(*refs):
    _prompt_or_zero(functools.partial(_conv_p_body, *refs), refs[6])


def _conv_p_body(cv_ref, cg_ref, w_ref, b_ref, gng_ref, gnb_ref, y_ref, st_ref, upad_ref):
    t_len = cv_ref.shape[0]
    tc = cv_ref.shape[1]
    upad_ref[0:CONV_PAD, :] = jnp.zeros((CONV_PAD, tc), F32)
    upad_ref[CONV_PAD + t_len:CONV_PAD + t_len + SUBLANES, :] = jnp.zeros((SUBLANES, tc), F32)

    def glu(c, carry):
        t0 = pl.multiple_of(c * LRU_TT, LRU_TT)
        u = cv_ref[pl.ds(t0, LRU_TT), :] * _sigmoid(cg_ref[pl.ds(t0, LRU_TT), :])
        upad_ref[pl.ds(CONV_PAD + t0, LRU_TT), :] = u
        return carry

    lax.fori_loop(0, t_len // LRU_TT, glu, 0)
    st_ref[0] = upad_ref[CONV_PAD + t_len - (CONV_K - 1):CONV_PAD + t_len, :]

    bias = b_ref[...]
    gn_g = gng_ref[...]
    gn_b = gnb_ref[...]
    shift = CONV_PAD - (CONV_K - 1)

    def chunk(c, carry):
        t0 = pl.multiple_of(c * CONV_TT, CONV_TT)
        acc = _causal_taps(upad_ref, w_ref, t0, CONV_TT, shift, CONV_K,
                           jnp.zeros((CONV_TT, tc), F32) + bias)

        def store(sl, v):
            y_ref[pl.ds(t0, CONV_TT), sl] = v.astype(y_ref.dtype)

        _group_norm_silu(acc, gn_g, gn_b, store)
        return carry

    lax.fori_loop(0, t_len // CONV_TT, chunk, 0)


def _conv_prompt(z, conv_w, conv_b, gn_g, gn_b):
    tc = CONV_TC
    nct = C_CONV // tc
    return pl.pallas_call(
        _conv_p_kernel,
        grid=(nct, BATCH + 1),
        in_specs=[pl.BlockSpec((SEQ, tc), lambda c, b: (_last_seq(b), c)),
                  pl.BlockSpec((SEQ, tc), lambda c, b: (_last_seq(b), c + nct)),
                  pl.BlockSpec((CONV_K, tc), lambda c, b: (0, c)),
                  pl.BlockSpec((1, tc), lambda c, b: (0, c)),
                  pl.BlockSpec((1, tc), lambda c, b: (0, c)),
                  pl.BlockSpec((1, tc), lambda c, b: (0, c))],
        out_specs=[pl.BlockSpec((SEQ, tc), lambda c, b: (b, c)),
                   pl.BlockSpec((1, CONV_K - 1, tc), lambda c, b: (_last_seq(b), 0, c))],
        out_shape=[jax.ShapeDtypeStruct((N_TOK, C_CONV), BF16),
                   jax.ShapeDtypeStruct((BATCH, CONV_K - 1, C_CONV), F32)],
        scratch_shapes=[pltpu.VMEM((CONV_PAD + SEQ + SUBLANES, tc), F32)],
        compiler_params=_cparams(("arbitrary", "arbitrary"), 8 * SEQ * tc * 4),
        name="conv_prompt",
    )(z, z, conv_w, conv_b.reshape(1, -1), gn_g.reshape(1, -1), gn_b.reshape(1, -1))


def _log_sigmoid(x):
    return -(jnp.maximum(-x, 0.0) + jnp.log(1.0 + jnp.exp(-jnp.abs(x))))


def _lru_gates(xc, wa, ba, wi, bi, log_sig_lam):
    xb = xc.astype(BF16)
    r = _sigmoid(jnp.dot(xb, wa, preferred_element_type=F32) + ba)
    i = _sigmoid(jnp.dot(xb, wi, preferred_element_type=F32) + bi)
    a = jnp.exp(RG_C * r * log_sig_lam)
    return a, jnp.sqrt(1.0 - a * a) * (i * xc)


LRU_PAD = 8


def _lru_p_kernel(*refs):
    _prompt_or_zero(functools.partial(_lru_p_body, *refs), refs[9])


def _lru_p_body(lx_ref, lg_ref, cw_ref, cb_ref, wa_ref, ba_ref, wi_ref, bi_ref, lam_ref,
                y_ref, buf_ref, h_ref, xpad_ref, a_ref, b_ref):
    t_len = lx_ref.shape[0]
    tc = lx_ref.shape[1]
    n_heads = tc // LRU_HD
    xpad_ref[0:LRU_PAD, :] = jnp.zeros((LRU_PAD, tc), F32)
    xpad_ref[LRU_PAD + t_len:LRU_PAD + t_len + SUBLANES, :] = jnp.zeros((SUBLANES, tc), F32)

    def copy_in(c, carry):
        t0 = pl.multiple_of(c * LRU_TT, LRU_TT)
        xpad_ref[pl.ds(LRU_PAD + t0, LRU_TT), :] = lx_ref[pl.ds(t0, LRU_TT), :]
        return carry

    lax.fori_loop(0, t_len // LRU_TT, copy_in, 0)
    buf_ref[0] = lx_ref[t_len - (LRU_CONV_K - 1):t_len, :]

    cb = cb_ref[...]
    log_sig_lam = _log_sigmoid(lam_ref[...])
    shift = LRU_PAD - (LRU_CONV_K - 1)

    def gates(c, carry):
        t0 = pl.multiple_of(c * LRU_TT, LRU_TT)
        xc = _causal_taps(xpad_ref, cw_ref, t0, LRU_TT, shift, LRU_CONV_K,
                          jnp.zeros((LRU_TT, tc), F32) + cb)
        for hh in range(n_heads):
            sl = slice(hh * LRU_HD, (hh + 1) * LRU_HD)
            a, b = _lru_gates(xc[:, sl], wa_ref[hh].astype(BF16), ba_ref[:, sl],
                              wi_ref[hh].astype(BF16), bi_ref[:, sl], log_sig_lam[:, sl])
            a_ref[pl.ds(t0, LRU_TT), sl] = a
            b_ref[pl.ds(t0, LRU_TT), sl] = b
        return carry

    lax.fori_loop(0, t_len // LRU_TT, gates, 0)

    row = lax.broadcasted_iota(I32, (SUBLANES, tc), 0)

    def scan(i, h):
        t0 = pl.multiple_of(i * SUBLANES, SUBLANES)
        a = a_ref[pl.ds(t0, SUBLANES), :]
        b = b_ref[pl.ds(t0, SUBLANES), :]
        for d in (1, 2, 4):
            a_sh = pltpu.roll(a, d, axis=0)
            b_sh = pltpu.roll(b, d, axis=0)
            m = row >= d
            b = jnp.where(m, a * b_sh + b, b)
            a = jnp.where(m, a * a_sh, a)
        hb = a * h + b
        b_ref[pl.ds(t0, SUBLANES), :] = hb
        return hb[SUBLANES - 1:SUBLANES, :]

    h_last = lax.fori_loop(0, t_len // SUBLANES, scan, jnp.zeros((1, tc), F32), unroll=4)
    h_ref[0] = h_last

    def gate_out(c, carry):
        t0 = pl.multiple_of(c * LRU_TT, LRU_TT)
        y = b_ref[pl.ds(t0, LRU_TT), :] * _gelu_tanh(lg_ref[pl.ds(t0, LRU_TT), :])
        y_ref[pl.ds(t0, LRU_TT), :] = y.astype(y_ref.dtype)
        return carry

    lax.fori_loop(0, t_len // LRU_TT, gate_out, 0)


def _lru_prompt(z, cw, cb, wa, ba, wi, bi, lam):
    tc = CONV_TC
    nct = C_LRU // tc
    hpb = tc // LRU_HD
    col0 = 2 * C_CONV // tc
    vec = lambda v: v.reshape(1, -1)
    return pl.pallas_call(
        _lru_p_kernel,
        grid=(nct, BATCH + 1),
        in_specs=[pl.BlockSpec((SEQ, tc), lambda c, b: (_last_seq(b), c + col0)),
                  pl.BlockSpec((SEQ, tc), lambda c, b: (_last_seq(b), c + col0 + nct)),
                  pl.BlockSpec((LRU_CONV_K, tc), lambda c, b: (0, c)),
                  pl.BlockSpec((1, tc), lambda c, b: (0, c)),
                  pl.BlockSpec((hpb, LRU_HD, LRU_HD), lambda c, b: (c, 0, 0)),
                  pl.BlockSpec((1, tc), lambda c, b: (0, c)),
                  pl.BlockSpec((hpb, LRU_HD, LRU_HD), lambda c, b: (c, 0, 0)),
                  pl.BlockSpec((1, tc), lambda c, b: (0, c)),
                  pl.BlockSpec((1, tc), lambda c, b: (0, c))],
        out_specs=[pl.BlockSpec((SEQ, tc), lambda c, b: (b, c)),
                   pl.BlockSpec((1, LRU_CONV_K - 1, tc), lambda c, b: (_last_seq(b), 0, c)),
                   pl.BlockSpec((1, 1, tc), lambda c, b: (_last_seq(b), 0, c))],
        out_shape=[jax.ShapeDtypeStruct((N_TOK, C_LRU), BF16),
                   jax.ShapeDtypeStruct((BATCH, LRU_CONV_K - 1, C_LRU), F32),
                   jax.ShapeDtypeStruct((BATCH, 1, C_LRU), F32)],
        scratch_shapes=[pltpu.VMEM((LRU_PAD + SEQ + SUBLANES, tc), F32),
                        pltpu.VMEM((SEQ, tc), F32),
                        pltpu.VMEM((SEQ, tc), F32)],
        compiler_params=_cparams(("arbitrary", "arbitrary"), 12 * SEQ * tc * 4),
        name="lru_prompt",
    )(z, z, cw, vec(cb), wa, vec(ba), wi, vec(bi), vec(lam))


def _mixer_s_kernel(cv_ref, cg_ref, lx_ref, lg_ref, st_ref, lst_ref, h0_ref,
                    cw_ref, cb_ref, gng_ref, gnb_ref, lcw_ref, lcb_ref,
                    wa_ref, ba_ref, wi_ref, bi_ref, lam_ref, yc_in, yl_in,
                    yc_ref, yl_ref, nst_ref, nlst_ref, nh_ref):
    del yc_in, yl_in
    u = cv_ref[...] * _sigmoid(cg_ref[...])
    acc = u * cw_ref[CONV_K - 1:CONV_K, :] + cb_ref[...]
    for k in range(CONV_K - 1):
        row = st_ref[:, k, :]
        acc = acc + row * cw_ref[k:k + 1, :]
        if k > 0:
            nst_ref[:, k - 1, :] = row
    nst_ref[:, CONV_K - 2, :] = u

    def store_c(sl, v):
        yc_ref[:, sl] = v.astype(yc_ref.dtype)

    _group_norm_silu(acc, gng_ref[...], gnb_ref[...], store_c)

    lx = lx_ref[...]
    xc = lx * lcw_ref[LRU_CONV_K - 1:LRU_CONV_K, :] + lcb_ref[...]
    for k in range(LRU_CONV_K - 1):
        row = lst_ref[:, k, :]
        xc = xc + row * lcw_ref[k:k + 1, :]
        if k > 0:
            nlst_ref[:, k - 1, :] = row
    nlst_ref[:, LRU_CONV_K - 2, :] = lx

    log_sig_lam = _log_sigmoid(lam_ref[...])
    for hh in range(LRU_HEADS):
        sl = slice(hh * LRU_HD, (hh + 1) * LRU_HD)
        a, b = _lru_gates(xc[:, sl], wa_ref[hh].astype(BF16), ba_ref[:, sl],
                          wi_ref[hh].astype(BF16), bi_ref[:, sl], log_sig_lam[:, sl])
        h = a * h0_ref[:, sl] + b
        nh_ref[:, sl] = h
        yl_ref[:, sl] = (h * _gelu_tanh(lg_ref[:, sl])).astype(yl_ref.dtype)


def _mixer_sample(z, st, lst, h0, layer, yc, yl, cw, cb, gn_g, gn_b, lcw, lcb, wa, ba, wi, bi, lam):
    bt = S_BT
    rb0 = N_P // bt
    vec = lambda v: v.reshape(1, -1)
    zspec = lambda col: pl.BlockSpec((bt, C_CONV), lambda i: (i + rb0, col))
    full = lambda shape: pl.BlockSpec(shape, lambda i: (0,) * len(shape))
    any_spec = pl.BlockSpec(memory_space=pl.ANY)
    return pl.pallas_call(
        _mixer_s_kernel,
        grid=(N_S // bt,),
        in_specs=[zspec(0), zspec(1), zspec(2), zspec(3),
                  pl.BlockSpec((None, bt, CONV_K - 1, C_CONV), lambda i: (layer, i, 0, 0)),
                  pl.BlockSpec((None, bt, LRU_CONV_K - 1, C_LRU), lambda i: (layer, i, 0, 0)),
                  pl.BlockSpec((None, bt, C_LRU), lambda i: (layer, i, 0)),
                  full((CONV_K, C_CONV)), full((1, C_CONV)), full((1, C_CONV)), full((1, C_CONV)),
                  full((LRU_CONV_K, C_LRU)), full((1, C_LRU)),
                  full((LRU_HEADS, LRU_HD, LRU_HD)), full((1, C_LRU)),
                  full((LRU_HEADS, LRU_HD, LRU_HD)), full((1, C_LRU)), full((1, C_LRU)),
                  any_spec, any_spec],
        out_specs=[pl.BlockSpec((bt, C_CONV), lambda i: (i + rb0, 0)),
                   pl.BlockSpec((bt, C_LRU), lambda i: (i + rb0, 0)),
                   pl.BlockSpec((bt, CONV_K - 1, C_CONV), lambda i: (i, 0, 0)),
                   pl.BlockSpec((bt, LRU_CONV_K - 1, C_LRU), lambda i: (i, 0, 0)),
                   pl.BlockSpec((bt, C_LRU), lambda i: (i, 0))],
        out_shape=[jax.ShapeDtypeStruct(yc.shape, yc.dtype),
                   jax.ShapeDtypeStruct(yl.shape, yl.dtype),
                   jax.ShapeDtypeStruct(st.shape[1:], F32),
                   jax.ShapeDtypeStruct(lst.shape[1:], F32),
                   jax.ShapeDtypeStruct(h0.shape[1:], F32)],
        input_output_aliases={18: 0, 19: 1},
        compiler_params=_cparams(("arbitrary",), 48 << 20),
        name="mixer_sample",
    )(z, z, z, z, st, lst, h0, cw, vec(cb), vec(gn_g), vec(gn_b), lcw, vec(lcb),
      wa, vec(ba), wi, vec(bi), vec(lam), yc, yl)


def _attn_p_kernel(q_ref, k_ref, v_ref, o_ref):
    i = pl.program_id(0)

    @pl.when(i < N_P // ATT_TQ)
    def _():
        _attn_p_body(q_ref, k_ref, v_ref, o_ref)

    @pl.when(i == N_P // ATT_TQ)
    def _():
        o_ref[...] = jnp.zeros(o_ref.shape, o_ref.dtype)


def _attn_p_body(q_ref, k_ref, v_ref, o_ref):
    scale = MEM_HD ** -0.5
    for h in range(MEM_HEADS):
        sl = slice(h * MEM_HD, (h + 1) * MEM_HD)
        q = q_ref[:, sl].astype(BF16)
        k = k_ref[:, sl].astype(BF16)
        v = v_ref[:, sl].astype(BF16)
        s = lax.dot_general(q, k, (((1,), (1,)), ((), ())), preferred_element_type=F32) * scale
        p = jnp.exp(s - jnp.max(s, axis=-1, keepdims=True))
        pr = p / jnp.sum(p, axis=-1, keepdims=True)
        o = jnp.dot(pr.astype(BF16), v, preferred_element_type=F32)
        o_ref[:, sl] = o.astype(o_ref.dtype)


def _attn_prompt(q, k, v):
    nq = SEQ // ATT_TQ
    return pl.pallas_call(
        _attn_p_kernel,
        grid=(BATCH * nq + 1,),
        in_specs=[pl.BlockSpec((ATT_TQ, MEM_W), lambda i: (i, 0)),
                  pl.BlockSpec((MEM_LEN, MEM_W), lambda i: (_last_seq(i // nq), 0)),
                  pl.BlockSpec((MEM_LEN, MEM_W), lambda i: (_last_seq(i // nq), 0))],
        out_specs=pl.BlockSpec((ATT_TQ, MEM_W), lambda i: (i, 0)),
        out_shape=jax.ShapeDtypeStruct((N_TOK, MEM_W), BF16),
        compiler_params=_cparams(("arbitrary",), 32 << 20),
        name="attn_prompt",
    )(q, k, v)


def _attn_s_kernel(q_ref, k_ref, v_ref, o_in, o_ref, stage_ref):
    del o_in
    jj = pl.program_id(1)
    scale = MEM_HD ** -0.5
    for bb in range(ATT_S_BT):
        r = jj * ATT_S_BT + bb
        q = q_ref[pl.ds(r, 1)]
        s = jnp.sum(k_ref[bb] * q, axis=-1, keepdims=True) * scale
        p = jnp.exp(s - jnp.max(s, axis=0, keepdims=True))
        l = jnp.sum(p, axis=0)
        o = jnp.sum(p * v_ref[bb], axis=0) / l
        for h in range(MEM_HEADS):
            stage_ref[pl.ds(r, 1), h * MEM_HD:(h + 1) * MEM_HD] = o[h:h + 1, :]

    @pl.when(jj == pl.num_programs(1) - 1)
    def _():
        o_ref[...] = stage_ref[...].astype(o_ref.dtype)


def _attn_sample(q4, kc, vc, layer, o):
    bt = S_BT
    inner = bt // ATT_S_BT
    rb0 = N_P // bt
    cache_spec = pl.BlockSpec((None, ATT_S_BT, MEM_LEN, MEM_HEADS, MEM_HD),
                              lambda i, j: (layer, i * inner + j, 0, 0, 0))
    return pl.pallas_call(
        _attn_s_kernel,
        grid=(N_S // bt, inner),
        in_specs=[pl.BlockSpec((bt, MEM_HEADS, MEM_HD), lambda i, j: (i, 0, 0)),
                  cache_spec, cache_spec,
                  pl.BlockSpec(memory_space=pl.ANY)],
        out_specs=pl.BlockSpec((bt, MEM_W), lambda i, j: (i + rb0, 0)),
        out_shape=jax.ShapeDtypeStruct(o.shape, o.dtype),
        scratch_shapes=[pltpu.VMEM((bt, MEM_W), F32)],
        input_output_aliases={3: 0},
        compiler_params=_cparams(("arbitrary", "arbitrary"), 48 << 20),
        name="attn_sample",
    )(q4, kc, vc, o)


def _norm_router_kernel(x_ref, g_ref, wr_ref, br_ref, xp_ref, ids_ref, gates_ref):
    xn = _rms(x_ref[...], g_ref[...])
    xp_ref[...] = pltpu.pack_elementwise([xn[:, :HALF], xn[:, HALF:]], packed_dtype=BF16)
    logits = jnp.dot(xn.astype(BF16), wr_ref[...], preferred_element_type=F32) + br_ref[...]
    lane = lax.broadcasted_iota(I32, logits.shape, 1)
    neg = jnp.float32(-jnp.inf)

    def first_max(vals):
        m = jnp.max(vals, axis=-1, keepdims=True)
        idx = jnp.min(jnp.where(vals == m, lane, ROUTER_W), axis=-1, keepdims=True)
        return m, idx

    is_group = lane < N_GROUPS
    g_max, g_sel = first_max(jnp.where(is_group, logits, neg))
    p_g = 1.0 / jnp.sum(jnp.where(is_group, jnp.exp(logits - g_max), 0.0), axis=-1, keepdims=True)
    lo = N_GROUPS + EXPERTS_PER_GROUP * g_sel
    in_group = (lane >= lo) & (lane < lo + EXPERTS_PER_GROUP)
    e_logits = jnp.where(in_group, logits, neg)
    t1, i1 = first_max(e_logits)
    t2, i2 = first_max(jnp.where(lane == i1, neg, e_logits))
    e = jnp.exp(t2 - t1)
    w1 = p_g / (1.0 + e)
    w2 = p_g * e / (1.0 + e)
    ids_ref[...] = jnp.where(lane == 0, i1 - N_GROUPS, jnp.where(lane == 1, i2 - N_GROUPS, 0))
    gates_ref[...] = jnp.where(lane == 0, w1, jnp.where(lane == 1, w2, 0.0))


def _norm_router(x, g, wr, br):
    tm = TM_NORM
    return pl.pallas_call(
        _norm_router_kernel,
        grid=(N_TOK // tm,),
        in_specs=[pl.BlockSpec((tm, D_MODEL), lambda i: (i, 0)),
                  pl.BlockSpec((1, D_MODEL), lambda i: (0, 0)),
                  pl.BlockSpec((D_MODEL, ROUTER_W), lambda i: (0, 0)),
                  pl.BlockSpec((1, ROUTER_W), lambda i: (0, 0))],
        out_specs=[pl.BlockSpec((tm, HALF), lambda i: (i, 0)),
                   pl.BlockSpec((tm, ROUTER_W), lambda i: (i, 0)),
                   pl.BlockSpec((tm, ROUTER_W), lambda i: (i, 0))],
        out_shape=[jax.ShapeDtypeStruct((N_TOK, HALF), jnp.uint32),
                   jax.ShapeDtypeStruct((N_TOK, ROUTER_W), I32),
                   jax.ShapeDtypeStruct((N_TOK, ROUTER_W), F32)],
        compiler_params=_cparams(("arbitrary",), 8 * tm * D_MODEL * 4),
        name="norm_router",
    )(x, g.reshape(1, -1), wr, br)


def _unpack_x(xp):
    lo = pltpu.unpack_elementwise(xp, index=0, packed_dtype=BF16, unpacked_dtype=F32)
    hi = pltpu.unpack_elementwise(xp, index=1, packed_dtype=BF16, unpacked_dtype=F32)
    return lo.astype(BF16), hi.astype(BF16)


def _dot(a, w):
    return lax.dot_general(a, w, (((1,), (0,)), ((), ())), preferred_element_type=F32)


def _ffn_kernel(sbe_ref, sbb_ref, sbn_ref, sbz_ref, tok_ref, xp_hbm, wg_ref, wu_ref, wd_ref, ys_hbm,
                xbuf_ref, acc_ref, xsem, ysem, pend_ref):
    del sbe_ref
    sb = pl.program_id(0)
    j = pl.program_id(1)
    nj = pl.num_programs(1)
    slot = sb % 2

    @pl.when((sb == 0) & (j == 0))
    def _():
        pend_ref[0] = 0

    def row_copy(s, slot_, r):
        tok = tok_ref[sbb_ref[s] * MOE_BLOCK + r]
        return pltpu.make_async_copy(xp_hbm.at[pl.ds(tok, 1), :],
                                     xbuf_ref.at[slot_, pl.ds(r, 1), :], xsem.at[slot_])

    def start_blocks(s, slot_, lo, hi):
        def block(i, carry):
            def body(r, c):
                row_copy(s, slot_, i * MOE_BLOCK + r).start(priority=GATHER_PRIORITY)
                return c

            return lax.fori_loop(0, MOE_BLOCK, body, carry, unroll=ISSUE_UNROLL)

        lax.fori_loop(lo, jnp.minimum(hi, sbn_ref[s]), block, 0)

    def wait_rows(s, slot_):
        def block(i, carry):
            r0 = pl.multiple_of(i * MOE_BLOCK, MOE_BLOCK)
            pltpu.make_async_copy(xp_hbm.at[pl.ds(0, MOE_BLOCK), :],
                                  xbuf_ref.at[slot_, pl.ds(r0, MOE_BLOCK), :], xsem.at[slot_]).wait()
            return carry

        lax.fori_loop(0, sbn_ref[s], block, 0)

    def out_copy(i, blk):
        r0 = pl.multiple_of(i * MOE_BLOCK, MOE_BLOCK)
        r1 = pl.multiple_of(blk * MOE_BLOCK, MOE_BLOCK)
        return pltpu.make_async_copy(acc_ref.at[pl.ds(r0, MOE_BLOCK), :],
                                     ys_hbm.at[pl.ds(r1, MOE_BLOCK), :], ysem)

    def wait_out():
        def body(i, carry):
            out_copy(0, 0).wait()
            return carry

        lax.fori_loop(0, pend_ref[0], body, 0)
        pend_ref[0] = 0

    per_j = -(-NSUB // (D_EXPERT // TF))

    @pl.when((sb == 0) & (j == 0))
    def _():
        start_blocks(sb, slot, 0, NSUB)

    @pl.when(sb + 1 < pl.num_programs(0))
    def _():
        start_blocks(sb + 1, 1 - slot, j * per_j, (j + 1) * per_j)

    nsub = sbn_ref[sb]
    nzero = sbz_ref[sb]

    def zero_block(i, carry):
        r0 = pl.multiple_of(i * MOE_BLOCK, MOE_BLOCK)
        acc_ref[pl.ds(r0, MOE_BLOCK), :] = jnp.zeros((MOE_BLOCK, D_MODEL), F32)
        return carry

    def start_out(i, carry):
        out_copy(i, sbb_ref[sb] + i).start()
        return carry

    @pl.when(j == 0)
    def _():
        wait_rows(sb, slot)
        wait_out()
        lax.fori_loop(0, jnp.maximum(nsub, nzero), zero_block, 0)
        lax.fori_loop(0, nzero, start_out, 0)
        pend_ref[0] = nzero

    def sub(i, carry):
        r0 = pl.multiple_of(i * MOE_BLOCK, MOE_BLOCK)
        lo, hi = _unpack_x(xbuf_ref[slot, pl.ds(r0, MOE_BLOCK), :])
        g = _dot(lo, wg_ref[:HALF, :]) + _dot(hi, wg_ref[HALF:, :])
        u = _dot(lo, wu_ref[:HALF, :]) + _dot(hi, wu_ref[HALF:, :])
        h = (_silu(g) * u).astype(BF16)
        acc_ref[pl.ds(r0, MOE_BLOCK), :] += _dot(h, wd_ref[...])
        return carry

    lax.fori_loop(0, nsub, sub, 0)

    @pl.when((j == nj - 1) & (nsub > 0))
    def _():
        lax.fori_loop(0, nsub, start_out, 0)
        pend_ref[0] = nsub

    @pl.when((sb == pl.num_programs(0) - 1) & (j == nj - 1))
    def _():
        wait_out()


def _moe_ffn(xp, plan, w_gate, w_up, w_down, layer):
    nj = D_EXPERT // TF

    def hidden_slice(sb, j, sbn):
        return jnp.where(sbn[sb] > 0, j, nj - 1)

    up_spec = pl.BlockSpec((None, None, D_MODEL, TF),
                           lambda sb, j, sbe, sbb, sbn, sbz, tok: (layer, sbe[sb], 0, hidden_slice(sb, j, sbn)))
    down_spec = pl.BlockSpec((None, None, TF, D_MODEL),
                             lambda sb, j, sbe, sbb, sbn, sbz, tok: (layer, sbe[sb], hidden_slice(sb, j, sbn), 0))
    return pl.pallas_call(
        _ffn_kernel,
        grid_spec=pltpu.PrefetchScalarGridSpec(
            num_scalar_prefetch=5,
            grid=(N_SB, nj),
            in_specs=[pl.BlockSpec(memory_space=pl.ANY), up_spec, up_spec, down_spec],
            out_specs=pl.BlockSpec(memory_space=pl.ANY),
            scratch_shapes=[pltpu.VMEM((2, SB_ROWS, HALF), jnp.uint32),
                            pltpu.VMEM((SB_ROWS, D_MODEL), F32),
                            pltpu.SemaphoreType.DMA((2,)),
                            pltpu.SemaphoreType.DMA(()),
                            pltpu.SMEM((1,), I32)]),
        out_shape=jax.ShapeDtypeStruct((N_ROWS, D_MODEL), F32),
        compiler_params=_cparams(("arbitrary", "arbitrary"), VMEM_CAP),
        name="moe_ffn",
    )(plan["sb_e"], plan["sb_blk"], plan["sb_n"], plan["sb_z"], plan["row_tok"], xp, w_gate, w_up, w_down)


def _combine_kernel(dest_ref, x_ref, gates_ref, g_ref, ys_hbm, out_a, out_b, buf_ref, sem, *, final):
    i = pl.program_id(0)
    n = pl.num_programs(0)

    def row_copy(step, slot, t, k):
        s = (step * COMB_TM + t) * TOP_K + k
        return pltpu.make_async_copy(ys_hbm.at[pl.ds(dest_ref[s], 1), :],
                                     buf_ref.at[slot, k, pl.ds(t, 1), :], sem.at[slot])

    def start_all(step, slot):
        def body(t, carry):
            for k in range(TOP_K):
                row_copy(step, slot, t, k).start(priority=k % 2)
            return carry

        lax.fori_loop(0, COMB_TM, body, 0, unroll=ISSUE_UNROLL)

    def wait_all(step, slot):
        for k in range(TOP_K):
            pltpu.make_async_copy(ys_hbm.at[pl.ds(0, COMB_TM), :], buf_ref.at[slot, k], sem.at[slot]).wait()

    slot = i % 2

    @pl.when(i == 0)
    def _():
        start_all(i, slot)

    @pl.when(i + 1 < n)
    def _():
        start_all(i + 1, 1 - slot)

    wait_all(i, slot)
    gates = gates_ref[...]
    y = x_ref[...] + gates[:, 0:1] * buf_ref[slot, 0] + gates[:, 1:2] * buf_ref[slot, 1]
    yn = _rms(y, g_ref[...])
    if final:
        @pl.when(i < n - 1)
        def _():
            out_a[...] = yn

        @pl.when(i == n - 1)
        def _():
            out_b[...] = yn
    else:
        out_a[...] = y
        out_b[...] = yn.astype(out_b.dtype)


def _combine(x, ys, dest, gates, g_next, *, final):
    tm = COMB_TM
    assert N_S == tm
    n_p_blocks = N_P // tm
    if final:
        out_specs = [pl.BlockSpec((tm, D_MODEL), lambda i, d: (jnp.minimum(i, n_p_blocks - 1), 0)),
                     pl.BlockSpec((tm, D_MODEL), lambda i, d: (0, 0))]
        out_shape = [jax.ShapeDtypeStruct((N_P, D_MODEL), F32),
                     jax.ShapeDtypeStruct((N_S, D_MODEL), F32)]
    else:
        out_specs = [pl.BlockSpec((tm, D_MODEL), lambda i, d: (i, 0)),
                     pl.BlockSpec((tm, D_MODEL), lambda i, d: (i, 0))]
        out_shape = [jax.ShapeDtypeStruct((N_TOK, D_MODEL), F32),
                     jax.ShapeDtypeStruct((N_TOK, D_MODEL), BF16)]
    return pl.pallas_call(
        functools.partial(_combine_kernel, final=final),
        grid_spec=pltpu.PrefetchScalarGridSpec(
            num_scalar_prefetch=1,
            grid=(N_TOK // tm,),
            in_specs=[pl.BlockSpec((tm, D_MODEL), lambda i, d: (i, 0)),
                      pl.BlockSpec((tm, ROUTER_W), lambda i, d: (i, 0)),
                      pl.BlockSpec((1, D_MODEL), lambda i, d: (0, 0)),
                      pl.BlockSpec(memory_space=pl.ANY)],
            out_specs=out_specs,
            scratch_shapes=[pltpu.VMEM((2, TOP_K, tm, D_MODEL), F32),
                            pltpu.SemaphoreType.DMA((2,))]),
        out_shape=out_shape,
        compiler_params=_cparams(("arbitrary",), 40 << 20),
        name="moe_combine",
    )(dest, x, gates, g_next.reshape(1, -1), ys)


def _dispatch_plan(ids):
    flat_e = ids[:, :TOP_K].reshape(-1)
    onehot = (flat_e[:, None] == jnp.arange(N_EXPERTS, dtype=I32)[None, :]).astype(I32)
    csum = jnp.cumsum(onehot, axis=0)
    rank = jnp.sum(onehot * csum, axis=1) - 1
    counts = csum[-1]
    nblk = (counts + MOE_BLOCK - 1) // MOE_BLOCK
    bend = jnp.cumsum(nblk)
    bstart = bend - nblk
    dest = (jnp.sum(onehot * bstart[None, :], axis=1) * MOE_BLOCK + rank).astype(I32)
    row_tok = jnp.zeros((N_ROWS,), I32).at[dest].set(jnp.arange(N_SLOT, dtype=I32) // TOP_K)

    nsup = (nblk + NSUB - 1) // NSUB
    sup_end = jnp.cumsum(nsup)
    sup_start = sup_end - nsup
    total_sup = sup_end[-1]
    sb = jnp.arange(N_SB, dtype=I32)
    e_of = jnp.minimum(jnp.sum((sup_end[None, :] <= sb[:, None]).astype(I32), axis=1), N_EXPERTS - 1)
    q = sb - sup_start[e_of]
    real = sb < total_sup
    n_real = jnp.clip(nblk[e_of] - NSUB * q, 0, NSUB)
    blk_real = bstart[e_of] + NSUB * q
    blk_tail = bend[-1] + NSUB * (sb - total_sup)
    n_tail = jnp.clip(N_BLOCKS - blk_tail, 0, NSUB)
    e_last = e_of[total_sup - 1]
    return dict(
        dest=dest, row_tok=row_tok,
        sb_e=jnp.where(real, e_of, e_last).astype(I32),
        sb_blk=jnp.where(real, blk_real, jnp.minimum(blk_tail, N_BLOCKS - 1)).astype(I32),
        sb_n=jnp.where(real, n_real, 0).astype(I32),
        sb_z=jnp.where(real, 0, n_tail).astype(I32))


def kernel(x_prompt, x_sample, mem_prompt, state_conv, state_lru_conv, state_lru_h, cache_mem_k, cache_mem_v, norm_mix, w_in, conv_w, conv_b, conv_gn_g, conv_gn_b, lru_conv_w, lru_conv_b, lru_wa, lru_ba, lru_wi, lru_bi, lru_lambda, w_out, norm_attn, norm_mem_kv, w_q, w_k, w_v, w_o, norm_ffn, w_router_g, b_router_g, w_router_e, b_router_e, w_gate, w_up, w_down, norm_final):
    x = jnp.concatenate([x_prompt.reshape(N_P, D_MODEL), x_sample.reshape(N_S, D_MODEL)], axis=0)
    mem = mem_prompt.reshape(BATCH * MEM_LEN, D_MODEL)
    xn = _norm(x, norm_mix[0], tm=TM_NORM)

    conv_p, lruc_p, h_p, mk_p, mv_p, conv_s, lruc_s, h_s = ([] for _ in range(8))
    for l in range(DEPTH):
        z = _mm([xn], w_in, l, tn=1024, tm=TM, name="w_in")
        yc, cst = _conv_prompt(z, conv_w[l], conv_b[l], conv_gn_g[l], conv_gn_b[l])
        yl, lst, hl = _lru_prompt(z, lru_conv_w[l], lru_conv_b[l], lru_wa[l], lru_ba[l],
                                  lru_wi[l], lru_bi[l], lru_lambda[l])
        yc, yl, cst_s, lst_s, hl_s = _mixer_sample(
            z, state_conv, state_lru_conv, state_lru_h, l, yc, yl,
            conv_w[l], conv_b[l], conv_gn_g[l], conv_gn_b[l], lru_conv_w[l], lru_conv_b[l],
            lru_wa[l], lru_ba[l], lru_wi[l], lru_bi[l], lru_lambda[l])
        x = _mm([yc, yl], w_out, l, tn=1024, tm=TM_NORM, res=x, name="w_out")
        conv_p.append(cst); lruc_p.append(lst); h_p.append(hl.reshape(BATCH, C_LRU))
        conv_s.append(cst_s); lruc_s.append(lst_s); h_s.append(hl_s)

        xn = _norm(x, norm_attn[l], tm=TM_NORM)
        q = _mm([xn], w_q, l, tn=1024, tm=TM, name="w_q")
        mn = _norm(mem, norm_mem_kv[l], tm=256)
        k_p = _mm([mn], w_k, l, tn=512, tm=BATCH * MEM_LEN, name="w_k")
        v_p = _mm([mn], w_v, l, tn=512, tm=BATCH * MEM_LEN, name="w_v")
        o = _attn_prompt(q, k_p, v_p)
        o = _attn_sample(q[N_P:].reshape(N_S, MEM_HEADS, MEM_HD), cache_mem_k, cache_mem_v, l, o)
        x = _mm([o], w_o, l, tn=1024, tm=TM, res=x, name="w_o")
        mk_p.append(k_p.reshape(BATCH, MEM_LEN, MEM_HEADS, MEM_HD))
        mv_p.append(v_p.reshape(BATCH, MEM_LEN, MEM_HEADS, MEM_HD))

        wr = jnp.concatenate([w_router_g[l], w_router_e[l].reshape(D_MODEL, N_EXPERTS),
                              jnp.zeros((D_MODEL, ROUTER_W - N_GROUPS - N_EXPERTS), F32)],
                             axis=1).astype(BF16)
        br = jnp.concatenate([b_router_g[l], b_router_e[l].reshape(N_EXPERTS),
                              jnp.zeros((ROUTER_W - N_GROUPS - N_EXPERTS,), F32)]).reshape(1, ROUTER_W)
        xp, ids, gates = _norm_router(x, norm_ffn[l], wr, br)
        plan = _dispatch_plan(ids)
        ys = _moe_ffn(xp, plan, w_gate, w_up, w_down, l)
        if l < DEPTH - 1:
            x, xn = _combine(x, ys, plan["dest"], gates, norm_mix[l + 1], final=False)
        else:
            y_p, y_s = _combine(x, ys, plan["dest"], gates, norm_final, final=True)

    y_prompt = y_p.reshape(BATCH, SEQ, D_MODEL)
    y_sample = y_s.reshape(DEC_BATCH, 1, D_MODEL)
    return (y_prompt, y_sample, jnp.stack(conv_p), jnp.stack(lruc_p), jnp.stack(h_p),
            jnp.stack(mk_p), jnp.stack(mv_p), jnp.stack(conv_s), jnp.stack(lruc_s), jnp.stack(h_s))
```

```python
import functools
import math

import jax
import jax.numpy as jnp
from jax import lax
from jax.experimental import pallas as pl
from jax.experimental.pallas import tpu as pltpu

F32 = jnp.float32
BF16 = jnp.bfloat16
I32 = jnp.int32

D_MODEL = 4096
BATCH = 4
SEQ = 2048
DEPTH = 2
DEC_BATCH = 128
C_CONV = D_MODEL // 2
C_LRU = D_MODEL // 2
D_IN = 2 * C_CONV + 2 * C_LRU
CONV_GROUPS = 16
GROUP_W = C_CONV // CONV_GROUPS
CONV_K = 31
LRU_HEADS = 16
LRU_HD = C_LRU // LRU_HEADS
LRU_CONV_K = 4
RG_C = 8.0
MEM_LEN = 256
MEM_HEADS = 4
MEM_HD = D_MODEL // 16
MEM_W = MEM_HEADS * MEM_HD
N_GROUPS = 4
EXPERTS_PER_GROUP = 8
N_EXPERTS = N_GROUPS * EXPERTS_PER_GROUP
TOP_K = 2
D_EXPERT = D_MODEL // 4
EPS = 1e-6

N_P = BATCH * SEQ
N_S = DEC_BATCH
N_TOK = N_P + N_S
N_SLOT = N_TOK * TOP_K

LANES = 128
SUBLANES = 8
VMEM_CAP = 56 * 1024 * 1024

TM = 1040
TM_NORM = 416
MOE_BLOCK = 128
N_BLOCKS = -(-(N_SLOT + N_EXPERTS * (MOE_BLOCK - 1)) // MOE_BLOCK)
N_ROWS = N_BLOCKS * MOE_BLOCK
TF = 256
ROUTER_W = LANES
HALF = D_MODEL // 2
CONV_TT = 64
CONV_TC = 256
LRU_TT = 256
COMB_TM = 128
PLAN_CHUNK = 128
ATT_TQ = 512
S_BT = 16
ATT_S_BT = 4
ISSUE_UNROLL = 8
GATHER_PRIORITY = 1
NSUB = 6
SB_ROWS = NSUB * MOE_BLOCK
N_SB = (N_BLOCKS + (NSUB - 1) * (N_EXPERTS + 1)) // NSUB + 1


def _cparams(sem, vmem_bytes):
    limit = min(VMEM_CAP, max(32 * 1024 * 1024, int(vmem_bytes)))
    return pltpu.CompilerParams(dimension_semantics=sem, vmem_limit_bytes=limit)


def _sigmoid(x):
    return 1.0 / (1.0 + jnp.exp(-x))


def _silu(x):
    return x * _sigmoid(x)


def _gelu_tanh(x):
    c = math.sqrt(2.0 / math.pi)
    return 0.5 * x * (1.0 + jnp.tanh(c * (x + 0.044715 * (x * x * x))))


def _rms(x, g):
    ms = jnp.mean(x * x, axis=-1, keepdims=True)
    return x * lax.rsqrt(ms + EPS) * g


def _norm_kernel(x_ref, g_ref, o_ref):
    o_ref[...] = _rms(x_ref[...], g_ref[...]).astype(o_ref.dtype)


def _norm(x, g, *, tm, out_dtype=BF16, row_block0=0, n_rows=None):
    n_rows = x.shape[0] if n_rows is None else n_rows
    d = x.shape[1]
    return pl.pallas_call(
        _norm_kernel,
        grid=(n_rows // tm,),
        in_specs=[pl.BlockSpec((tm, d), lambda i: (i + row_block0, 0)),
                  pl.BlockSpec((1, d), lambda i: (0, 0))],
        out_specs=pl.BlockSpec((tm, d), lambda i: (i, 0)),
        out_shape=jax.ShapeDtypeStruct((n_rows, d), out_dtype),
        compiler_params=_cparams(("arbitrary",), 6 * tm * d * 4),
        name="rmsnorm",
    )(x, g.reshape(1, d))


def _mm_kernel(*refs, n_a, has_res):
    a_refs = refs[:n_a]
    w_ref = refs[n_a]
    res_ref = refs[n_a + 1] if has_res else None
    o_ref = refs[n_a + 1 + int(has_res)]
    acc = None
    off = 0
    for a_ref in a_refs:
        k = a_ref.shape[1]
        part = lax.dot_general(a_ref[...].astype(BF16), w_ref[off:off + k, :], (((1,), (0,)), ((), ())),
                               preferred_element_type=F32)
        acc = part if acc is None else acc + part
        off += k
    if has_res:
        acc = acc + res_ref[...]
    o_ref[...] = acc.astype(o_ref.dtype)


def _mm(a_list, w, layer, *, tn, tm, res=None, out_dtype=F32, name="proj"):
    m = a_list[0].shape[0]
    _, k, n = w.shape
    assert sum(a.shape[1] for a in a_list) == k
    in_specs = [pl.BlockSpec((tm, a.shape[1]), lambda j, i: (i, 0)) for a in a_list]
    in_specs.append(pl.BlockSpec((None, k, tn), lambda j, i: (layer, 0, j), pipeline_mode=pl.Buffered(1)))
    args = list(a_list) + [w]
    if res is not None:
        in_specs.append(pl.BlockSpec((tm, tn), lambda j, i: (i, j)))
        args.append(res)
    a_bytes = sum(a.dtype.itemsize * a.shape[1] for a in a_list) * tm
    vmem = 2 * a_bytes + k * tn * 4 + 6 * tm * tn * 4 + (4 << 20)
    return pl.pallas_call(
        functools.partial(_mm_kernel, n_a=len(a_list), has_res=res is not None),
        grid=(n // tn, m // tm),
        in_specs=in_specs,
        out_specs=pl.BlockSpec((tm, tn), lambda j, i: (i, j)),
        out_shape=jax.ShapeDtypeStruct((m, n), out_dtype),
        compiler_params=_cparams(("arbitrary", "arbitrary"), vmem),
        name=name,
    )(*args)


def _group_norm_silu(y, gn_g, gn_b, store):
    for g in range(y.shape[1] // GROUP_W):
        sl = slice(g * GROUP_W, (g + 1) * GROUP_W)
        yg = y[:, sl]
        mu = jnp.mean(yg, axis=-1, keepdims=True)
        d = yg - mu
        var = jnp.mean(d * d, axis=-1, keepdims=True)
        yn = d * lax.rsqrt(var + EPS) * gn_g[:, sl] + gn_b[:, sl]
        store(sl, _silu(yn))


CONV_PAD = 32


def _causal_taps(src_ref, w_ref, t0, tt, first_off, n_taps, init):
    acc = init
    for s in range(SUBLANES):
        part = None
        for k in range(n_taps):
            off = first_off + k
            if off % SUBLANES != s:
                continue
            base = pl.multiple_of(t0 + (off - s), SUBLANES)
            term = src_ref[pl.ds(base, tt + SUBLANES), :] * w_ref[k:k + 1, :]
            part = term if part is None else part + term
        if part is not None:
            acc = acc + part[s:s + tt, :]
    return acc


def _last_seq(b):
    return jnp.minimum(b, BATCH - 1)


def _prompt_or_zero(body, y_ref):
    b = pl.program_id(1)

    @pl.when(b < BATCH)
    def _():
        body()

    @pl.when(b == BATCH)
    def _():
        y_ref[...] = jnp.zeros(y_ref.shape, y_ref.dtype)


def _conv_p_kernel(*refs):
    _prompt_or_zero(functools.partial(_conv_p_body, *refs), refs[6])


def _conv_p_body(cv_ref, cg_ref, w_ref, b_ref, gng_ref, gnb_ref, y_ref, st_ref, upad_ref):
    t_len = cv_ref.shape[0]
    tc = cv_ref.shape[1]
    upad_ref[0:CONV_PAD, :] = jnp.zeros((CONV_PAD, tc), F32)
    upad_ref[CONV_PAD + t_len:CONV_PAD + t_len + SUBLANES, :] = jnp.zeros((SUBLANES, tc), F32)

    def glu(c, carry):
        t0 = pl.multiple_of(c * LRU_TT, LRU_TT)
        u = cv_ref[pl.ds(t0, LRU_TT), :] * _sigmoid(cg_ref[pl.ds(t0, LRU_TT), :])
        upad_ref[pl.ds(CONV_PAD + t0, LRU_TT), :] = u
        return carry

    lax.fori_loop(0, t_len // LRU_TT, glu, 0)
    st_ref[0] = upad_ref[CONV_PAD + t_len - (CONV_K - 1):CONV_PAD + t_len, :]

    bias = b_ref[...]
    gn_g = gng_ref[...]
    gn_b = gnb_ref[...]
    shift = CONV_PAD - (CONV_K - 1)

    def chunk(c, carry):
        t0 = pl.multiple_of(c * CONV_TT, CONV_TT)
        acc = _causal_taps(upad_ref, w_ref, t0, CONV_TT, shift, CONV_K,
                           jnp.zeros((CONV_TT, tc), F32) + bias)

        def store(sl, v):
            y_ref[pl.ds(t0, CONV_TT), sl] = v.astype(y_ref.dtype)

        _group_norm_silu(acc, gn_g, gn_b, store)
        return carry

    lax.fori_loop(0, t_len // CONV_TT, chunk, 0)


def _conv_prompt(z, conv_w, conv_b, gn_g, gn_b):
    tc = CONV_TC
    nct = C_CONV // tc
    return pl.pallas_call(
        _conv_p_kernel,
        grid=(nct, BATCH + 1),
        in_specs=[pl.BlockSpec((SEQ, tc), lambda c, b: (_last_seq(b), c)),
                  pl.BlockSpec((SEQ, tc), lambda c, b: (_last_seq(b), c + nct)),
                  pl.BlockSpec((CONV_K, tc), lambda c, b: (0, c)),
                  pl.BlockSpec((1, tc), lambda c, b: (0, c)),
                  pl.BlockSpec((1, tc), lambda c, b: (0, c)),
                  pl.BlockSpec((1, tc), lambda c, b: (0, c))],
        out_specs=[pl.BlockSpec((SEQ, tc), lambda c, b: (b, c)),
                   pl.BlockSpec((1, CONV_K - 1, tc), lambda c, b: (_last_seq(b), 0, c))],
        out_shape=[jax.ShapeDtypeStruct((N_TOK, C_CONV), BF16),
                   jax.ShapeDtypeStruct((BATCH, CONV_K - 1, C_CONV), F32)],
        scratch_shapes=[pltpu.VMEM((CONV_PAD + SEQ + SUBLANES, tc), F32)],
        compiler_params=_cparams(("arbitrary", "arbitrary"), 8 * SEQ * tc * 4),
        name="conv_prompt",
    )(z, z, conv_w, conv_b.reshape(1, -1), gn_g.reshape(1, -1), gn_b.reshape(1, -1))


def _log_sigmoid(x):
    return -(jnp.maximum(-x, 0.0) + jnp.log(1.0 + jnp.exp(-jnp.abs(x))))


def _lru_gates(xc, wa, ba, wi, bi, log_sig_lam):
    xb = xc.astype(BF16)
    r = _sigmoid(jnp.dot(xb, wa, preferred_element_type=F32) + ba)
    i = _sigmoid(jnp.dot(xb, wi, preferred_element_type=F32) + bi)
    a = jnp.exp(RG_C * r * log_sig_lam)
    return a, jnp.sqrt(1.0 - a * a) * (i * xc)


LRU_PAD = 8


def _lru_p_kernel(*refs):
    _prompt_or_zero(functools.partial(_lru_p_body, *refs), refs[9])


def _lru_p_body(lx_ref, lg_ref, cw_ref, cb_ref, wa_ref, ba_ref, wi_ref, bi_ref, lam_ref,
                y_ref, buf_ref, h_ref, xpad_ref, a_ref, b_ref):
    t_len = lx_ref.shape[0]
    tc = lx_ref.shape[1]
    n_heads = tc // LRU_HD
    xpad_ref[0:LRU_PAD, :] = jnp.zeros((LRU_PAD, tc), F32)
    xpad_ref[LRU_PAD + t_len:LRU_PAD + t_len + SUBLANES, :] = jnp.zeros((SUBLANES, tc), F32)

    def copy_in(c, carry):
        t0 = pl.multiple_of(c * LRU_TT, LRU_TT)
        xpad_ref[pl.ds(LRU_PAD + t0, LRU_TT), :] = lx_ref[pl.ds(t0, LRU_TT), :]
        return carry

    lax.fori_loop(0, t_len // LRU_TT, copy_in, 0)
    buf_ref[0] = lx_ref[t_len - (LRU_CONV_K - 1):t_len, :]

    cb = cb_ref[...]
    log_sig_lam = _log_sigmoid(lam_ref[...])
    shift = LRU_PAD - (LRU_CONV_K - 1)

    def gates(c, carry):
        t0 = pl.multiple_of(c * LRU_TT, LRU_TT)
        xc = _causal_taps(xpad_ref, cw_ref, t0, LRU_TT, shift, LRU_CONV_K,
                          jnp.zeros((LRU_TT, tc), F32) + cb)
        for hh in range(n_heads):
            sl = slice(hh * LRU_HD, (hh + 1) * LRU_HD)
            a, b = _lru_gates(xc[:, sl], wa_ref[hh].astype(BF16), ba_ref[:, sl],
                              wi_ref[hh].astype(BF16), bi_ref[:, sl], log_sig_lam[:, sl])
            a_ref[pl.ds(t0, LRU_TT), sl] = a
            b_ref[pl.ds(t0, LRU_TT), sl] = b
        return carry

    lax.fori_loop(0, t_len // LRU_TT, gates, 0)

    row = lax.broadcasted_iota(I32, (SUBLANES, tc), 0)

    def scan(i, h):
        t0 = pl.multiple_of(i * SUBLANES, SUBLANES)
        a = a_ref[pl.ds(t0, SUBLANES), :]
        b = b_ref[pl.ds(t0, SUBLANES), :]
        for d in (1, 2, 4):
            a_sh = pltpu.roll(a, d, axis=0)
            b_sh = pltpu.roll(b, d, axis=0)
            m = row >= d
            b = jnp.where(m, a * b_sh + b, b)
            a = jnp.where(m, a * a_sh, a)
        hb = a * h + b
        b_ref[pl.ds(t0, SUBLANES), :] = hb
        return hb[SUBLANES - 1:SUBLANES, :]

    h_last = lax.fori_loop(0, t_len // SUBLANES, scan, jnp.zeros((1, tc), F32), unroll=4)
    h_ref[0] = h_last

    def gate_out(c, carry):
        t0 = pl.multiple_of(c * LRU_TT, LRU_TT)
        y = b_ref[pl.ds(t0, LRU_TT), :] * _gelu_tanh(lg_ref[pl.ds(t0, LRU_TT), :])
        y_ref[pl.ds(t0, LRU_TT), :] = y.astype(y_ref.dtype)
        return carry

    lax.fori_loop(0, t_len // LRU_TT, gate_out, 0)


def _lru_prompt(z, cw, cb, wa, ba, wi, bi, lam):
    tc = CONV_TC
    nct = C_LRU // tc
    hpb = tc // LRU_HD
    col0 = 2 * C_CONV // tc
    vec = lambda v: v.reshape(1, -1)
    return pl.pallas_call(
        _lru_p_kernel,
        grid=(nct, BATCH + 1),
        in_specs=[pl.BlockSpec((SEQ, tc), lambda c, b: (_last_seq(b), c + col0)),
                  pl.BlockSpec((SEQ, tc), lambda c, b: (_last_seq(b), c + col0 + nct)),
                  pl.BlockSpec((LRU_CONV_K, tc), lambda c, b: (0, c)),
                  pl.BlockSpec((1, tc), lambda c, b: (0, c)),
                  pl.BlockSpec((hpb, LRU_HD, LRU_HD), lambda c, b: (c, 0, 0)),
                  pl.BlockSpec((1, tc), lambda c, b: (0, c)),
                  pl.BlockSpec((hpb, LRU_HD, LRU_HD), lambda c, b: (c, 0, 0)),
                  pl.BlockSpec((1, tc), lambda c, b: (0, c)),
                  pl.BlockSpec((1, tc), lambda c, b: (0, c))],
        out_specs=[pl.BlockSpec((SEQ, tc), lambda c, b: (b, c)),
                   pl.BlockSpec((1, LRU_CONV_K - 1, tc), lambda c, b: (_last_seq(b), 0, c)),
                   pl.BlockSpec((1, 1, tc), lambda c, b: (_last_seq(b), 0, c))],
        out_shape=[jax.ShapeDtypeStruct((N_TOK, C_LRU), BF16),
                   jax.ShapeDtypeStruct((BATCH, LRU_CONV_K - 1, C_LRU), F32),
                   jax.ShapeDtypeStruct((BATCH, 1, C_LRU), F32)],
        scratch_shapes=[pltpu.VMEM((LRU_PAD + SEQ + SUBLANES, tc), F32),
                        pltpu.VMEM((SEQ, tc), F32),
                        pltpu.VMEM((SEQ, tc), F32)],
        compiler_params=_cparams(("arbitrary", "arbitrary"), 12 * SEQ * tc * 4),
        name="lru_prompt",
    )(z, z, cw, vec(cb), wa, vec(ba), wi, vec(bi), vec(lam))


def _mixer_s_kernel(cv_ref, cg_ref, lx_ref, lg_ref, st_ref, lst_ref, h0_ref,
                    cw_ref, cb_ref, gng_ref, gnb_ref, lcw_ref, lcb_ref,
                    wa_ref, ba_ref, wi_ref, bi_ref, lam_ref, yc_in, yl_in,
                    yc_ref, yl_ref, nst_ref, nlst_ref, nh_ref):
    del yc_in, yl_in
    u = cv_ref[...] * _sigmoid(cg_ref[...])
    acc = u * cw_ref[CONV_K - 1:CONV_K, :] + cb_ref[...]
    for k in range(CONV_K - 1):
        row = st_ref[:, k, :]
        acc = acc + row * cw_ref[k:k + 1, :]
        if k > 0:
            nst_ref[:, k - 1, :] = row
    nst_ref[:, CONV_K - 2, :] = u

    def store_c(sl, v):
        yc_ref[:, sl] = v.astype(yc_ref.dtype)

    _group_norm_silu(acc, gng_ref[...], gnb_ref[...], store_c)

    lx = lx_ref[...]
    xc = lx * lcw_ref[LRU_CONV_K - 1:LRU_CONV_K, :] + lcb_ref[...]
    for k in range(LRU_CONV_K - 1):
        row = lst_ref[:, k, :]
        xc = xc + row * lcw_ref[k:k + 1, :]
        if k > 0:
            nlst_ref[:, k - 1, :] = row
    nlst_ref[:, LRU_CONV_K - 2, :] = lx

    log_sig_lam = _log_sigmoid(lam_ref[...])
    for hh in range(LRU_HEADS):
        sl = slice(hh * LRU_HD, (hh + 1) * LRU_HD)
        a, b = _lru_gates(xc[:, sl], wa_ref[hh].astype(BF16), ba_ref[:, sl],
                          wi_ref[hh].astype(BF16), bi_ref[:, sl], log_sig_lam[:, sl])
        h = a * h0_ref[:, sl] + b
        nh_ref[:, sl] = h
        yl_ref[:, sl] = (h * _gelu_tanh(lg_ref[:, sl])).astype(yl_ref.dtype)


def _mixer_sample(z, st, lst, h0, layer, yc, yl, cw, cb, gn_g, gn_b, lcw, lcb, wa, ba, wi, bi, lam):
    bt = S_BT
    rb0 = N_P // bt
    vec = lambda v: v.reshape(1, -1)
    zspec = lambda col: pl.BlockSpec((bt, C_CONV), lambda i: (i + rb0, col))
    full = lambda shape: pl.BlockSpec(shape, lambda i: (0,) * len(shape))
    any_spec = pl.BlockSpec(memory_space=pl.ANY)
    return pl.pallas_call(
        _mixer_s_kernel,
        grid=(N_S // bt,),
        in_specs=[zspec(0), zspec(1), zspec(2), zspec(3),
                  pl.BlockSpec((None, bt, CONV_K - 1, C_CONV), lambda i: (layer, i, 0, 0)),
                  pl.BlockSpec((None, bt, LRU_CONV_K - 1, C_LRU), lambda i: (layer, i, 0, 0)),
                  pl.BlockSpec((None, bt, C_LRU), lambda i: (layer, i, 0)),
                  full((CONV_K, C_CONV)), full((1, C_CONV)), full((1, C_CONV)), full((1, C_CONV)),
                  full((LRU_CONV_K, C_LRU)), full((1, C_LRU)),
                  full((LRU_HEADS, LRU_HD, LRU_HD)), full((1, C_LRU)),
                  full((LRU_HEADS, LRU_HD, LRU_HD)), full((1, C_LRU)), full((1, C_LRU)),
                  any_spec, any_spec],
        out_specs=[pl.BlockSpec((bt, C_CONV), lambda i: (i + rb0, 0)),
                   pl.BlockSpec((bt, C_LRU), lambda i: (i + rb0, 0)),
                   pl.BlockSpec((bt, CONV_K - 1, C_CONV), lambda i: (i, 0, 0)),
                   pl.BlockSpec((bt, LRU_CONV_K - 1, C_LRU), lambda i: (i, 0, 0)),
                   pl.BlockSpec((bt, C_LRU), lambda i: (i, 0))],
        out_shape=[jax.ShapeDtypeStruct(yc.shape, yc.dtype),
                   jax.ShapeDtypeStruct(yl.shape, yl.dtype),
                   jax.ShapeDtypeStruct(st.shape[1:], F32),
                   jax.ShapeDtypeStruct(lst.shape[1:], F32),
                   jax.ShapeDtypeStruct(h0.shape[1:], F32)],
        input_output_aliases={18: 0, 19: 1},
        compiler_params=_cparams(("arbitrary",), 48 << 20),
        name="mixer_sample",
    )(z, z, z, z, st, lst, h0, cw, vec(cb), vec(gn_g), vec(gn_b), lcw, vec(lcb),
      wa, vec(ba), wi, vec(bi), vec(lam), yc, yl)


def _attn_p_kernel(q_ref, k_ref, v_ref, o_ref):
    i = pl.program_id(0)

    @pl.when(i < N_P // ATT_TQ)
    def _():
        _attn_p_body(q_ref, k_ref, v_ref, o_ref)

    @pl.when(i == N_P // ATT_TQ)
    def _():
        o_ref[...] = jnp.zeros(o_ref.shape, o_ref.dtype)


def _attn_p_body(q_ref, k_ref, v_ref, o_ref):
    scale = MEM_HD ** -0.5
    for h in range(MEM_HEADS):
        sl = slice(h * MEM_HD, (h + 1) * MEM_HD)
        q = q_ref[:, sl].astype(BF16)
        k = k_ref[:, sl].astype(BF16)
        v = v_ref[:, sl].astype(BF16)
        s = lax.dot_general(q, k, (((1,), (1,)), ((), ())), preferred_element_type=F32) * scale
        p = jnp.exp(s - jnp.max(s, axis=-1, keepdims=True))
        pr = p / jnp.sum(p, axis=-1, keepdims=True)
        o = jnp.dot(pr.astype(BF16), v, preferred_element_type=F32)
        o_ref[:, sl] = o.astype(o_ref.dtype)


def _attn_prompt(q, k, v):
    nq = SEQ // ATT_TQ
    return pl.pallas_call(
        _attn_p_kernel,
        grid=(BATCH * nq + 1,),
        in_specs=[pl.BlockSpec((ATT_TQ, MEM_W), lambda i: (i, 0)),
                  pl.BlockSpec((MEM_LEN, MEM_W), lambda i: (_last_seq(i // nq), 0)),
                  pl.BlockSpec((MEM_LEN, MEM_W), lambda i: (_last_seq(i // nq), 0))],
        out_specs=pl.BlockSpec((ATT_TQ, MEM_W), lambda i: (i, 0)),
        out_shape=jax.ShapeDtypeStruct((N_TOK, MEM_W), BF16),
        compiler_params=_cparams(("arbitrary",), 32 << 20),
        name="attn_prompt",
    )(q, k, v)


def _attn_s_kernel(q_ref, k_ref, v_ref, o_in, o_ref, stage_ref):
    del o_in
    jj = pl.program_id(1)
    scale = MEM_HD ** -0.5
    for bb in range(ATT_S_BT):
        r = jj * ATT_S_BT + bb
        q = q_ref[pl.ds(r, 1)]
        s = jnp.sum(k_ref[bb] * q, axis=-1, keepdims=True) * scale
        p = jnp.exp(s - jnp.max(s, axis=0, keepdims=True))
        l = jnp.sum(p, axis=0)
        o = jnp.sum(p * v_ref[bb], axis=0) / l
        for h in range(MEM_HEADS):
            stage_ref[pl.ds(r, 1), h * MEM_HD:(h + 1) * MEM_HD] = o[h:h + 1, :]

    @pl.when(jj == pl.num_programs(1) - 1)
    def _():
        o_ref[...] = stage_ref[...].astype(o_ref.dtype)


def _attn_sample(q4, kc, vc, layer, o):
    bt = S_BT
    inner = bt // ATT_S_BT
    rb0 = N_P // bt
    cache_spec = pl.BlockSpec((None, ATT_S_BT, MEM_LEN, MEM_HEADS, MEM_HD),
                              lambda i, j: (layer, i * inner + j, 0, 0, 0))
    return pl.pallas_call(
        _attn_s_kernel,
        grid=(N_S // bt, inner),
        in_specs=[pl.BlockSpec((bt, MEM_HEADS, MEM_HD), lambda i, j: (i, 0, 0)),
                  cache_spec, cache_spec,
                  pl.BlockSpec(memory_space=pl.ANY)],
        out_specs=pl.BlockSpec((bt, MEM_W), lambda i, j: (i + rb0, 0)),
        out_shape=jax.ShapeDtypeStruct(o.shape, o.dtype),
        scratch_shapes=[pltpu.VMEM((bt, MEM_W), F32)],
        input_output_aliases={3: 0},
        compiler_params=_cparams(("arbitrary", "arbitrary"), 48 << 20),
        name="attn_sample",
    )(q4, kc, vc, o)


def _norm_router_kernel(x_ref, g_ref, wr_ref, br_ref, xp_ref, ids_ref, gates_ref):
    xn = _rms(x_ref[...], g_ref[...])
    xp_ref[...] = pltpu.pack_elementwise([xn[:, :HALF], xn[:, HALF:]], packed_dtype=BF16)
    logits = jnp.dot(xn.astype(BF16), wr_ref[...], preferred_element_type=F32) + br_ref[...]
    lane = lax.broadcasted_iota(I32, logits.shape, 1)
    neg = jnp.float32(-jnp.inf)

    def first_max(vals):
        m = jnp.max(vals, axis=-1, keepdims=True)
        idx = jnp.min(jnp.where(vals == m, lane, ROUTER_W), axis=-1, keepdims=True)
        return m, idx

    is_group = lane < N_GROUPS
    g_max, g_sel = first_max(jnp.where(is_group, logits, neg))
    p_g = 1.0 / jnp.sum(jnp.where(is_group, jnp.exp(logits - g_max), 0.0), axis=-1, keepdims=True)
    lo = N_GROUPS + EXPERTS_PER_GROUP * g_sel
    in_group = (lane >= lo) & (lane < lo + EXPERTS_PER_GROUP)
    e_logits = jnp.where(in_group, logits, neg)
    t1, i1 = first_max(e_logits)
    t2, i2 = first_max(jnp.where(lane == i1, neg, e_logits))
    e = jnp.exp(t2 - t1)
    w1 = p_g / (1.0 + e)
    w2 = p_g * e / (1.0 + e)
    ids_ref[...] = jnp.where(lane == 0, i1 - N_GROUPS, jnp.where(lane == 1, i2 - N_GROUPS, 0))
    gates_ref[...] = jnp.where(lane == 0, w1, jnp.where(lane == 1, w2, 0.0))


def _norm_router(x, g, wr, br):
    tm = TM_NORM
    return pl.pallas_call(
        _norm_router_kernel,
        grid=(N_TOK // tm,),
        in_specs=[pl.BlockSpec((tm, D_MODEL), lambda i: (i, 0)),
                  pl.BlockSpec((1, D_MODEL), lambda i: (0, 0)),
                  pl.BlockSpec((D_MODEL, ROUTER_W), lambda i: (0, 0)),
                  pl.BlockSpec((1, ROUTER_W), lambda i: (0, 0))],
        out_specs=[pl.BlockSpec((tm, HALF), lambda i: (i, 0)),
                   pl.BlockSpec((tm, ROUTER_W), lambda i: (i, 0)),
                   pl.BlockSpec((tm, ROUTER_W), lambda i: (i, 0))],
        out_shape=[jax.ShapeDtypeStruct((N_TOK, HALF), jnp.uint32),
                   jax.ShapeDtypeStruct((N_TOK, ROUTER_W), I32),
                   jax.ShapeDtypeStruct((N_TOK, ROUTER_W), F32)],
        compiler_params=_cparams(("arbitrary",), 8 * tm * D_MODEL * 4),
        name="norm_router",
    )(x, g.reshape(1, -1), wr, br)


def _unpack_x(xp):
    lo = pltpu.unpack_elementwise(xp, index=0, packed_dtype=BF16, unpacked_dtype=F32)
    hi = pltpu.unpack_elementwise(xp, index=1, packed_dtype=BF16, unpacked_dtype=F32)
    return lo.astype(BF16), hi.astype(BF16)


def _dot(a, w):
    return lax.dot_general(a, w, (((1,), (0,)), ((), ())), preferred_element_type=F32)


def _ffn_kernel(sbe_ref, sbb_ref, sbn_ref, sbz_ref, tok_ref, xp_hbm, wg_ref, wu_ref, wd_ref, ys_hbm,
                xbuf_ref, acc_ref, xsem, ysem, pend_ref):
    del sbe_ref
    sb = pl.program_id(0)
    j = pl.program_id(1)
    nj = pl.num_programs(1)
    slot = sb % 2

    @pl.when((sb == 0) & (j == 0))
    def _():
        pend_ref[0] = 0

    def row_copy(s, slot_, r):
        tok = tok_ref[sbb_ref[s] * MOE_BLOCK + r]
        return pltpu.make_async_copy(xp_hbm.at[pl.ds(tok, 1), :],
                                     xbuf_ref.at[slot_, pl.ds(r, 1), :], xsem.at[slot_])

    def start_blocks(s, slot_, lo, hi):
        def block(i, carry):
            def body(r, c):
                row_copy(s, slot_, i * MOE_BLOCK + r).start(priority=GATHER_PRIORITY)
                return c

            return lax.fori_loop(0, MOE_BLOCK, body, carry, unroll=ISSUE_UNROLL)

        lax.fori_loop(lo, jnp.minimum(hi, sbn_ref[s]), block, 0)

    def wait_rows(s, slot_):
        def block(i, carry):
            r0 = pl.multiple_of(i * MOE_BLOCK, MOE_BLOCK)
            pltpu.make_async_copy(xp_hbm.at[pl.ds(0, MOE_BLOCK), :],
                                  xbuf_ref.at[slot_, pl.ds(r0, MOE_BLOCK), :], xsem.at[slot_]).wait()
            return carry

        lax.fori_loop(0, sbn_ref[s], block, 0)

    def out_copy(i, blk):
        r0 = pl.multiple_of(i * MOE_BLOCK, MOE_BLOCK)
        r1 = pl.multiple_of(blk * MOE_BLOCK, MOE_BLOCK)
        return pltpu.make_async_copy(acc_ref.at[pl.ds(r0, MOE_BLOCK), :],
                                     ys_hbm.at[pl.ds(r1, MOE_BLOCK), :], ysem)

    def wait_out():
        def body(i, carry):
            out_copy(0, 0).wait()
            return carry

        lax.fori_loop(0, pend_ref[0], body, 0)
        pend_ref[0] = 0

    per_j = -(-NSUB // (D_EXPERT // TF))

    @pl.when((sb == 0) & (j == 0))
    def _():
        start_blocks(sb, slot, 0, NSUB)

    @pl.when(sb + 1 < pl.num_programs(0))
    def _():
        start_blocks(sb + 1, 1 - slot, j * per_j, (j + 1) * per_j)

    nsub = sbn_ref[sb]
    nzero = sbz_ref[sb]

    def zero_block(i, carry):
        r0 = pl.multiple_of(i * MOE_BLOCK, MOE_BLOCK)
        acc_ref[pl.ds(r0, MOE_BLOCK), :] = jnp.zeros((MOE_BLOCK, D_MODEL), F32)
        return carry

    def start_out(i, carry):
        out_copy(i, sbb_ref[sb] + i).start()
        return carry

    @pl.when(j == 0)
    def _():
        wait_rows(sb, slot)
        wait_out()
        lax.fori_loop(0, jnp.maximum(nsub, nzero), zero_block, 0)
        lax.fori_loop(0, nzero, start_out, 0)
        pend_ref[0] = nzero

    def ffn_rows(r0, rows):
        lo, hi = _unpack_x(xbuf_ref[slot, pl.ds(r0, rows), :])
        g = _dot(lo, wg_ref[:HALF, :]) + _dot(hi, wg_ref[HALF:, :])
        u = _dot(lo, wu_ref[:HALF, :]) + _dot(hi, wu_ref[HALF:, :])
        h = (_silu(g) * u).astype(BF16)
        acc_ref[pl.ds(r0, rows), :] += _dot(h, wd_ref[...])

    def pair(i, carry):
        ffn_rows(pl.multiple_of(i * (2 * MOE_BLOCK), 2 * MOE_BLOCK), 2 * MOE_BLOCK)
        return carry

    lax.fori_loop(0, nsub // 2, pair, 0)

    @pl.when(nsub % 2 == 1)
    def _():
        ffn_rows(pl.multiple_of((nsub - 1) * MOE_BLOCK, MOE_BLOCK), MOE_BLOCK)

    @pl.when((j == nj - 1) & (nsub > 0))
    def _():
        lax.fori_loop(0, nsub, start_out, 0)
        pend_ref[0] = nsub

    @pl.when((sb == pl.num_programs(0) - 1) & (j == nj - 1))
    def _():
        wait_out()


def _moe_ffn(xp, plan, w_gate, w_up, w_down, layer):
    nj = D_EXPERT // TF

    def hidden_slice(sb, j, sbn):
        return jnp.where(sbn[sb] > 0, j, nj - 1)

    up_spec = pl.BlockSpec((None, None, D_MODEL, TF),
                           lambda sb, j, sbe, sbb, sbn, sbz, tok: (layer, sbe[sb], 0, hidden_slice(sb, j, sbn)))
    down_spec = pl.BlockSpec((None, None, TF, D_MODEL),
                             lambda sb, j, sbe, sbb, sbn, sbz, tok: (layer, sbe[sb], hidden_slice(sb, j, sbn), 0))
    return pl.pallas_call(
        _ffn_kernel,
        grid_spec=pltpu.PrefetchScalarGridSpec(
            num_scalar_prefetch=5,
            grid=(N_SB, nj),
            in_specs=[pl.BlockSpec(memory_space=pl.ANY), up_spec, up_spec, down_spec],
            out_specs=pl.BlockSpec(memory_space=pl.ANY),
            scratch_shapes=[pltpu.VMEM((2, SB_ROWS, HALF), jnp.uint32),
                            pltpu.VMEM((SB_ROWS, D_MODEL), F32),
                            pltpu.SemaphoreType.DMA((2,)),
                            pltpu.SemaphoreType.DMA(()),
                            pltpu.SMEM((1,), I32)]),
        out_shape=jax.ShapeDtypeStruct((N_ROWS, D_MODEL), F32),
        compiler_params=_cparams(("arbitrary", "arbitrary"), VMEM_CAP),
        name="moe_ffn",
    )(plan["sb_e"], plan["sb_blk"], plan["sb_n"], plan["sb_z"], plan["row_tok"], xp, w_gate, w_up, w_down)


def _combine_kernel(dest_ref, x_ref, gates_ref, g_ref, ys_hbm, out_a, out_b, buf_ref, sem, *, final):
    i = pl.program_id(0)
    n = pl.num_programs(0)

    def row_copy(step, slot, t, k):
        s = (step * COMB_TM + t) * TOP_K + k
        return pltpu.make_async_copy(ys_hbm.at[pl.ds(dest_ref[s], 1), :],
                                     buf_ref.at[slot, k, pl.ds(t, 1), :], sem.at[slot])

    def start_all(step, slot):
        def body(t, carry):
            for k in range(TOP_K):
                row_copy(step, slot, t, k).start(priority=k % 2)
            return carry

        lax.fori_loop(0, COMB_TM, body, 0, unroll=ISSUE_UNROLL)

    def wait_all(step, slot):
        for k in range(TOP_K):
            pltpu.make_async_copy(ys_hbm.at[pl.ds(0, COMB_TM), :], buf_ref.at[slot, k], sem.at[slot]).wait()

    slot = i % 2

    @pl.when(i == 0)
    def _():
        start_all(i, slot)

    @pl.when(i + 1 < n)
    def _():
        start_all(i + 1, 1 - slot)

    wait_all(i, slot)
    gates = gates_ref[...]
    y = x_ref[...] + gates[:, 0:1] * buf_ref[slot, 0] + gates[:, 1:2] * buf_ref[slot, 1]
    yn = _rms(y, g_ref[...])
    if final:
        @pl.when(i < n - 1)
        def _():
            out_a[...] = yn

        @pl.when(i == n - 1)
        def _():
            out_b[...] = yn
    else:
        out_a[...] = y
        out_b[...] = yn.astype(out_b.dtype)


def _combine(x, ys, dest, gates, g_next, *, final):
    tm = COMB_TM
    assert N_S == tm
    n_p_blocks = N_P // tm
    if final:
        out_specs = [pl.BlockSpec((tm, D_MODEL), lambda i, d: (jnp.minimum(i, n_p_blocks - 1), 0)),
                     pl.BlockSpec((tm, D_MODEL), lambda i, d: (0, 0))]
        out_shape = [jax.ShapeDtypeStruct((N_P, D_MODEL), F32),
                     jax.ShapeDtypeStruct((N_S, D_MODEL), F32)]
    else:
        out_specs = [pl.BlockSpec((tm, D_MODEL), lambda i, d: (i, 0)),
                     pl.BlockSpec((tm, D_MODEL), lambda i, d: (i, 0))]
        out_shape = [jax.ShapeDtypeStruct((N_TOK, D_MODEL), F32),
                     jax.ShapeDtypeStruct((N_TOK, D_MODEL), BF16)]
    return pl.pallas_call(
        functools.partial(_combine_kernel, final=final),
        grid_spec=pltpu.PrefetchScalarGridSpec(
            num_scalar_prefetch=1,
            grid=(N_TOK // tm,),
            in_specs=[pl.BlockSpec((tm, D_MODEL), lambda i, d: (i, 0)),
                      pl.BlockSpec((tm, ROUTER_W), lambda i, d: (i, 0)),
                      pl.BlockSpec((1, D_MODEL), lambda i, d: (0, 0)),
                      pl.BlockSpec(memory_space=pl.ANY)],
            out_specs=out_specs,
            scratch_shapes=[pltpu.VMEM((2, TOP_K, tm, D_MODEL), F32),
                            pltpu.SemaphoreType.DMA((2,))]),
        out_shape=out_shape,
        compiler_params=_cparams(("arbitrary",), 40 << 20),
        name="moe_combine",
    )(dest, x, gates, g_next.reshape(1, -1), ys)


def _dispatch_plan(ids):
    flat_e = ids[:, :TOP_K].reshape(-1)
    onehot = (flat_e[:, None] == jnp.arange(N_EXPERTS, dtype=I32)[None, :]).astype(I32)
    chunks = onehot.reshape(N_SLOT // PLAN_CHUNK, PLAN_CHUNK, N_EXPERTS).astype(BF16)
    tri = jnp.tril(jnp.ones((PLAN_CHUNK, PLAN_CHUNK), BF16))
    within = jnp.einsum("ij,cjk->cik", tri, chunks, preferred_element_type=F32).astype(I32)
    totals = within[:, -1, :]
    before = jnp.cumsum(totals, axis=0) - totals
    csum = (within + before[:, None, :]).reshape(N_SLOT, N_EXPERTS)
    rank = jnp.sum(onehot * csum, axis=1) - 1
    counts = csum[-1]
    nblk = (counts + MOE_BLOCK - 1) // MOE_BLOCK
    bend = jnp.cumsum(nblk)
    bstart = bend - nblk
    dest = (jnp.sum(onehot * bstart[None, :], axis=1) * MOE_BLOCK + rank).astype(I32)
    row_tok = jnp.zeros((N_ROWS,), I32).at[dest].set(jnp.arange(N_SLOT, dtype=I32) // TOP_K)

    nsup = (nblk + NSUB - 1) // NSUB
    sup_end = jnp.cumsum(nsup)
    sup_start = sup_end - nsup
    total_sup = sup_end[-1]
    sb = jnp.arange(N_SB, dtype=I32)
    e_of = jnp.minimum(jnp.sum((sup_end[None, :] <= sb[:, None]).astype(I32), axis=1), N_EXPERTS - 1)
    q = sb - sup_start[e_of]
    real = sb < total_sup
    n_real = jnp.clip(nblk[e_of] - NSUB * q, 0, NSUB)
    blk_real = bstart[e_of] + NSUB * q
    blk_tail = bend[-1] + NSUB * (sb - total_sup)
    n_tail = jnp.clip(N_BLOCKS - blk_tail, 0, NSUB)
    e_last = e_of[total_sup - 1]
    return dict(
        dest=dest, row_tok=row_tok,
        sb_e=jnp.where(real, e_of, e_last).astype(I32),
        sb_blk=jnp.where(real, blk_real, jnp.minimum(blk_tail, N_BLOCKS - 1)).astype(I32),
        sb_n=jnp.where(real, n_real, 0).astype(I32),
        sb_z=jnp.where(real, 0, n_tail).astype(I32))


def kernel(x_prompt, x_sample, mem_prompt, state_conv, state_lru_conv, state_lru_h, cache_mem_k, cache_mem_v, norm_mix, w_in, conv_w, conv_b, conv_gn_g, conv_gn_b, lru_conv_w, lru_conv_b, lru_wa, lru_ba, lru_wi, lru_bi, lru_lambda, w_out, norm_attn, norm_mem_kv, w_q, w_k, w_v, w_o, norm_ffn, w_router_g, b_router_g, w_router_e, b_router_e, w_gate, w_up, w_down, norm_final):
    x = jnp.concatenate([x_prompt.reshape(N_P, D_MODEL), x_sample.reshape(N_S, D_MODEL)], axis=0)
    mem = mem_prompt.reshape(BATCH * MEM_LEN, D_MODEL)
    xn = _norm(x, norm_mix[0], tm=TM_NORM)

    conv_p, lruc_p, h_p, mk_p, mv_p, conv_s, lruc_s, h_s = ([] for _ in range(8))
    for l in range(DEPTH):
        z = _mm([xn], w_in, l, tn=1024, tm=TM, name="w_in")
        yc, cst = _conv_prompt(z, conv_w[l], conv_b[l], conv_gn_g[l], conv_gn_b[l])
        yl, lst, hl = _lru_prompt(z, lru_conv_w[l], lru_conv_b[l], lru_wa[l], lru_ba[l],
                                  lru_wi[l], lru_bi[l], lru_lambda[l])
        yc, yl, cst_s, lst_s, hl_s = _mixer_sample(
            z, state_conv, state_lru_conv, state_lru_h, l, yc, yl,
            conv_w[l], conv_b[l], conv_gn_g[l], conv_gn_b[l], lru_conv_w[l], lru_conv_b[l],
            lru_wa[l], lru_ba[l], lru_wi[l], lru_bi[l], lru_lambda[l])
        x = _mm([yc, yl], w_out, l, tn=1024, tm=TM_NORM, res=x, name="w_out")
        conv_p.append(cst); lruc_p.append(lst); h_p.append(hl.reshape(BATCH, C_LRU))
        conv_s.append(cst_s); lruc_s.append(lst_s); h_s.append(hl_s)

        xn = _norm(x, norm_attn[l], tm=TM_NORM)
        q = _mm([xn], w_q, l, tn=1024, tm=TM, name="w_q")
        mn = _norm(mem, norm_mem_kv[l], tm=256)
        k_p = _mm([mn], w_k, l, tn=512, tm=BATCH * MEM_LEN, name="w_k")
        v_p = _mm([mn], w_v, l, tn=512, tm=BATCH * MEM_LEN, name="w_v")
        o = _attn_prompt(q, k_p, v_p)
        o = _attn_sample(q[N_P:].reshape(N_S, MEM_HEADS, MEM_HD), cache_mem_k, cache_mem_v, l, o)
        x = _mm([o], w_o, l, tn=1024, tm=TM, res=x, name="w_o")
        mk_p.append(k_p.reshape(BATCH, MEM_LEN, MEM_HEADS, MEM_HD))
        mv_p.append(v_p.reshape(BATCH, MEM_LEN, MEM_HEADS, MEM_HD))

        wr = jnp.concatenate([w_router_g[l], w_router_e[l].reshape(D_MODEL, N_EXPERTS),
                              jnp.zeros((D_MODEL, ROUTER_W - N_GROUPS - N_EXPERTS), F32)],
                             axis=1).astype(BF16)
        br = jnp.concatenate([b_router_g[l], b_router_e[l].reshape(N_EXPERTS),
                              jnp.zeros((ROUTER_W - N_GROUPS - N_EXPERTS,), F32)]).reshape(1, ROUTER_W)
        xp, ids, gates = _norm_router(x, norm_ffn[l], wr, br)
        plan = _dispatch_plan(ids)
        ys = _moe_ffn(xp, plan, w_gate, w_up, w_down, l)
        if l < DEPTH - 1:
            x, xn = _combine(x, ys, plan["dest"], gates, norm_mix[l + 1], final=False)
        else:
            y_p, y_s = _combine(x, ys, plan["dest"], gates, norm_final, final=True)

    y_prompt = y_p.reshape(BATCH, SEQ, D_MODEL)
    y_sample = y_s.reshape(DEC_BATCH, 1, D_MODEL)
    return (y_prompt, y_sample, jnp.stack(conv_p), jnp.stack(lruc_p), jnp.stack(h_p),
            jnp.stack(mk_p), jnp.stack(mv_p), jnp.stack(conv_s), jnp.stack(lruc_s), jnp.stack(h_s))
```

```python
import functools
import math

import jax
import jax.numpy as jnp
from jax import lax
from jax.experimental import pallas as pl
from jax.experimental.pallas import tpu as pltpu

F32 = jnp.float32
BF16 = jnp.bfloat16
I32 = jnp.int32

D_MODEL = 4096
BATCH = 4
SEQ = 2048
DEPTH = 2
DEC_BATCH = 128
C_CONV = D_MODEL // 2
C_LRU = D_MODEL // 2
D_IN = 2 * C_CONV + 2 * C_LRU
CONV_GROUPS = 16
GROUP_W = C_CONV // CONV_GROUPS
CONV_K = 31
LRU_HEADS = 16
LRU_HD = C_LRU // LRU_HEADS
LRU_CONV_K = 4
RG_C = 8.0
MEM_LEN = 256
MEM_HEADS = 4
MEM_HD = D_MODEL // 16
MEM_W = MEM_HEADS * MEM_HD
N_GROUPS = 4
EXPERTS_PER_GROUP = 8
N_EXPERTS = N_GROUPS * EXPERTS_PER_GROUP
TOP_K = 2
D_EXPERT = D_MODEL // 4
EPS = 1e-6

N_P = BATCH * SEQ
N_S = DEC_BATCH
N_TOK = N_P + N_S
N_SLOT = N_TOK * TOP_K

LANES = 128
SUBLANES = 8
VMEM_CAP = 56 * 1024 * 1024

TM = 1040
TM_NORM = 416
MOE_BLOCK = 128
N_BLOCKS = -(-(N_SLOT + N_EXPERTS * (MOE_BLOCK - 1)) // MOE_BLOCK)
N_ROWS = N_BLOCKS * MOE_BLOCK
TF = 256
ROUTER_W = LANES
HALF = D_MODEL // 2
CONV_TT = 64
CONV_TC = 256
LRU_TT = 256
COMB_TM = 128
PLAN_CHUNK = 128
ATT_TQ = 512
S_BT = 16
ATT_S_BT = 4
ISSUE_UNROLL = 8
GATHER_PRIORITY = 1
NSUB = 6
SB_ROWS = NSUB * MOE_BLOCK
N_SB = (N_BLOCKS + (NSUB - 1) * (N_EXPERTS + 1)) // NSUB + 1


def _cparams(sem, vmem_bytes):
    limit = min(VMEM_CAP, max(32 * 1024 * 1024, int(vmem_bytes)))
    return pltpu.CompilerParams(dimension_semantics=sem, vmem_limit_bytes=limit)


def _sigmoid(x):
    return 1.0 / (1.0 + jnp.exp(-x))


def _silu(x):
    return x * _sigmoid(x)


def _gelu_tanh(x):
    c = math.sqrt(2.0 / math.pi)
    return 0.5 * x * (1.0 + jnp.tanh(c * (x + 0.044715 * (x * x * x))))


def _dot(a, w):
    return lax.dot_general(a, w, (((1,), (0,)), ((), ())), preferred_element_type=F32)


def _bf16_round(x):
    return x.astype(BF16).astype(F32)


def _rms(x, g):
    ms = jnp.mean(x * x, axis=-1, keepdims=True)
    return x * lax.rsqrt(ms + EPS) * g


def _norm_kernel(x_ref, g_ref, o_ref):
    o_ref[...] = _rms(x_ref[...], g_ref[...]).astype(o_ref.dtype)


def _norm(x, g, *, tm, out_dtype=BF16, row_block0=0, n_rows=None):
    n_rows = x.shape[0] if n_rows is None else n_rows
    d = x.shape[1]
    return pl.pallas_call(
        _norm_kernel,
        grid=(n_rows // tm,),
        in_specs=[pl.BlockSpec((tm, d), lambda i: (i + row_block0, 0)),
                  pl.BlockSpec((1, d), lambda i: (0, 0))],
        out_specs=pl.BlockSpec((tm, d), lambda i: (i, 0)),
        out_shape=jax.ShapeDtypeStruct((n_rows, d), out_dtype),
        compiler_params=_cparams(("arbitrary",), 6 * tm * d * 4),
        name="rmsnorm",
    )(x, g.reshape(1, d))


def _mm_kernel(*refs, n_a, has_res):
    a_refs = refs[:n_a]
    w_ref = refs[n_a]
    res_ref = refs[n_a + 1] if has_res else None
    o_ref = refs[n_a + 1 + int(has_res)]
    acc = None
    off = 0
    for a_ref in a_refs:
        k = a_ref.shape[1]
        part = lax.dot_general(a_ref[...].astype(BF16), w_ref[off:off + k, :], (((1,), (0,)), ((), ())),
                               preferred_element_type=F32)
        acc = part if acc is None else acc + part
        off += k
    if has_res:
        acc = acc + res_ref[...]
    o_ref[...] = acc.astype(o_ref.dtype)


def _mm(a_list, w, layer, *, tn, tm, res=None, out_dtype=F32, name="proj"):
    m = a_list[0].shape[0]
    _, k, n = w.shape
    assert sum(a.shape[1] for a in a_list) == k
    in_specs = [pl.BlockSpec((tm, a.shape[1]), lambda j, i: (i, 0)) for a in a_list]
    in_specs.append(pl.BlockSpec((None, k, tn), lambda j, i: (layer, 0, j), pipeline_mode=pl.Buffered(1)))
    args = list(a_list) + [w]
    if res is not None:
        in_specs.append(pl.BlockSpec((tm, tn), lambda j, i: (i, j)))
        args.append(res)
    a_bytes = sum(a.dtype.itemsize * a.shape[1] for a in a_list) * tm
    vmem = 2 * a_bytes + k * tn * 4 + 6 * tm * tn * 4 + (4 << 20)
    return pl.pallas_call(
        functools.partial(_mm_kernel, n_a=len(a_list), has_res=res is not None),
        grid=(n // tn, m // tm),
        in_specs=in_specs,
        out_specs=pl.BlockSpec((tm, tn), lambda j, i: (i, j)),
        out_shape=jax.ShapeDtypeStruct((m, n), out_dtype),
        compiler_params=_cparams(("arbitrary", "arbitrary"), vmem),
        name=name,
    )(*args)


def _norm_mm_kernel(x_ref, g_ref, w_ref, o_ref):
    o_ref[...] = _dot(_rms(x_ref[...], g_ref[...]).astype(BF16), w_ref[...])


def _norm_mm(x, g, w, layer, *, tm, name):
    m, d = x.shape
    _, k, n = w.shape
    return pl.pallas_call(
        _norm_mm_kernel,
        grid=(m // tm,),
        in_specs=[pl.BlockSpec((tm, d), lambda i: (i, 0)),
                  pl.BlockSpec((1, d), lambda i: (0, 0)),
                  pl.BlockSpec((None, k, n), lambda i: (layer, 0, 0), pipeline_mode=pl.Buffered(1))],
        out_specs=pl.BlockSpec((tm, n), lambda i: (i, 0)),
        out_shape=jax.ShapeDtypeStruct((m, n), F32),
        compiler_params=_cparams(("arbitrary",), VMEM_CAP),
        name=name,
    )(x, g.reshape(1, d), w)


def _group_norm_silu(y, gn_g, gn_b, store):
    for g in range(y.shape[1] // GROUP_W):
        sl = slice(g * GROUP_W, (g + 1) * GROUP_W)
        yg = y[:, sl]
        mu = jnp.mean(yg, axis=-1, keepdims=True)
        d = yg - mu
        var = jnp.mean(d * d, axis=-1, keepdims=True)
        yn = d * lax.rsqrt(var + EPS) * gn_g[:, sl] + gn_b[:, sl]
        store(sl, _silu(yn))


CONV_PAD = 32


def _causal_taps(src_ref, w_ref, t0, tt, first_off, n_taps, init):
    acc = init
    for s in range(SUBLANES):
        part = None
        for k in range(n_taps):
            off = first_off + k
            if off % SUBLANES != s:
                continue
            base = pl.multiple_of(t0 + (off - s), SUBLANES)
            term = src_ref[pl.ds(base, tt + SUBLANES), :] * w_ref[k:k + 1, :]
            part = term if part is None else part + term
        if part is not None:
            acc = acc + part[s:s + tt, :]
    return acc


def _last_seq(b):
    return jnp.minimum(b, BATCH - 1)


def _prompt_or_zero(body, y_ref):
    b = pl.program_id(1)

    @pl.when(b < BATCH)
    def _():
        body()

    @pl.when(b == BATCH)
    def _():
        y_ref[...] = jnp.zeros(y_ref.shape, y_ref.dtype)


def _conv_p_kernel(*refs):
    _prompt_or_zero(functools.partial(_conv_p_body, *refs), refs[6])


def _conv_p_body(cv_ref, cg_ref, w_ref, b_ref, gng_ref, gnb_ref, y_ref, st_ref, upad_ref, wr_ref):
    t_len = cv_ref.shape[0]
    tc = cv_ref.shape[1]
    wr_ref[0:CONV_K, :] = _bf16_round(w_ref[...])
    upad_ref[0:CONV_PAD, :] = jnp.zeros((CONV_PAD, tc), F32)
    upad_ref[CONV_PAD + t_len:CONV_PAD + t_len + SUBLANES, :] = jnp.zeros((SUBLANES, tc), F32)

    def glu(c, carry):
        t0 = pl.multiple_of(c * LRU_TT, LRU_TT)
        u = cv_ref[pl.ds(t0, LRU_TT), :] * _sigmoid(cg_ref[pl.ds(t0, LRU_TT), :])
        upad_ref[pl.ds(CONV_PAD + t0, LRU_TT), :] = _bf16_round(u)
        return carry

    lax.fori_loop(0, t_len // LRU_TT, glu, 0)
    tail = slice(t_len - (CONV_K - 1), t_len)
    st_ref[0] = cv_ref[tail, :] * _sigmoid(cg_ref[tail, :])

    bias = b_ref[...]
    gn_g = gng_ref[...]
    gn_b = gnb_ref[...]
    shift = CONV_PAD - (CONV_K - 1)

    def chunk(c, carry):
        t0 = pl.multiple_of(c * CONV_TT, CONV_TT)
        acc = _causal_taps(upad_ref, wr_ref, t0, CONV_TT, shift, CONV_K,
                           jnp.zeros((CONV_TT, tc), F32) + bias)

        def store(sl, v):
            y_ref[pl.ds(t0, CONV_TT), sl] = v.astype(y_ref.dtype)

        _group_norm_silu(acc, gn_g, gn_b, store)
        return carry

    lax.fori_loop(0, t_len // CONV_TT, chunk, 0)


def _conv_prompt(z, conv_w, conv_b, gn_g, gn_b):
    tc = CONV_TC
    nct = C_CONV // tc
    return pl.pallas_call(
        _conv_p_kernel,
        grid=(nct, BATCH + 1),
        in_specs=[pl.BlockSpec((SEQ, tc), lambda c, b: (_last_seq(b), c)),
                  pl.BlockSpec((SEQ, tc), lambda c, b: (_last_seq(b), c + nct)),
                  pl.BlockSpec((CONV_K, tc), lambda c, b: (0, c)),
                  pl.BlockSpec((1, tc), lambda c, b: (0, c)),
                  pl.BlockSpec((1, tc), lambda c, b: (0, c)),
                  pl.BlockSpec((1, tc), lambda c, b: (0, c))],
        out_specs=[pl.BlockSpec((SEQ, tc), lambda c, b: (b, c)),
                   pl.BlockSpec((1, CONV_K - 1, tc), lambda c, b: (_last_seq(b), 0, c))],
        out_shape=[jax.ShapeDtypeStruct((N_TOK, C_CONV), BF16),
                   jax.ShapeDtypeStruct((BATCH, CONV_K - 1, C_CONV), F32)],
        scratch_shapes=[pltpu.VMEM((CONV_PAD + SEQ + SUBLANES, tc), F32),
                        pltpu.VMEM((CONV_PAD, tc), F32)],
        compiler_params=_cparams(("arbitrary", "arbitrary"), 8 * SEQ * tc * 4),
        name="conv_prompt",
    )(z, z, conv_w, conv_b.reshape(1, -1), gn_g.reshape(1, -1), gn_b.reshape(1, -1))


def _log_sigmoid(x):
    return -(jnp.maximum(-x, 0.0) + jnp.log(1.0 + jnp.exp(-jnp.abs(x))))


def _lru_gates(xc, wa, ba, wi, bi, log_sig_lam):
    xb = xc.astype(BF16)
    r = _sigmoid(jnp.dot(xb, wa, preferred_element_type=F32) + ba)
    i = _sigmoid(jnp.dot(xb, wi, preferred_element_type=F32) + bi)
    a = jnp.exp(RG_C * r * log_sig_lam)
    return a, jnp.sqrt(1.0 - a * a) * (i * xc)


LRU_PAD = 8


def _lru_p_kernel(*refs):
    _prompt_or_zero(functools.partial(_lru_p_body, *refs), refs[9])


def _lru_p_body(lx_ref, lg_ref, cw_ref, cb_ref, wa_ref, ba_ref, wi_ref, bi_ref, lam_ref,
                y_ref, buf_ref, h_ref, xpad_ref, a_ref, b_ref, cwr_ref):
    t_len = lx_ref.shape[0]
    tc = lx_ref.shape[1]
    n_heads = tc // LRU_HD
    xpad_ref[0:LRU_PAD, :] = jnp.zeros((LRU_PAD, tc), F32)
    xpad_ref[LRU_PAD + t_len:LRU_PAD + t_len + SUBLANES, :] = jnp.zeros((SUBLANES, tc), F32)

    def copy_in(c, carry):
        t0 = pl.multiple_of(c * LRU_TT, LRU_TT)
        xpad_ref[pl.ds(LRU_PAD + t0, LRU_TT), :] = _bf16_round(lx_ref[pl.ds(t0, LRU_TT), :])
        return carry

    lax.fori_loop(0, t_len // LRU_TT, copy_in, 0)
    buf_ref[0] = lx_ref[t_len - (LRU_CONV_K - 1):t_len, :]

    cb = cb_ref[...]
    cwr_ref[0:LRU_CONV_K, :] = _bf16_round(cw_ref[...])
    log_sig_lam = _log_sigmoid(lam_ref[...])
    shift = LRU_PAD - (LRU_CONV_K - 1)

    def gates(c, carry):
        t0 = pl.multiple_of(c * LRU_TT, LRU_TT)
        xc = _causal_taps(xpad_ref, cwr_ref, t0, LRU_TT, shift, LRU_CONV_K,
                          jnp.zeros((LRU_TT, tc), F32) + cb)
        for hh in range(n_heads):
            sl = slice(hh * LRU_HD, (hh + 1) * LRU_HD)
            a, b = _lru_gates(xc[:, sl], wa_ref[hh].astype(BF16), ba_ref[:, sl],
                              wi_ref[hh].astype(BF16), bi_ref[:, sl], log_sig_lam[:, sl])
            a_ref[pl.ds(t0, LRU_TT), sl] = a
            b_ref[pl.ds(t0, LRU_TT), sl] = b
        return carry

    lax.fori_loop(0, t_len // LRU_TT, gates, 0)

    row = lax.broadcasted_iota(I32, (SUBLANES, tc), 0)

    def scan(i, h):
        t0 = pl.multiple_of(i * SUBLANES, SUBLANES)
        a = a_ref[pl.ds(t0, SUBLANES), :]
        b = b_ref[pl.ds(t0, SUBLANES), :]
        for d in (1, 2, 4):
            a_sh = pltpu.roll(a, d, axis=0)
            b_sh = pltpu.roll(b, d, axis=0)
            m = row >= d
            b = jnp.where(m, a * b_sh + b, b)
            a = jnp.where(m, a * a_sh, a)
        hb = a * h + b
        b_ref[pl.ds(t0, SUBLANES), :] = hb
        return hb[SUBLANES - 1:SUBLANES, :]

    h_last = lax.fori_loop(0, t_len // SUBLANES, scan, jnp.zeros((1, tc), F32), unroll=4)
    h_ref[0] = h_last

    def gate_out(c, carry):
        t0 = pl.multiple_of(c * LRU_TT, LRU_TT)
        y = b_ref[pl.ds(t0, LRU_TT), :] * _gelu_tanh(lg_ref[pl.ds(t0, LRU_TT), :])
        y_ref[pl.ds(t0, LRU_TT), :] = y.astype(y_ref.dtype)
        return carry

    lax.fori_loop(0, t_len // LRU_TT, gate_out, 0)


def _lru_prompt(z, cw, cb, wa, ba, wi, bi, lam):
    tc = CONV_TC
    nct = C_LRU // tc
    hpb = tc // LRU_HD
    col0 = 2 * C_CONV // tc
    vec = lambda v: v.reshape(1, -1)
    return pl.pallas_call(
        _lru_p_kernel,
        grid=(nct, BATCH + 1),
        in_specs=[pl.BlockSpec((SEQ, tc), lambda c, b: (_last_seq(b), c + col0)),
                  pl.BlockSpec((SEQ, tc), lambda c, b: (_last_seq(b), c + col0 + nct)),
                  pl.BlockSpec((LRU_CONV_K, tc), lambda c, b: (0, c)),
                  pl.BlockSpec((1, tc), lambda c, b: (0, c)),
                  pl.BlockSpec((hpb, LRU_HD, LRU_HD), lambda c, b: (c, 0, 0)),
                  pl.BlockSpec((1, tc), lambda c, b: (0, c)),
                  pl.BlockSpec((hpb, LRU_HD, LRU_HD), lambda c, b: (c, 0, 0)),
                  pl.BlockSpec((1, tc), lambda c, b: (0, c)),
                  pl.BlockSpec((1, tc), lambda c, b: (0, c))],
        out_specs=[pl.BlockSpec((SEQ, tc), lambda c, b: (b, c)),
                   pl.BlockSpec((1, LRU_CONV_K - 1, tc), lambda c, b: (_last_seq(b), 0, c)),
                   pl.BlockSpec((1, 1, tc), lambda c, b: (_last_seq(b), 0, c))],
        out_shape=[jax.ShapeDtypeStruct((N_TOK, C_LRU), BF16),
                   jax.ShapeDtypeStruct((BATCH, LRU_CONV_K - 1, C_LRU), F32),
                   jax.ShapeDtypeStruct((BATCH, 1, C_LRU), F32)],
        scratch_shapes=[pltpu.VMEM((LRU_PAD + SEQ + SUBLANES, tc), F32),
                        pltpu.VMEM((SEQ, tc), F32),
                        pltpu.VMEM((SEQ, tc), F32),
                        pltpu.VMEM((LRU_PAD, tc), F32)],
        compiler_params=_cparams(("arbitrary", "arbitrary"), 12 * SEQ * tc * 4),
        name="lru_prompt",
    )(z, z, cw, vec(cb), wa, vec(ba), wi, vec(bi), vec(lam))


def _mixer_s_kernel(cv_ref, cg_ref, lx_ref, lg_ref, st_ref, lst_ref, h0_ref,
                    cw_ref, cb_ref, gng_ref, gnb_ref, lcw_ref, lcb_ref,
                    wa_ref, ba_ref, wi_ref, bi_ref, lam_ref, yc_in, yl_in,
                    yc_ref, yl_ref, nst_ref, nlst_ref, nh_ref):
    del yc_in, yl_in
    u = cv_ref[...] * _sigmoid(cg_ref[...])
    acc = u * cw_ref[CONV_K - 1:CONV_K, :] + cb_ref[...]
    for k in range(CONV_K - 1):
        row = st_ref[k]
        acc = acc + row * cw_ref[k:k + 1, :]
        if k > 0:
            nst_ref[k - 1] = row
    nst_ref[CONV_K - 2] = u

    def store_c(sl, v):
        yc_ref[:, sl] = v.astype(yc_ref.dtype)

    _group_norm_silu(acc, gng_ref[...], gnb_ref[...], store_c)

    lx = lx_ref[...]
    xc = lx * lcw_ref[LRU_CONV_K - 1:LRU_CONV_K, :] + lcb_ref[...]
    for k in range(LRU_CONV_K - 1):
        row = lst_ref[k]
        xc = xc + row * lcw_ref[k:k + 1, :]
        if k > 0:
            nlst_ref[k - 1] = row
    nlst_ref[LRU_CONV_K - 2] = lx

    log_sig_lam = _log_sigmoid(lam_ref[...])
    for hh in range(LRU_HEADS):
        sl = slice(hh * LRU_HD, (hh + 1) * LRU_HD)
        a, b = _lru_gates(xc[:, sl], wa_ref[hh].astype(BF16), ba_ref[:, sl],
                          wi_ref[hh].astype(BF16), bi_ref[:, sl], log_sig_lam[:, sl])
        h = a * h0_ref[:, sl] + b
        nh_ref[:, sl] = h
        yl_ref[:, sl] = (h * _gelu_tanh(lg_ref[:, sl])).astype(yl_ref.dtype)


def _mixer_sample(z, st, lst, h0, layer, yc, yl, cw, cb, gn_g, gn_b, lcw, lcb, wa, ba, wi, bi, lam):
    bt = S_BT
    rb0 = N_P // bt
    vec = lambda v: v.reshape(1, -1)
    zspec = lambda col: pl.BlockSpec((bt, C_CONV), lambda i: (i + rb0, col))
    full = lambda shape: pl.BlockSpec(shape, lambda i: (0,) * len(shape))
    any_spec = pl.BlockSpec(memory_space=pl.ANY)
    return pl.pallas_call(
        _mixer_s_kernel,
        grid=(N_S // bt,),
        in_specs=[zspec(0), zspec(1), zspec(2), zspec(3),
                  pl.BlockSpec((None, CONV_K - 1, bt, C_CONV), lambda i: (layer, 0, i, 0)),
                  pl.BlockSpec((None, LRU_CONV_K - 1, bt, C_LRU), lambda i: (layer, 0, i, 0)),
                  pl.BlockSpec((None, bt, C_LRU), lambda i: (layer, i, 0)),
                  full((CONV_K, C_CONV)), full((1, C_CONV)), full((1, C_CONV)), full((1, C_CONV)),
                  full((LRU_CONV_K, C_LRU)), full((1, C_LRU)),
                  full((LRU_HEADS, LRU_HD, LRU_HD)), full((1, C_LRU)),
                  full((LRU_HEADS, LRU_HD, LRU_HD)), full((1, C_LRU)), full((1, C_LRU)),
                  any_spec, any_spec],
        out_specs=[pl.BlockSpec((bt, C_CONV), lambda i: (i + rb0, 0)),
                   pl.BlockSpec((bt, C_LRU), lambda i: (i + rb0, 0)),
                   pl.BlockSpec((CONV_K - 1, bt, C_CONV), lambda i: (0, i, 0)),
                   pl.BlockSpec((LRU_CONV_K - 1, bt, C_LRU), lambda i: (0, i, 0)),
                   pl.BlockSpec((bt, C_LRU), lambda i: (i, 0))],
        out_shape=[jax.ShapeDtypeStruct(yc.shape, yc.dtype),
                   jax.ShapeDtypeStruct(yl.shape, yl.dtype),
                   jax.ShapeDtypeStruct(st.shape[1:], F32),
                   jax.ShapeDtypeStruct(lst.shape[1:], F32),
                   jax.ShapeDtypeStruct(h0.shape[1:], F32)],
        input_output_aliases={18: 0, 19: 1},
        compiler_params=_cparams(("arbitrary",), 48 << 20),
        name="mixer_sample",
    )(z, z, z, z, st, lst, h0, cw, vec(cb), vec(gn_g), vec(gn_b), lcw, vec(lcb),
      wa, vec(ba), wi, vec(bi), vec(lam), yc, yl)


def _attn_p_kernel(q_ref, k_ref, v_ref, o_ref):
    i = pl.program_id(0)

    @pl.when(i < N_P // ATT_TQ)
    def _():
        _attn_p_body(q_ref, k_ref, v_ref, o_ref)

    @pl.when(i == N_P // ATT_TQ)
    def _():
        o_ref[...] = jnp.zeros(o_ref.shape, o_ref.dtype)


def _attn_p_body(q_ref, k_ref, v_ref, o_ref):
    scale = MEM_HD ** -0.5
    for h in range(MEM_HEADS):
        sl = slice(h * MEM_HD, (h + 1) * MEM_HD)
        q = q_ref[:, sl].astype(BF16)
        k = k_ref[:, sl].astype(BF16)
        v = v_ref[:, sl].astype(BF16)
        s = lax.dot_general(q, k, (((1,), (1,)), ((), ())), preferred_element_type=F32) * scale
        p = jnp.exp(s - jnp.max(s, axis=-1, keepdims=True))
        pr = p / jnp.sum(p, axis=-1, keepdims=True)
        o = jnp.dot(pr.astype(BF16), v, preferred_element_type=F32)
        o_ref[:, sl] = o.astype(o_ref.dtype)


def _attn_prompt(q, k, v):
    nq = SEQ // ATT_TQ
    return pl.pallas_call(
        _attn_p_kernel,
        grid=(BATCH * nq + 1,),
        in_specs=[pl.BlockSpec((ATT_TQ, MEM_W), lambda i: (i, 0)),
                  pl.BlockSpec((MEM_LEN, MEM_W), lambda i: (_last_seq(i // nq), 0)),
                  pl.BlockSpec((MEM_LEN, MEM_W), lambda i: (_last_seq(i // nq), 0))],
        out_specs=pl.BlockSpec((ATT_TQ, MEM_W), lambda i: (i, 0)),
        out_shape=jax.ShapeDtypeStruct((N_TOK, MEM_W), BF16),
        compiler_params=_cparams(("arbitrary",), 32 << 20),
        name="attn_prompt",
    )(q, k, v)


def _attn_s_kernel(q_ref, k_ref, v_ref, o_in, o_ref, stage_ref):
    del o_in
    jj = pl.program_id(1)
    scale = MEM_HD ** -0.5
    for bb in range(ATT_S_BT):
        r = jj * ATT_S_BT + bb
        q = q_ref[pl.ds(r, 1)]
        s = jnp.sum(k_ref[bb] * q, axis=-1, keepdims=True) * scale
        p = jnp.exp(s - jnp.max(s, axis=0, keepdims=True))
        l = jnp.sum(p, axis=0)
        o = jnp.sum(p * v_ref[bb], axis=0) / l
        for h in range(MEM_HEADS):
            stage_ref[pl.ds(r, 1), h * MEM_HD:(h + 1) * MEM_HD] = o[h:h + 1, :]

    @pl.when(jj == pl.num_programs(1) - 1)
    def _():
        o_ref[...] = stage_ref[...].astype(o_ref.dtype)


def _attn_sample(q4, kc, vc, layer, o):
    bt = S_BT
    inner = bt // ATT_S_BT
    rb0 = N_P // bt
    cache_spec = pl.BlockSpec((None, ATT_S_BT, MEM_LEN, MEM_HEADS, MEM_HD),
                              lambda i, j: (layer, i * inner + j, 0, 0, 0))
    return pl.pallas_call(
        _attn_s_kernel,
        grid=(N_S // bt, inner),
        in_specs=[pl.BlockSpec((bt, MEM_HEADS, MEM_HD), lambda i, j: (i, 0, 0)),
                  cache_spec, cache_spec,
                  pl.BlockSpec(memory_space=pl.ANY)],
        out_specs=pl.BlockSpec((bt, MEM_W), lambda i, j: (i + rb0, 0)),
        out_shape=jax.ShapeDtypeStruct(o.shape, o.dtype),
        scratch_shapes=[pltpu.VMEM((bt, MEM_W), F32)],
        input_output_aliases={3: 0},
        compiler_params=_cparams(("arbitrary", "arbitrary"), 48 << 20),
        name="attn_sample",
    )(q4, kc, vc, o)


def _norm_router_kernel(x_ref, g_ref, wr_ref, br_ref, xp_ref, ids_ref, gates_ref):
    xn = _rms(x_ref[...], g_ref[...])
    xp_ref[...] = pltpu.pack_elementwise([xn[:, :HALF], xn[:, HALF:]], packed_dtype=BF16)
    logits = jnp.dot(xn.astype(BF16), wr_ref[...], preferred_element_type=F32) + br_ref[...]
    lane = lax.broadcasted_iota(I32, logits.shape, 1)
    neg = jnp.float32(-jnp.inf)

    def first_max(vals):
        m = jnp.max(vals, axis=-1, keepdims=True)
        idx = jnp.min(jnp.where(vals == m, lane, ROUTER_W), axis=-1, keepdims=True)
        return m, idx

    is_group = lane < N_GROUPS
    g_max, g_sel = first_max(jnp.where(is_group, logits, neg))
    p_g = 1.0 / jnp.sum(jnp.where(is_group, jnp.exp(logits - g_max), 0.0), axis=-1, keepdims=True)
    lo = N_GROUPS + EXPERTS_PER_GROUP * g_sel
    in_group = (lane >= lo) & (lane < lo + EXPERTS_PER_GROUP)
    e_logits = jnp.where(in_group, logits, neg)
    t1, i1 = first_max(e_logits)
    t2, i2 = first_max(jnp.where(lane == i1, neg, e_logits))
    e = jnp.exp(t2 - t1)
    w1 = p_g / (1.0 + e)
    w2 = p_g * e / (1.0 + e)
    ids_ref[...] = jnp.where(lane == 0, i1 - N_GROUPS, jnp.where(lane == 1, i2 - N_GROUPS, 0))
    gates_ref[...] = jnp.where(lane == 0, w1, jnp.where(lane == 1, w2, 0.0))


def _norm_router(x, g, wr, br):
    tm = TM_NORM
    return pl.pallas_call(
        _norm_router_kernel,
        grid=(N_TOK // tm,),
        in_specs=[pl.BlockSpec((tm, D_MODEL), lambda i: (i, 0)),
                  pl.BlockSpec((1, D_MODEL), lambda i: (0, 0)),
                  pl.BlockSpec((D_MODEL, ROUTER_W), lambda i: (0, 0)),
                  pl.BlockSpec((1, ROUTER_W), lambda i: (0, 0))],
        out_specs=[pl.BlockSpec((tm, HALF), lambda i: (i, 0)),
                   pl.BlockSpec((tm, ROUTER_W), lambda i: (i, 0)),
                   pl.BlockSpec((tm, ROUTER_W), lambda i: (i, 0))],
        out_shape=[jax.ShapeDtypeStruct((N_TOK, HALF), jnp.uint32),
                   jax.ShapeDtypeStruct((N_TOK, ROUTER_W), I32),
                   jax.ShapeDtypeStruct((N_TOK, ROUTER_W), F32)],
        compiler_params=_cparams(("arbitrary",), 8 * tm * D_MODEL * 4),
        name="norm_router",
    )(x, g.reshape(1, -1), wr, br)


def _unpack_x(xp):
    lo = pltpu.unpack_elementwise(xp, index=0, packed_dtype=BF16, unpacked_dtype=F32)
    hi = pltpu.unpack_elementwise(xp, index=1, packed_dtype=BF16, unpacked_dtype=F32)
    return lo.astype(BF16), hi.astype(BF16)


def _ffn_kernel(sbe_ref, sbb_ref, sbn_ref, sbz_ref, tok_ref, xp_hbm, wg_ref, wu_ref, wd_ref, ys_hbm,
                xbuf_ref, acc_ref, xsem, ysem, pend_ref):
    del sbe_ref
    sb = pl.program_id(0)
    j = pl.program_id(1)
    nj = pl.num_programs(1)
    slot = sb % 2

    @pl.when((sb == 0) & (j == 0))
    def _():
        pend_ref[0] = 0

    def row_copy(s, slot_, r):
        tok = tok_ref[sbb_ref[s] * MOE_BLOCK + r]
        return pltpu.make_async_copy(xp_hbm.at[pl.ds(tok, 1), :],
                                     xbuf_ref.at[slot_, pl.ds(r, 1), :], xsem.at[slot_])

    def start_blocks(s, slot_, lo, hi):
        def block(i, carry):
            def body(r, c):
                row_copy(s, slot_, i * MOE_BLOCK + r).start(priority=GATHER_PRIORITY)
                return c

            return lax.fori_loop(0, MOE_BLOCK, body, carry, unroll=ISSUE_UNROLL)

        lax.fori_loop(lo, jnp.minimum(hi, sbn_ref[s]), block, 0)

    def wait_rows(s, slot_):
        def block(i, carry):
            r0 = pl.multiple_of(i * MOE_BLOCK, MOE_BLOCK)
            pltpu.make_async_copy(xp_hbm.at[pl.ds(0, MOE_BLOCK), :],
                                  xbuf_ref.at[slot_, pl.ds(r0, MOE_BLOCK), :], xsem.at[slot_]).wait()
            return carry

        lax.fori_loop(0, sbn_ref[s], block, 0)

    def out_copy(i, blk):
        r0 = pl.multiple_of(i * MOE_BLOCK, MOE_BLOCK)
        r1 = pl.multiple_of(blk * MOE_BLOCK, MOE_BLOCK)
        return pltpu.make_async_copy(acc_ref.at[pl.ds(r0, MOE_BLOCK), :],
                                     ys_hbm.at[pl.ds(r1, MOE_BLOCK), :], ysem)

    def wait_out():
        def body(i, carry):
            out_copy(0, 0).wait()
            return carry

        lax.fori_loop(0, pend_ref[0], body, 0)
        pend_ref[0] = 0

    per_j = -(-NSUB // (D_EXPERT // TF))

    @pl.when((sb == 0) & (j == 0))
    def _():
        start_blocks(sb, slot, 0, NSUB)

    @pl.when(sb + 1 < pl.num_programs(0))
    def _():
        start_blocks(sb + 1, 1 - slot, j * per_j, (j + 1) * per_j)

    nsub = sbn_ref[sb]
    nzero = sbz_ref[sb]

    def zero_block(i, carry):
        r0 = pl.multiple_of(i * MOE_BLOCK, MOE_BLOCK)
        acc_ref[pl.ds(r0, MOE_BLOCK), :] = jnp.zeros((MOE_BLOCK, D_MODEL), F32)
        return carry

    def start_out(i, carry):
        out_copy(i, sbb_ref[sb] + i).start()
        return carry

    @pl.when(j == 0)
    def _():
        wait_rows(sb, slot)
        wait_out()
        lax.fori_loop(0, jnp.maximum(nsub, nzero), zero_block, 0)
        lax.fori_loop(0, nzero, start_out, 0)
        pend_ref[0] = nzero

    def ffn_rows(r0, rows):
        lo, hi = _unpack_x(xbuf_ref[slot, pl.ds(r0, rows), :])
        g = _dot(lo, wg_ref[:HALF, :]) + _dot(hi, wg_ref[HALF:, :])
        u = _dot(lo, wu_ref[:HALF, :]) + _dot(hi, wu_ref[HALF:, :])
        h = (_silu(g) * u).astype(BF16)
        acc_ref[pl.ds(r0, rows), :] += _dot(h, wd_ref[...])

    def pair(i, carry):
        ffn_rows(pl.multiple_of(i * (2 * MOE_BLOCK), 2 * MOE_BLOCK), 2 * MOE_BLOCK)
        return carry

    lax.fori_loop(0, nsub // 2, pair, 0)

    @pl.when(nsub % 2 == 1)
    def _():
        ffn_rows(pl.multiple_of((nsub - 1) * MOE_BLOCK, MOE_BLOCK), MOE_BLOCK)

    @pl.when((j == nj - 1) & (nsub > 0))
    def _():
        lax.fori_loop(0, nsub, start_out, 0)
        pend_ref[0] = nsub

    @pl.when((sb == pl.num_programs(0) - 1) & (j == nj - 1))
    def _():
        wait_out()


def _moe_ffn(xp, plan, w_gate, w_up, w_down, layer):
    nj = D_EXPERT // TF

    def hidden_slice(sb, j, sbn):
        return jnp.where(sbn[sb] > 0, j, nj - 1)

    up_spec = pl.BlockSpec((None, None, D_MODEL, TF),
                           lambda sb, j, sbe, sbb, sbn, sbz, tok: (layer, sbe[sb], 0, hidden_slice(sb, j, sbn)))
    down_spec = pl.BlockSpec((None, None, TF, D_MODEL),
                             lambda sb, j, sbe, sbb, sbn, sbz, tok: (layer, sbe[sb], hidden_slice(sb, j, sbn), 0))
    return pl.pallas_call(
        _ffn_kernel,
        grid_spec=pltpu.PrefetchScalarGridSpec(
            num_scalar_prefetch=5,
            grid=(N_SB, nj),
            in_specs=[pl.BlockSpec(memory_space=pl.ANY), up_spec, up_spec, down_spec],
            out_specs=pl.BlockSpec(memory_space=pl.ANY),
            scratch_shapes=[pltpu.VMEM((2, SB_ROWS, HALF), jnp.uint32),
                            pltpu.VMEM((SB_ROWS, D_MODEL), F32),
                            pltpu.SemaphoreType.DMA((2,)),
                            pltpu.SemaphoreType.DMA(()),
                            pltpu.SMEM((1,), I32)]),
        out_shape=jax.ShapeDtypeStruct((N_ROWS, D_MODEL), F32),
        compiler_params=_cparams(("arbitrary", "arbitrary"), VMEM_CAP),
        name="moe_ffn",
    )(plan["sb_e"], plan["sb_blk"], plan["sb_n"], plan["sb_z"], plan["row_tok"], xp, w_gate, w_up, w_down)


def _combine_kernel(dest_ref, x_ref, gates_ref, g_ref, ys_hbm, out_a, out_b, buf_ref, sem, *, final):
    i = pl.program_id(0)
    n = pl.num_programs(0)

    def row_copy(step, slot, t, k):
        s = (step * COMB_TM + t) * TOP_K + k
        return pltpu.make_async_copy(ys_hbm.at[pl.ds(dest_ref[s], 1), :],
                                     buf_ref.at[slot, k, pl.ds(t, 1), :], sem.at[slot])

    def start_all(step, slot):
        def body(t, carry):
            for k in range(TOP_K):
                row_copy(step, slot, t, k).start(priority=k % 2)
            return carry

        lax.fori_loop(0, COMB_TM, body, 0, unroll=ISSUE_UNROLL)

    def wait_all(step, slot):
        for k in range(TOP_K):
            pltpu.make_async_copy(ys_hbm.at[pl.ds(0, COMB_TM), :], buf_ref.at[slot, k], sem.at[slot]).wait()

    slot = i % 2

    @pl.when(i == 0)
    def _():
        start_all(i, slot)

    @pl.when(i + 1 < n)
    def _():
        start_all(i + 1, 1 - slot)

    wait_all(i, slot)
    gates = gates_ref[...]
    y = x_ref[...] + gates[:, 0:1] * buf_ref[slot, 0] + gates[:, 1:2] * buf_ref[slot, 1]
    yn = _rms(y, g_ref[...])
    if final:
        @pl.when(i < n - 1)
        def _():
            out_a[...] = yn

        @pl.when(i == n - 1)
        def _():
            out_b[...] = yn
    else:
        out_a[...] = y
        out_b[...] = yn.astype(out_b.dtype)


def _combine(x, ys, dest, gates, g_next, *, final):
    tm = COMB_TM
    assert N_S == tm
    n_p_blocks = N_P // tm
    if final:
        out_specs = [pl.BlockSpec((tm, D_MODEL), lambda i, d: (jnp.minimum(i, n_p_blocks - 1), 0)),
                     pl.BlockSpec((tm, D_MODEL), lambda i, d: (0, 0))]
        out_shape = [jax.ShapeDtypeStruct((N_P, D_MODEL), F32),
                     jax.ShapeDtypeStruct((N_S, D_MODEL), F32)]
    else:
        out_specs = [pl.BlockSpec((tm, D_MODEL), lambda i, d: (i, 0)),
                     pl.BlockSpec((tm, D_MODEL), lambda i, d: (i, 0))]
        out_shape = [jax.ShapeDtypeStruct((N_TOK, D_MODEL), F32),
                     jax.ShapeDtypeStruct((N_TOK, D_MODEL), BF16)]
    return pl.pallas_call(
        functools.partial(_combine_kernel, final=final),
        grid_spec=pltpu.PrefetchScalarGridSpec(
            num_scalar_prefetch=1,
            grid=(N_TOK // tm,),
            in_specs=[pl.BlockSpec((tm, D_MODEL), lambda i, d: (i, 0)),
                      pl.BlockSpec((tm, ROUTER_W), lambda i, d: (i, 0)),
                      pl.BlockSpec((1, D_MODEL), lambda i, d: (0, 0)),
                      pl.BlockSpec(memory_space=pl.ANY)],
            out_specs=out_specs,
            scratch_shapes=[pltpu.VMEM((2, TOP_K, tm, D_MODEL), F32),
                            pltpu.SemaphoreType.DMA((2,))]),
        out_shape=out_shape,
        compiler_params=_cparams(("arbitrary",), 40 << 20),
        name="moe_combine",
    )(dest, x, gates, g_next.reshape(1, -1), ys)


def _dispatch_plan(ids):
    flat_e = ids[:, :TOP_K].reshape(-1)
    onehot = (flat_e[:, None] == jnp.arange(N_EXPERTS, dtype=I32)[None, :]).astype(I32)
    chunks = onehot.reshape(N_SLOT // PLAN_CHUNK, PLAN_CHUNK, N_EXPERTS).astype(BF16)
    tri = jnp.tril(jnp.ones((PLAN_CHUNK, PLAN_CHUNK), BF16))
    within = jnp.einsum("ij,cjk->cik", tri, chunks, preferred_element_type=F32).astype(I32)
    totals = within[:, -1, :]
    before = jnp.cumsum(totals, axis=0) - totals
    csum = (within + before[:, None, :]).reshape(N_SLOT, N_EXPERTS)
    rank = jnp.sum(onehot * csum, axis=1) - 1
    counts = csum[-1]
    nblk = (counts + MOE_BLOCK - 1) // MOE_BLOCK
    bend = jnp.cumsum(nblk)
    bstart = bend - nblk
    dest = (jnp.sum(onehot * bstart[None, :], axis=1) * MOE_BLOCK + rank).astype(I32)
    row_tok = jnp.zeros((N_ROWS,), I32).at[dest].set(jnp.arange(N_SLOT, dtype=I32) // TOP_K)

    nsup = (nblk + NSUB - 1) // NSUB
    sup_end = jnp.cumsum(nsup)
    sup_start = sup_end - nsup
    total_sup = sup_end[-1]
    sb = jnp.arange(N_SB, dtype=I32)
    e_of = jnp.minimum(jnp.sum((sup_end[None, :] <= sb[:, None]).astype(I32), axis=1), N_EXPERTS - 1)
    q = sb - sup_start[e_of]
    real = sb < total_sup
    n_real = jnp.clip(nblk[e_of] - NSUB * q, 0, NSUB)
    blk_real = bstart[e_of] + NSUB * q
    blk_tail = bend[-1] + NSUB * (sb - total_sup)
    n_tail = jnp.clip(N_BLOCKS - blk_tail, 0, NSUB)
    e_last = e_of[total_sup - 1]
    return dict(
        dest=dest, row_tok=row_tok,
        sb_e=jnp.where(real, e_of, e_last).astype(I32),
        sb_blk=jnp.where(real, blk_real, jnp.minimum(blk_tail, N_BLOCKS - 1)).astype(I32),
        sb_n=jnp.where(real, n_real, 0).astype(I32),
        sb_z=jnp.where(real, 0, n_tail).astype(I32))


def kernel(x_prompt, x_sample, mem_prompt, state_conv, state_lru_conv, state_lru_h, cache_mem_k, cache_mem_v, norm_mix, w_in, conv_w, conv_b, conv_gn_g, conv_gn_b, lru_conv_w, lru_conv_b, lru_wa, lru_ba, lru_wi, lru_bi, lru_lambda, w_out, norm_attn, norm_mem_kv, w_q, w_k, w_v, w_o, norm_ffn, w_router_g, b_router_g, w_router_e, b_router_e, w_gate, w_up, w_down, norm_final):
    x = jnp.concatenate([x_prompt.reshape(N_P, D_MODEL), x_sample.reshape(N_S, D_MODEL)], axis=0)
    mem = mem_prompt.reshape(BATCH * MEM_LEN, D_MODEL)
    xn = _norm(x, norm_mix[0], tm=TM_NORM)

    state_conv_t = state_conv.transpose(0, 2, 1, 3)
    state_lru_conv_t = state_lru_conv.transpose(0, 2, 1, 3)
    conv_p, lruc_p, h_p, mk_p, mv_p, conv_s, lruc_s, h_s = ([] for _ in range(8))
    for l in range(DEPTH):
        z = _mm([xn], w_in, l, tn=1024, tm=TM, name="w_in")
        yc, cst = _conv_prompt(z, conv_w[l], conv_b[l], conv_gn_g[l], conv_gn_b[l])
        yl, lst, hl = _lru_prompt(z, lru_conv_w[l], lru_conv_b[l], lru_wa[l], lru_ba[l],
                                  lru_wi[l], lru_bi[l], lru_lambda[l])
        yc, yl, cst_s, lst_s, hl_s = _mixer_sample(
            z, state_conv_t, state_lru_conv_t, state_lru_h, l, yc, yl,
            conv_w[l], conv_b[l], conv_gn_g[l], conv_gn_b[l], lru_conv_w[l], lru_conv_b[l],
            lru_wa[l], lru_ba[l], lru_wi[l], lru_bi[l], lru_lambda[l])
        x = _mm([yc, yl], w_out, l, tn=1024, tm=TM_NORM, res=x, name="w_out")
        conv_p.append(cst); lruc_p.append(lst); h_p.append(hl.reshape(BATCH, C_LRU))
        conv_s.append(cst_s); lruc_s.append(lst_s); h_s.append(hl_s)

        q = _norm_mm(x, norm_attn[l], w_q, l, tm=TM_NORM, name="w_q")
        mn = _norm(mem, norm_mem_kv[l], tm=256)
        k_p = _mm([mn], w_k, l, tn=512, tm=BATCH * MEM_LEN, name="w_k")
        v_p = _mm([mn], w_v, l, tn=512, tm=BATCH * MEM_LEN, name="w_v")
        o = _attn_prompt(q, k_p, v_p)
        o = _attn_sample(q[N_P:].reshape(N_S, MEM_HEADS, MEM_HD), cache_mem_k, cache_mem_v, l, o)
        x = _mm([o], w_o, l, tn=1024, tm=TM, res=x, name="w_o")
        mk_p.append(k_p.reshape(BATCH, MEM_LEN, MEM_HEADS, MEM_HD))
        mv_p.append(v_p.reshape(BATCH, MEM_LEN, MEM_HEADS, MEM_HD))

        wr = jnp.concatenate([w_router_g[l], w_router_e[l].reshape(D_MODEL, N_EXPERTS),
                              jnp.zeros((D_MODEL, ROUTER_W - N_GROUPS - N_EXPERTS), F32)],
                             axis=1).astype(BF16)
        br = jnp.concatenate([b_router_g[l], b_router_e[l].reshape(N_EXPERTS),
                              jnp.zeros((ROUTER_W - N_GROUPS - N_EXPERTS,), F32)]).reshape(1, ROUTER_W)
        xp, ids, gates = _norm_router(x, norm_ffn[l], wr, br)
        plan = _dispatch_plan(ids)
        ys = _moe_ffn(xp, plan, w_gate, w_up, w_down, l)
        if l < DEPTH - 1:
            x, xn = _combine(x, ys, plan["dest"], gates, norm_mix[l + 1], final=False)
        else:
            y_p, y_s = _combine(x, ys, plan["dest"], gates, norm_final, final=True)

    y_prompt = y_p.reshape(BATCH, SEQ, D_MODEL)
    y_sample = y_s.reshape(DEC_BATCH, 1, D_MODEL)
    return (y_prompt, y_sample, jnp.stack(conv_p), jnp.stack(lruc_p), jnp.stack(h_p),
            jnp.stack(mk_p), jnp.stack(mv_p), jnp.stack(conv_s).transpose(0, 2, 1, 3),
            jnp.stack(lruc_s).transpose(0, 2, 1, 3), jnp.stack(h_s))
```

```python
import functools
import math

import jax
import jax.numpy as jnp
from jax import lax
from jax.experimental import pallas as pl
from jax.experimental.pallas import tpu as pltpu

F32 = jnp.float32
BF16 = jnp.bfloat16
I32 = jnp.int32

D_MODEL = 4096
BATCH = 4
SEQ = 2048
DEPTH = 2
DEC_BATCH = 128
C_CONV = D_MODEL // 2
C_LRU = D_MODEL // 2
D_IN = 2 * C_CONV + 2 * C_LRU
CONV_GROUPS = 16
GROUP_W = C_CONV // CONV_GROUPS
CONV_K = 31
LRU_HEADS = 16
LRU_HD = C_LRU // LRU_HEADS
LRU_CONV_K = 4
RG_C = 8.0
MEM_LEN = 256
MEM_HEADS = 4
MEM_HD = D_MODEL // 16
MEM_W = MEM_HEADS * MEM_HD
N_GROUPS = 4
EXPERTS_PER_GROUP = 8
N_EXPERTS = N_GROUPS * EXPERTS_PER_GROUP
TOP_K = 2
D_EXPERT = D_MODEL // 4
EPS = 1e-6

N_P = BATCH * SEQ
N_S = DEC_BATCH
N_TOK = N_P + N_S
N_SLOT = N_TOK * TOP_K

LANES = 128
SUBLANES = 8
VMEM_CAP = 56 * 1024 * 1024

TM = 1040
TM_NORM = 416
MOE_BLOCK = 128
N_BLOCKS = -(-(N_SLOT + N_EXPERTS * (MOE_BLOCK - 1)) // MOE_BLOCK)
N_ROWS = N_BLOCKS * MOE_BLOCK
TF = 256
ROUTER_W = LANES
HALF = D_MODEL // 2
CONV_TT = 64
CONV_UNROLL = 4
SCAN_UNROLL = 8
CONV_TC = 256
LRU_TT = 256
COMB_TM = 128
PLAN_CHUNK = 128
ATT_TQ = 512
S_BT = 16
ATT_S_BT = 4
ISSUE_UNROLL = 8
GATHER_PRIORITY = 1
NSUB = 6
SB_ROWS = NSUB * MOE_BLOCK
N_SB = (N_BLOCKS + (NSUB - 1) * (N_EXPERTS + 1)) // NSUB + 1


def _cparams(sem, vmem_bytes):
    limit = min(VMEM_CAP, max(32 * 1024 * 1024, int(vmem_bytes)))
    return pltpu.CompilerParams(dimension_semantics=sem, vmem_limit_bytes=limit)


def _sigmoid(x):
    return 1.0 / (1.0 + jnp.exp(-x))


def _silu(x):
    return x * _sigmoid(x)


def _gelu_tanh(x):
    c = math.sqrt(2.0 / math.pi)
    return 0.5 * x * (1.0 + jnp.tanh(c * (x + 0.044715 * (x * x * x))))


def _dot(a, w):
    return lax.dot_general(a, w, (((1,), (0,)), ((), ())), preferred_element_type=F32)


def _bf16_round(x):
    return x.astype(BF16).astype(F32)


def _rms(x, g):
    ms = jnp.mean(x * x, axis=-1, keepdims=True)
    return x * lax.rsqrt(ms + EPS) * g


def _norm_kernel(x_ref, g_ref, o_ref):
    o_ref[...] = _rms(x_ref[...], g_ref[...]).astype(o_ref.dtype)


def _norm(x, g, *, tm, out_dtype=BF16, row_block0=0, n_rows=None):
    n_rows = x.shape[0] if n_rows is None else n_rows
    d = x.shape[1]
    return pl.pallas_call(
        _norm_kernel,
        grid=(n_rows // tm,),
        in_specs=[pl.BlockSpec((tm, d), lambda i: (i + row_block0, 0)),
                  pl.BlockSpec((1, d), lambda i: (0, 0))],
        out_specs=pl.BlockSpec((tm, d), lambda i: (i, 0)),
        out_shape=jax.ShapeDtypeStruct((n_rows, d), out_dtype),
        compiler_params=_cparams(("arbitrary",), 6 * tm * d * 4),
        name="rmsnorm",
    )(x, g.reshape(1, d))


def _mm_kernel(*refs, n_a, has_res):
    a_refs = refs[:n_a]
    w_ref = refs[n_a]
    res_ref = refs[n_a + 1] if has_res else None
    o_ref = refs[n_a + 1 + int(has_res)]
    acc = None
    off = 0
    for a_ref in a_refs:
        k = a_ref.shape[1]
        part = lax.dot_general(a_ref[...].astype(BF16), w_ref[off:off + k, :], (((1,), (0,)), ((), ())),
                               preferred_element_type=F32)
        acc = part if acc is None else acc + part
        off += k
    if has_res:
        acc = acc + res_ref[...]
    o_ref[...] = acc.astype(o_ref.dtype)


def _mm(a_list, w, layer, *, tn, tm, res=None, out_dtype=F32, name="proj"):
    m = a_list[0].shape[0]
    _, k, n = w.shape
    assert sum(a.shape[1] for a in a_list) == k
    in_specs = [pl.BlockSpec((tm, a.shape[1]), lambda j, i: (i, 0)) for a in a_list]
    in_specs.append(pl.BlockSpec((None, k, tn), lambda j, i: (layer, 0, j), pipeline_mode=pl.Buffered(1)))
    args = list(a_list) + [w]
    if res is not None:
        in_specs.append(pl.BlockSpec((tm, tn), lambda j, i: (i, j)))
        args.append(res)
    a_bytes = sum(a.dtype.itemsize * a.shape[1] for a in a_list) * tm
    vmem = 2 * a_bytes + k * tn * 4 + 6 * tm * tn * 4 + (4 << 20)
    return pl.pallas_call(
        functools.partial(_mm_kernel, n_a=len(a_list), has_res=res is not None),
        grid=(n // tn, m // tm),
        in_specs=in_specs,
        out_specs=pl.BlockSpec((tm, tn), lambda j, i: (i, j)),
        out_shape=jax.ShapeDtypeStruct((m, n), out_dtype),
        compiler_params=_cparams(("arbitrary", "arbitrary"), vmem),
        name=name,
    )(*args)


def _norm_mm_kernel(x_ref, g_ref, w_ref, o_ref):
    o_ref[...] = _dot(_rms(x_ref[...], g_ref[...]).astype(BF16), w_ref[...])


def _norm_mm(x, g, w, layer, *, tm, name):
    m, d = x.shape
    _, k, n = w.shape
    return pl.pallas_call(
        _norm_mm_kernel,
        grid=(m // tm,),
        in_specs=[pl.BlockSpec((tm, d), lambda i: (i, 0)),
                  pl.BlockSpec((1, d), lambda i: (0, 0)),
                  pl.BlockSpec((None, k, n), lambda i: (layer, 0, 0), pipeline_mode=pl.Buffered(1))],
        out_specs=pl.BlockSpec((tm, n), lambda i: (i, 0)),
        out_shape=jax.ShapeDtypeStruct((m, n), F32),
        compiler_params=_cparams(("arbitrary",), VMEM_CAP),
        name=name,
    )(x, g.reshape(1, d), w)


def _group_norm_silu(y, gn_g, gn_b, store):
    for g in range(y.shape[1] // GROUP_W):
        sl = slice(g * GROUP_W, (g + 1) * GROUP_W)
        yg = y[:, sl]
        mu = jnp.mean(yg, axis=-1, keepdims=True)
        d = yg - mu
        var = jnp.mean(d * d, axis=-1, keepdims=True)
        yn = d * lax.rsqrt(var + EPS) * gn_g[:, sl] + gn_b[:, sl]
        store(sl, _silu(yn))


CONV_PAD = 32


def _causal_taps(src_ref, w_ref, t0, tt, first_off, n_taps, init):
    acc = init
    for s in range(SUBLANES):
        part = None
        for k in range(n_taps):
            off = first_off + k
            if off % SUBLANES != s:
                continue
            base = pl.multiple_of(t0 + (off - s), SUBLANES)
            term = src_ref[pl.ds(base, tt + SUBLANES), :] * w_ref[k:k + 1, :]
            part = term if part is None else part + term
        if part is not None:
            acc = acc + part[s:s + tt, :]
    return acc


def _last_seq(b):
    return jnp.minimum(b, BATCH - 1)


def _prompt_or_zero(body, y_ref):
    b = pl.program_id(1)

    @pl.when(b < BATCH)
    def _():
        body()

    @pl.when(b == BATCH)
    def _():
        y_ref[...] = jnp.zeros(y_ref.shape, y_ref.dtype)


def _conv_p_kernel(*refs):
    _prompt_or_zero(functools.partial(_conv_p_body, *refs), refs[6])


def _conv_p_body(cv_ref, cg_ref, w_ref, b_ref, gng_ref, gnb_ref, y_ref, st_ref, upad_ref, wr_ref):
    t_len = cv_ref.shape[0]
    tc = cv_ref.shape[1]
    wr_ref[0:CONV_K, :] = _bf16_round(w_ref[...])
    upad_ref[0:CONV_PAD, :] = jnp.zeros((CONV_PAD, tc), F32)
    upad_ref[CONV_PAD + t_len:CONV_PAD + t_len + SUBLANES, :] = jnp.zeros((SUBLANES, tc), F32)

    def glu(c, carry):
        t0 = pl.multiple_of(c * LRU_TT, LRU_TT)
        u = cv_ref[pl.ds(t0, LRU_TT), :] * _sigmoid(cg_ref[pl.ds(t0, LRU_TT), :])
        upad_ref[pl.ds(CONV_PAD + t0, LRU_TT), :] = _bf16_round(u)
        return carry

    lax.fori_loop(0, t_len // LRU_TT, glu, 0)
    tail = slice(t_len - (CONV_K - 1), t_len)
    st_ref[0] = cv_ref[tail, :] * _sigmoid(cg_ref[tail, :])

    bias = b_ref[...]
    gn_g = gng_ref[...]
    gn_b = gnb_ref[...]
    shift = CONV_PAD - (CONV_K - 1)

    def chunk(c, carry):
        t0 = pl.multiple_of(c * CONV_TT, CONV_TT)
        acc = _causal_taps(upad_ref, wr_ref, t0, CONV_TT, shift, CONV_K,
                           jnp.zeros((CONV_TT, tc), F32) + bias)

        def store(sl, v):
            y_ref[pl.ds(t0, CONV_TT), sl] = v.astype(y_ref.dtype)

        _group_norm_silu(acc, gn_g, gn_b, store)
        return carry

    lax.fori_loop(0, t_len // CONV_TT, chunk, 0, unroll=CONV_UNROLL)


def _conv_prompt(z, conv_w, conv_b, gn_g, gn_b):
    tc = CONV_TC
    nct = C_CONV // tc
    return pl.pallas_call(
        _conv_p_kernel,
        grid=(nct, BATCH + 1),
        in_specs=[pl.BlockSpec((SEQ, tc), lambda c, b: (_last_seq(b), c)),
                  pl.BlockSpec((SEQ, tc), lambda c, b: (_last_seq(b), c + nct)),
                  pl.BlockSpec((CONV_K, tc), lambda c, b: (0, c)),
                  pl.BlockSpec((1, tc), lambda c, b: (0, c)),
                  pl.BlockSpec((1, tc), lambda c, b: (0, c)),
                  pl.BlockSpec((1, tc), lambda c, b: (0, c))],
        out_specs=[pl.BlockSpec((SEQ, tc), lambda c, b: (b, c)),
                   pl.BlockSpec((1, CONV_K - 1, tc), lambda c, b: (_last_seq(b), 0, c))],
        out_shape=[jax.ShapeDtypeStruct((N_TOK, C_CONV), BF16),
                   jax.ShapeDtypeStruct((BATCH, CONV_K - 1, C_CONV), F32)],
        scratch_shapes=[pltpu.VMEM((CONV_PAD + SEQ + SUBLANES, tc), F32),
                        pltpu.VMEM((CONV_PAD, tc), F32)],
        compiler_params=_cparams(("arbitrary", "arbitrary"), 8 * SEQ * tc * 4),
        name="conv_prompt",
    )(z, z, conv_w, conv_b.reshape(1, -1), gn_g.reshape(1, -1), gn_b.reshape(1, -1))


def _log_sigmoid(x):
    return -(jnp.maximum(-x, 0.0) + jnp.log(1.0 + jnp.exp(-jnp.abs(x))))


def _lru_gates(xc, wa, ba, wi, bi, log_sig_lam):
    xb = xc.astype(BF16)
    r = _sigmoid(jnp.dot(xb, wa, preferred_element_type=F32) + ba)
    i = _sigmoid(jnp.dot(xb, wi, preferred_element_type=F32) + bi)
    a = jnp.exp(RG_C * r * log_sig_lam)
    return a, jnp.sqrt(1.0 - a * a) * (i * xc)


LRU_PAD = 8


def _lru_p_kernel(*refs):
    _prompt_or_zero(functools.partial(_lru_p_body, *refs), refs[9])


def _lru_p_body(lx_ref, lg_ref, cw_ref, cb_ref, wa_ref, ba_ref, wi_ref, bi_ref, lam_ref,
                y_ref, buf_ref, h_ref, xpad_ref, a_ref, b_ref, cwr_ref, hs_ref):
    t_len = lx_ref.shape[0]
    tc = lx_ref.shape[1]
    n_heads = tc // LRU_HD
    xpad_ref[0:LRU_PAD, :] = jnp.zeros((LRU_PAD, tc), F32)
    xpad_ref[LRU_PAD + t_len:LRU_PAD + t_len + SUBLANES, :] = jnp.zeros((SUBLANES, tc), F32)

    def copy_in(c, carry):
        t0 = pl.multiple_of(c * LRU_TT, LRU_TT)
        xpad_ref[pl.ds(LRU_PAD + t0, LRU_TT), :] = _bf16_round(lx_ref[pl.ds(t0, LRU_TT), :])
        return carry

    lax.fori_loop(0, t_len // LRU_TT, copy_in, 0)
    buf_ref[0] = lx_ref[t_len - (LRU_CONV_K - 1):t_len, :]

    cb = cb_ref[...]
    cwr_ref[0:LRU_CONV_K, :] = _bf16_round(cw_ref[...])
    log_sig_lam = _log_sigmoid(lam_ref[...])
    shift = LRU_PAD - (LRU_CONV_K - 1)

    def gates(c, carry):
        t0 = pl.multiple_of(c * LRU_TT, LRU_TT)
        xc = _causal_taps(xpad_ref, cwr_ref, t0, LRU_TT, shift, LRU_CONV_K,
                          jnp.zeros((LRU_TT, tc), F32) + cb)
        for hh in range(n_heads):
            sl = slice(hh * LRU_HD, (hh + 1) * LRU_HD)
            a, b = _lru_gates(xc[:, sl], wa_ref[hh].astype(BF16), ba_ref[:, sl],
                              wi_ref[hh].astype(BF16), bi_ref[:, sl], log_sig_lam[:, sl])
            a_ref[pl.ds(t0, LRU_TT), sl] = a
            b_ref[pl.ds(t0, LRU_TT), sl] = b
        return carry

    lax.fori_loop(0, t_len // LRU_TT, gates, 0)

    row = lax.broadcasted_iota(I32, (SUBLANES, tc), 0)

    def scan(i, h):
        t0 = pl.multiple_of(i * SUBLANES, SUBLANES)
        a = a_ref[pl.ds(t0, SUBLANES), :]
        b = b_ref[pl.ds(t0, SUBLANES), :]
        for d in (1, 2, 4):
            a_sh = pltpu.roll(a, d, axis=0)
            b_sh = pltpu.roll(b, d, axis=0)
            m = row >= d
            b = jnp.where(m, a * b_sh + b, b)
            a = jnp.where(m, a * a_sh, a)
        hs_ref[pl.ds(t0, SUBLANES), :] = a * h + b
        return a[SUBLANES - 1:SUBLANES, :] * h + b[SUBLANES - 1:SUBLANES, :]

    h_last = lax.fori_loop(0, t_len // SUBLANES, scan, jnp.zeros((1, tc), F32), unroll=SCAN_UNROLL)
    h_ref[0] = h_last

    def gate_out(c, carry):
        t0 = pl.multiple_of(c * LRU_TT, LRU_TT)
        y = hs_ref[pl.ds(t0, LRU_TT), :] * _gelu_tanh(lg_ref[pl.ds(t0, LRU_TT), :])
        y_ref[pl.ds(t0, LRU_TT), :] = y.astype(y_ref.dtype)
        return carry

    lax.fori_loop(0, t_len // LRU_TT, gate_out, 0)


def _lru_prompt(z, cw, cb, wa, ba, wi, bi, lam):
    tc = CONV_TC
    nct = C_LRU // tc
    hpb = tc // LRU_HD
    col0 = 2 * C_CONV // tc
    vec = lambda v: v.reshape(1, -1)
    return pl.pallas_call(
        _lru_p_kernel,
        grid=(nct, BATCH + 1),
        in_specs=[pl.BlockSpec((SEQ, tc), lambda c, b: (_last_seq(b), c + col0)),
                  pl.BlockSpec((SEQ, tc), lambda c, b: (_last_seq(b), c + col0 + nct)),
                  pl.BlockSpec((LRU_CONV_K, tc), lambda c, b: (0, c)),
                  pl.BlockSpec((1, tc), lambda c, b: (0, c)),
                  pl.BlockSpec((hpb, LRU_HD, LRU_HD), lambda c, b: (c, 0, 0)),
                  pl.BlockSpec((1, tc), lambda c, b: (0, c)),
                  pl.BlockSpec((hpb, LRU_HD, LRU_HD), lambda c, b: (c, 0, 0)),
                  pl.BlockSpec((1, tc), lambda c, b: (0, c)),
                  pl.BlockSpec((1, tc), lambda c, b: (0, c))],
        out_specs=[pl.BlockSpec((SEQ, tc), lambda c, b: (b, c)),
                   pl.BlockSpec((1, LRU_CONV_K - 1, tc), lambda c, b: (_last_seq(b), 0, c)),
                   pl.BlockSpec((1, 1, tc), lambda c, b: (_last_seq(b), 0, c))],
        out_shape=[jax.ShapeDtypeStruct((N_TOK, C_LRU), BF16),
                   jax.ShapeDtypeStruct((BATCH, LRU_CONV_K - 1, C_LRU), F32),
                   jax.ShapeDtypeStruct((BATCH, 1, C_LRU), F32)],
        scratch_shapes=[pltpu.VMEM((LRU_PAD + SEQ + SUBLANES, tc), F32),
                        pltpu.VMEM((SEQ, tc), F32),
                        pltpu.VMEM((SEQ, tc), F32),
                        pltpu.VMEM((LRU_PAD, tc), F32),
                        pltpu.VMEM((SEQ, tc), F32)],
        compiler_params=_cparams(("arbitrary", "arbitrary"), 12 * SEQ * tc * 4),
        name="lru_prompt",
    )(z, z, cw, vec(cb), wa, vec(ba), wi, vec(bi), vec(lam))


def _mixer_s_kernel(cv_ref, cg_ref, lx_ref, lg_ref, st_ref, lst_ref, h0_ref,
                    cw_ref, cb_ref, gng_ref, gnb_ref, lcw_ref, lcb_ref,
                    wa_ref, ba_ref, wi_ref, bi_ref, lam_ref, yc_in, yl_in,
                    yc_ref, yl_ref, nst_ref, nlst_ref, nh_ref):
    del yc_in, yl_in
    u = cv_ref[...] * _sigmoid(cg_ref[...])
    acc = u * cw_ref[CONV_K - 1:CONV_K, :] + cb_ref[...]
    for k in range(CONV_K - 1):
        row = st_ref[k]
        acc = acc + row * cw_ref[k:k + 1, :]
        if k > 0:
            nst_ref[k - 1] = row
    nst_ref[CONV_K - 2] = u

    def store_c(sl, v):
        yc_ref[:, sl] = v.astype(yc_ref.dtype)

    _group_norm_silu(acc, gng_ref[...], gnb_ref[...], store_c)

    lx = lx_ref[...]
    xc = lx * lcw_ref[LRU_CONV_K - 1:LRU_CONV_K, :] + lcb_ref[...]
    for k in range(LRU_CONV_K - 1):
        row = lst_ref[k]
        xc = xc + row * lcw_ref[k:k + 1, :]
        if k > 0:
            nlst_ref[k - 1] = row
    nlst_ref[LRU_CONV_K - 2] = lx

    log_sig_lam = _log_sigmoid(lam_ref[...])
    for hh in range(LRU_HEADS):
        sl = slice(hh * LRU_HD, (hh + 1) * LRU_HD)
        a, b = _lru_gates(xc[:, sl], wa_ref[hh].astype(BF16), ba_ref[:, sl],
                          wi_ref[hh].astype(BF16), bi_ref[:, sl], log_sig_lam[:, sl])
        h = a * h0_ref[:, sl] + b
        nh_ref[:, sl] = h
        yl_ref[:, sl] = (h * _gelu_tanh(lg_ref[:, sl])).astype(yl_ref.dtype)


def _mixer_sample(z, st, lst, h0, layer, yc, yl, cw, cb, gn_g, gn_b, lcw, lcb, wa, ba, wi, bi, lam):
    bt = S_BT
    rb0 = N_P // bt
    vec = lambda v: v.reshape(1, -1)
    zspec = lambda col: pl.BlockSpec((bt, C_CONV), lambda i: (i + rb0, col))
    full = lambda shape: pl.BlockSpec(shape, lambda i: (0,) * len(shape))
    any_spec = pl.BlockSpec(memory_space=pl.ANY)
    return pl.pallas_call(
        _mixer_s_kernel,
        grid=(N_S // bt,),
        in_specs=[zspec(0), zspec(1), zspec(2), zspec(3),
                  pl.BlockSpec((None, CONV_K - 1, bt, C_CONV), lambda i: (layer, 0, i, 0)),
                  pl.BlockSpec((None, LRU_CONV_K - 1, bt, C_LRU), lambda i: (layer, 0, i, 0)),
                  pl.BlockSpec((None, bt, C_LRU), lambda i: (layer, i, 0)),
                  full((CONV_K, C_CONV)), full((1, C_CONV)), full((1, C_CONV)), full((1, C_CONV)),
                  full((LRU_CONV_K, C_LRU)), full((1, C_LRU)),
                  full((LRU_HEADS, LRU_HD, LRU_HD)), full((1, C_LRU)),
                  full((LRU_HEADS, LRU_HD, LRU_HD)), full((1, C_LRU)), full((1, C_LRU)),
                  any_spec, any_spec],
        out_specs=[pl.BlockSpec((bt, C_CONV), lambda i: (i + rb0, 0)),
                   pl.BlockSpec((bt, C_LRU), lambda i: (i + rb0, 0)),
                   pl.BlockSpec((CONV_K - 1, bt, C_CONV), lambda i: (0, i, 0)),
                   pl.BlockSpec((LRU_CONV_K - 1, bt, C_LRU), lambda i: (0, i, 0)),
                   pl.BlockSpec((bt, C_LRU), lambda i: (i, 0))],
        out_shape=[jax.ShapeDtypeStruct(yc.shape, yc.dtype),
                   jax.ShapeDtypeStruct(yl.shape, yl.dtype),
                   jax.ShapeDtypeStruct(st.shape[1:], F32),
                   jax.ShapeDtypeStruct(lst.shape[1:], F32),
                   jax.ShapeDtypeStruct(h0.shape[1:], F32)],
        input_output_aliases={18: 0, 19: 1},
        compiler_params=_cparams(("arbitrary",), 48 << 20),
        name="mixer_sample",
    )(z, z, z, z, st, lst, h0, cw, vec(cb), vec(gn_g), vec(gn_b), lcw, vec(lcb),
      wa, vec(ba), wi, vec(bi), vec(lam), yc, yl)


def _attn_p_kernel(q_ref, k_ref, v_ref, o_ref):
    i = pl.program_id(0)

    @pl.when(i < N_P // ATT_TQ)
    def _():
        _attn_p_body(q_ref, k_ref, v_ref, o_ref)

    @pl.when(i == N_P // ATT_TQ)
    def _():
        o_ref[...] = jnp.zeros(o_ref.shape, o_ref.dtype)


def _attn_p_body(q_ref, k_ref, v_ref, o_ref):
    scale = MEM_HD ** -0.5
    for h in range(MEM_HEADS):
        sl = slice(h * MEM_HD, (h + 1) * MEM_HD)
        q = q_ref[:, sl].astype(BF16)
        k = k_ref[:, sl].astype(BF16)
        v = v_ref[:, sl].astype(BF16)
        s = lax.dot_general(q, k, (((1,), (1,)), ((), ())), preferred_element_type=F32) * scale
        p = jnp.exp(s - jnp.max(s, axis=-1, keepdims=True))
        pr = p / jnp.sum(p, axis=-1, keepdims=True)
        o = jnp.dot(pr.astype(BF16), v, preferred_element_type=F32)
        o_ref[:, sl] = o.astype(o_ref.dtype)


def _attn_prompt(q, k, v):
    nq = SEQ // ATT_TQ
    return pl.pallas_call(
        _attn_p_kernel,
        grid=(BATCH * nq + 1,),
        in_specs=[pl.BlockSpec((ATT_TQ, MEM_W), lambda i: (i, 0)),
                  pl.BlockSpec((MEM_LEN, MEM_W), lambda i: (_last_seq(i // nq), 0)),
                  pl.BlockSpec((MEM_LEN, MEM_W), lambda i: (_last_seq(i // nq), 0))],
        out_specs=pl.BlockSpec((ATT_TQ, MEM_W), lambda i: (i, 0)),
        out_shape=jax.ShapeDtypeStruct((N_TOK, MEM_W), BF16),
        compiler_params=_cparams(("arbitrary",), 32 << 20),
        name="attn_prompt",
    )(q, k, v)


def _attn_s_kernel(q_ref, k_ref, v_ref, o_in, o_ref, stage_ref):
    del o_in
    jj = pl.program_id(1)
    scale = MEM_HD ** -0.5
    for bb in range(ATT_S_BT):
        r = jj * ATT_S_BT + bb
        q = q_ref[pl.ds(r, 1)]
        s = jnp.sum(k_ref[bb] * q, axis=-1, keepdims=True) * scale
        p = jnp.exp(s - jnp.max(s, axis=0, keepdims=True))
        l = jnp.sum(p, axis=0)
        o = jnp.sum(p * v_ref[bb], axis=0) / l
        for h in range(MEM_HEADS):
            stage_ref[pl.ds(r, 1), h * MEM_HD:(h + 1) * MEM_HD] = o[h:h + 1, :]

    @pl.when(jj == pl.num_programs(1) - 1)
    def _():
        o_ref[...] = stage_ref[...].astype(o_ref.dtype)


def _attn_sample(q4, kc, vc, layer, o):
    bt = S_BT
    inner = bt // ATT_S_BT
    rb0 = N_P // bt
    cache_spec = pl.BlockSpec((None, ATT_S_BT, MEM_LEN, MEM_HEADS, MEM_HD),
                              lambda i, j: (layer, i * inner + j, 0, 0, 0))
    return pl.pallas_call(
        _attn_s_kernel,
        grid=(N_S // bt, inner),
        in_specs=[pl.BlockSpec((bt, MEM_HEADS, MEM_HD), lambda i, j: (i, 0, 0)),
                  cache_spec, cache_spec,
                  pl.BlockSpec(memory_space=pl.ANY)],
        out_specs=pl.BlockSpec((bt, MEM_W), lambda i, j: (i + rb0, 0)),
        out_shape=jax.ShapeDtypeStruct(o.shape, o.dtype),
        scratch_shapes=[pltpu.VMEM((bt, MEM_W), F32)],
        input_output_aliases={3: 0},
        compiler_params=_cparams(("arbitrary", "arbitrary"), 48 << 20),
        name="attn_sample",
    )(q4, kc, vc, o)


def _norm_router_kernel(x_ref, g_ref, wr_ref, br_ref, xp_ref, ids_ref, gates_ref):
    xn = _rms(x_ref[...], g_ref[...])
    xp_ref[...] = pltpu.pack_elementwise([xn[:, :HALF], xn[:, HALF:]], packed_dtype=BF16)
    logits = jnp.dot(xn.astype(BF16), wr_ref[...], preferred_element_type=F32) + br_ref[...]
    lane = lax.broadcasted_iota(I32, logits.shape, 1)
    neg = jnp.float32(-jnp.inf)

    def first_max(vals):
        m = jnp.max(vals, axis=-1, keepdims=True)
        idx = jnp.min(jnp.where(vals == m, lane, ROUTER_W), axis=-1, keepdims=True)
        return m, idx

    is_group = lane < N_GROUPS
    g_max, g_sel = first_max(jnp.where(is_group, logits, neg))
    p_g = 1.0 / jnp.sum(jnp.where(is_group, jnp.exp(logits - g_max), 0.0), axis=-1, keepdims=True)
    lo = N_GROUPS + EXPERTS_PER_GROUP * g_sel
    in_group = (lane >= lo) & (lane < lo + EXPERTS_PER_GROUP)
    e_logits = jnp.where(in_group, logits, neg)
    t1, i1 = first_max(e_logits)
    t2, i2 = first_max(jnp.where(lane == i1, neg, e_logits))
    e = jnp.exp(t2 - t1)
    w1 = p_g / (1.0 + e)
    w2 = p_g * e / (1.0 + e)
    ids_ref[...] = jnp.where(lane == 0, i1 - N_GROUPS, jnp.where(lane == 1, i2 - N_GROUPS, 0))
    gates_ref[...] = jnp.where(lane == 0, w1, jnp.where(lane == 1, w2, 0.0))


def _norm_router(x, g, wr, br):
    tm = TM_NORM
    return pl.pallas_call(
        _norm_router_kernel,
        grid=(N_TOK // tm,),
        in_specs=[pl.BlockSpec((tm, D_MODEL), lambda i: (i, 0)),
                  pl.BlockSpec((1, D_MODEL), lambda i: (0, 0)),
                  pl.BlockSpec((D_MODEL, ROUTER_W), lambda i: (0, 0)),
                  pl.BlockSpec((1, ROUTER_W), lambda i: (0, 0))],
        out_specs=[pl.BlockSpec((tm, HALF), lambda i: (i, 0)),
                   pl.BlockSpec((tm, ROUTER_W), lambda i: (i, 0)),
                   pl.BlockSpec((tm, ROUTER_W), lambda i: (i, 0))],
        out_shape=[jax.ShapeDtypeStruct((N_TOK, HALF), jnp.uint32),
                   jax.ShapeDtypeStruct((N_TOK, ROUTER_W), I32),
                   jax.ShapeDtypeStruct((N_TOK, ROUTER_W), F32)],
        compiler_params=_cparams(("arbitrary",), 8 * tm * D_MODEL * 4),
        name="norm_router",
    )(x, g.reshape(1, -1), wr, br)


def _unpack_x(xp):
    lo = pltpu.unpack_elementwise(xp, index=0, packed_dtype=BF16, unpacked_dtype=F32)
    hi = pltpu.unpack_elementwise(xp, index=1, packed_dtype=BF16, unpacked_dtype=F32)
    return lo.astype(BF16), hi.astype(BF16)


def _ffn_kernel(sbe_ref, sbb_ref, sbn_ref, sbz_ref, tok_ref, xp_hbm, wg_ref, wu_ref, wd_ref, ys_hbm,
                xbuf_ref, acc_ref, xsem, ysem, pend_ref):
    del sbe_ref
    sb = pl.program_id(0)
    j = pl.program_id(1)
    nj = pl.num_programs(1)
    slot = sb % 2

    @pl.when((sb == 0) & (j == 0))
    def _():
        pend_ref[0] = 0

    def row_copy(s, slot_, r):
        tok = tok_ref[sbb_ref[s] * MOE_BLOCK + r]
        return pltpu.make_async_copy(xp_hbm.at[pl.ds(tok, 1), :],
                                     xbuf_ref.at[slot_, pl.ds(r, 1), :], xsem.at[slot_])

    def start_blocks(s, slot_, lo, hi):
        def block(i, carry):
            def body(r, c):
                row_copy(s, slot_, i * MOE_BLOCK + r).start(priority=GATHER_PRIORITY)
                return c

            return lax.fori_loop(0, MOE_BLOCK, body, carry, unroll=ISSUE_UNROLL)

        lax.fori_loop(lo, jnp.minimum(hi, sbn_ref[s]), block, 0)

    def wait_rows(s, slot_):
        def block(i, carry):
            r0 = pl.multiple_of(i * MOE_BLOCK, MOE_BLOCK)
            pltpu.make_async_copy(xp_hbm.at[pl.ds(0, MOE_BLOCK), :],
                                  xbuf_ref.at[slot_, pl.ds(r0, MOE_BLOCK), :], xsem.at[slot_]).wait()
            return carry

        lax.fori_loop(0, sbn_ref[s], block, 0)

    def out_copy(i, blk):
        r0 = pl.multiple_of(i * MOE_BLOCK, MOE_BLOCK)
        r1 = pl.multiple_of(blk * MOE_BLOCK, MOE_BLOCK)
        return pltpu.make_async_copy(acc_ref.at[pl.ds(r0, MOE_BLOCK), :],
                                     ys_hbm.at[pl.ds(r1, MOE_BLOCK), :], ysem)

    def wait_out():
        def body(i, carry):
            out_copy(0, 0).wait()
            return carry

        lax.fori_loop(0, pend_ref[0], body, 0)
        pend_ref[0] = 0

    per_j = -(-NSUB // (D_EXPERT // TF))

    @pl.when((sb == 0) & (j == 0))
    def _():
        start_blocks(sb, slot, 0, NSUB)

    @pl.when(sb + 1 < pl.num_programs(0))
    def _():
        start_blocks(sb + 1, 1 - slot, j * per_j, (j + 1) * per_j)

    nsub = sbn_ref[sb]
    nzero = sbz_ref[sb]

    def zero_block(i, carry):
        r0 = pl.multiple_of(i * MOE_BLOCK, MOE_BLOCK)
        acc_ref[pl.ds(r0, MOE_BLOCK), :] = jnp.zeros((MOE_BLOCK, D_MODEL), F32)
        return carry

    def start_out(i, carry):
        out_copy(i, sbb_ref[sb] + i).start()
        return carry

    @pl.when(j == 0)
    def _():
        wait_rows(sb, slot)
        wait_out()
        lax.fori_loop(0, jnp.maximum(nsub, nzero), zero_block, 0)
        lax.fori_loop(0, nzero, start_out, 0)
        pend_ref[0] = nzero

    def ffn_rows(r0, rows):
        lo, hi = _unpack_x(xbuf_ref[slot, pl.ds(r0, rows), :])
        g = _dot(lo, wg_ref[:HALF, :]) + _dot(hi, wg_ref[HALF:, :])
        u = _dot(lo, wu_ref[:HALF, :]) + _dot(hi, wu_ref[HALF:, :])
        h = (_silu(g) * u).astype(BF16)
        acc_ref[pl.ds(r0, rows), :] += _dot(h, wd_ref[...])

    def pair(i, carry):
        ffn_rows(pl.multiple_of(i * (2 * MOE_BLOCK), 2 * MOE_BLOCK), 2 * MOE_BLOCK)
        return carry

    lax.fori_loop(0, nsub // 2, pair, 0)

    @pl.when(nsub % 2 == 1)
    def _():
        ffn_rows(pl.multiple_of((nsub - 1) * MOE_BLOCK, MOE_BLOCK), MOE_BLOCK)

    @pl.when((j == nj - 1) & (nsub > 0))
    def _():
        lax.fori_loop(0, nsub, start_out, 0)
        pend_ref[0] = nsub

    @pl.when((sb == pl.num_programs(0) - 1) & (j == nj - 1))
    def _():
        wait_out()


def _moe_ffn(xp, plan, w_gate, w_up, w_down, layer):
    nj = D_EXPERT // TF

    def hidden_slice(sb, j, sbn):
        return jnp.where(sbn[sb] > 0, j, nj - 1)

    up_spec = pl.BlockSpec((None, None, D_MODEL, TF),
                           lambda sb, j, sbe, sbb, sbn, sbz, tok: (layer, sbe[sb], 0, hidden_slice(sb, j, sbn)))
    down_spec = pl.BlockSpec((None, None, TF, D_MODEL),
                             lambda sb, j, sbe, sbb, sbn, sbz, tok: (layer, sbe[sb], hidden_slice(sb, j, sbn), 0))
    return pl.pallas_call(
        _ffn_kernel,
        grid_spec=pltpu.PrefetchScalarGridSpec(
            num_scalar_prefetch=5,
            grid=(N_SB, nj),
            in_specs=[pl.BlockSpec(memory_space=pl.ANY), up_spec, up_spec, down_spec],
            out_specs=pl.BlockSpec(memory_space=pl.ANY),
            scratch_shapes=[pltpu.VMEM((2, SB_ROWS, HALF), jnp.uint32),
                            pltpu.VMEM((SB_ROWS, D_MODEL), F32),
                            pltpu.SemaphoreType.DMA((2,)),
                            pltpu.SemaphoreType.DMA(()),
                            pltpu.SMEM((1,), I32)]),
        out_shape=jax.ShapeDtypeStruct((N_ROWS, D_MODEL), F32),
        compiler_params=_cparams(("arbitrary", "arbitrary"), VMEM_CAP),
        name="moe_ffn",
    )(plan["sb_e"], plan["sb_blk"], plan["sb_n"], plan["sb_z"], plan["row_tok"], xp, w_gate, w_up, w_down)


def _combine_kernel(dest_ref, x_ref, gates_ref, g_ref, ys_hbm, out_a, out_b, buf_ref, sem, *, final):
    i = pl.program_id(0)
    n = pl.num_programs(0)

    def row_copy(step, slot, t, k):
        s = (step * COMB_TM + t) * TOP_K + k
        return pltpu.make_async_copy(ys_hbm.at[pl.ds(dest_ref[s], 1), :],
                                     buf_ref.at[slot, k, pl.ds(t, 1), :], sem.at[slot])

    def start_all(step, slot):
        def body(t, carry):
            for k in range(TOP_K):
                row_copy(step, slot, t, k).start(priority=k % 2)
            return carry

        lax.fori_loop(0, COMB_TM, body, 0, unroll=ISSUE_UNROLL)

    def wait_all(step, slot):
        for k in range(TOP_K):
            pltpu.make_async_copy(ys_hbm.at[pl.ds(0, COMB_TM), :], buf_ref.at[slot, k], sem.at[slot]).wait()

    slot = i % 2

    @pl.when(i == 0)
    def _():
        start_all(i, slot)

    @pl.when(i + 1 < n)
    def _():
        start_all(i + 1, 1 - slot)

    wait_all(i, slot)
    gates = gates_ref[...]
    y = x_ref[...] + gates[:, 0:1] * buf_ref[slot, 0] + gates[:, 1:2] * buf_ref[slot, 1]
    yn = _rms(y, g_ref[...])
    if final:
        @pl.when(i < n - 1)
        def _():
            out_a[...] = yn

        @pl.when(i == n - 1)
        def _():
            out_b[...] = yn
    else:
        out_a[...] = y
        out_b[...] = yn.astype(out_b.dtype)


def _combine(x, ys, dest, gates, g_next, *, final):
    tm = COMB_TM
    assert N_S == tm
    n_p_blocks = N_P // tm
    if final:
        out_specs = [pl.BlockSpec((tm, D_MODEL), lambda i, d: (jnp.minimum(i, n_p_blocks - 1), 0)),
                     pl.BlockSpec((tm, D_MODEL), lambda i, d: (0, 0))]
        out_shape = [jax.ShapeDtypeStruct((N_P, D_MODEL), F32),
                     jax.ShapeDtypeStruct((N_S, D_MODEL), F32)]
    else:
        out_specs = [pl.BlockSpec((tm, D_MODEL), lambda i, d: (i, 0)),
                     pl.BlockSpec((tm, D_MODEL), lambda i, d: (i, 0))]
        out_shape = [jax.ShapeDtypeStruct((N_TOK, D_MODEL), F32),
                     jax.ShapeDtypeStruct((N_TOK, D_MODEL), BF16)]
    return pl.pallas_call(
        functools.partial(_combine_kernel, final=final),
        grid_spec=pltpu.PrefetchScalarGridSpec(
            num_scalar_prefetch=1,
            grid=(N_TOK // tm,),
            in_specs=[pl.BlockSpec((tm, D_MODEL), lambda i, d: (i, 0)),
                      pl.BlockSpec((tm, ROUTER_W), lambda i, d: (i, 0)),
                      pl.BlockSpec((1, D_MODEL), lambda i, d: (0, 0)),
                      pl.BlockSpec(memory_space=pl.ANY)],
            out_specs=out_specs,
            scratch_shapes=[pltpu.VMEM((2, TOP_K, tm, D_MODEL), F32),
                            pltpu.SemaphoreType.DMA((2,))]),
        out_shape=out_shape,
        compiler_params=_cparams(("arbitrary",), 40 << 20),
        name="moe_combine",
    )(dest, x, gates, g_next.reshape(1, -1), ys)


def _dispatch_plan(ids):
    flat_e = ids[:, :TOP_K].reshape(-1)
    onehot = (flat_e[:, None] == jnp.arange(N_EXPERTS, dtype=I32)[None, :]).astype(I32)
    chunks = onehot.reshape(N_SLOT // PLAN_CHUNK, PLAN_CHUNK, N_EXPERTS).astype(BF16)
    tri = jnp.tril(jnp.ones((PLAN_CHUNK, PLAN_CHUNK), BF16))
    within = jnp.einsum("ij,cjk->cik", tri, chunks, preferred_element_type=F32).astype(I32)
    totals = within[:, -1, :]
    before = jnp.cumsum(totals, axis=0) - totals
    csum = (within + before[:, None, :]).reshape(N_SLOT, N_EXPERTS)
    rank = jnp.sum(onehot * csum, axis=1) - 1
    counts = csum[-1]
    nblk = (counts + MOE_BLOCK - 1) // MOE_BLOCK
    bend = jnp.cumsum(nblk)
    bstart = bend - nblk
    dest = (jnp.sum(onehot * bstart[None, :], axis=1) * MOE_BLOCK + rank).astype(I32)
    row_tok = jnp.zeros((N_ROWS,), I32).at[dest].set(jnp.arange(N_SLOT, dtype=I32) // TOP_K)

    nsup = (nblk + NSUB - 1) // NSUB
    sup_end = jnp.cumsum(nsup)
    sup_start = sup_end - nsup
    total_sup = sup_end[-1]
    sb = jnp.arange(N_SB, dtype=I32)
    e_of = jnp.minimum(jnp.sum((sup_end[None, :] <= sb[:, None]).astype(I32), axis=1), N_EXPERTS - 1)
    q = sb - sup_start[e_of]
    real = sb < total_sup
    n_real = jnp.clip(nblk[e_of] - NSUB * q, 0, NSUB)
    blk_real = bstart[e_of] + NSUB * q
    blk_tail = bend[-1] + NSUB * (sb - total_sup)
    n_tail = jnp.clip(N_BLOCKS - blk_tail, 0, NSUB)
    e_last = e_of[total_sup - 1]
    return dict(
        dest=dest, row_tok=row_tok,
        sb_e=jnp.where(real, e_of, e_last).astype(I32),
        sb_blk=jnp.where(real, blk_real, jnp.minimum(blk_tail, N_BLOCKS - 1)).astype(I32),
        sb_n=jnp.where(real, n_real, 0).astype(I32),
        sb_z=jnp.where(real, 0, n_tail).astype(I32))


def kernel(x_prompt, x_sample, mem_prompt, state_conv, state_lru_conv, state_lru_h, cache_mem_k, cache_mem_v, norm_mix, w_in, conv_w, conv_b, conv_gn_g, conv_gn_b, lru_conv_w, lru_conv_b, lru_wa, lru_ba, lru_wi, lru_bi, lru_lambda, w_out, norm_attn, norm_mem_kv, w_q, w_k, w_v, w_o, norm_ffn, w_router_g, b_router_g, w_router_e, b_router_e, w_gate, w_up, w_down, norm_final):
    x = jnp.concatenate([x_prompt.reshape(N_P, D_MODEL), x_sample.reshape(N_S, D_MODEL)], axis=0)
    mem = mem_prompt.reshape(BATCH * MEM_LEN, D_MODEL)
    xn = _norm(x, norm_mix[0], tm=TM_NORM)

    state_conv_t = state_conv.transpose(0, 2, 1, 3)
    state_lru_conv_t = state_lru_conv.transpose(0, 2, 1, 3)
    conv_p, lruc_p, h_p, mk_p, mv_p, conv_s, lruc_s, h_s = ([] for _ in range(8))
    for l in range(DEPTH):
        z = _mm([xn], w_in, l, tn=1024, tm=TM, name="w_in")
        yc, cst = _conv_prompt(z, conv_w[l], conv_b[l], conv_gn_g[l], conv_gn_b[l])
        yl, lst, hl = _lru_prompt(z, lru_conv_w[l], lru_conv_b[l], lru_wa[l], lru_ba[l],
                                  lru_wi[l], lru_bi[l], lru_lambda[l])
        yc, yl, cst_s, lst_s, hl_s = _mixer_sample(
            z, state_conv_t, state_lru_conv_t, state_lru_h, l, yc, yl,
            conv_w[l], conv_b[l], conv_gn_g[l], conv_gn_b[l], lru_conv_w[l], lru_conv_b[l],
            lru_wa[l], lru_ba[l], lru_wi[l], lru_bi[l], lru_lambda[l])
        x = _mm([yc, yl], w_out, l, tn=1024, tm=TM_NORM, res=x, name="w_out")
        conv_p.append(cst); lruc_p.append(lst); h_p.append(hl.reshape(BATCH, C_LRU))
        conv_s.append(cst_s); lruc_s.append(lst_s); h_s.append(hl_s)

        q = _norm_mm(x, norm_attn[l], w_q, l, tm=TM_NORM, name="w_q")
        mn = _norm(mem, norm_mem_kv[l], tm=256)
        k_p = _mm([mn], w_k, l, tn=512, tm=BATCH * MEM_LEN, name="w_k")
        v_p = _mm([mn], w_v, l, tn=512, tm=BATCH * MEM_LEN, name="w_v")
        o = _attn_prompt(q, k_p, v_p)
        o = _attn_sample(q[N_P:].reshape(N_S, MEM_HEADS, MEM_HD), cache_mem_k, cache_mem_v, l, o)
        x = _mm([o], w_o, l, tn=1024, tm=TM, res=x, name="w_o")
        mk_p.append(k_p.reshape(BATCH, MEM_LEN, MEM_HEADS, MEM_HD))
        mv_p.append(v_p.reshape(BATCH, MEM_LEN, MEM_HEADS, MEM_HD))

        wr = jnp.concatenate([w_router_g[l], w_router_e[l].reshape(D_MODEL, N_EXPERTS),
                              jnp.zeros((D_MODEL, ROUTER_W - N_GROUPS - N_EXPERTS), F32)],
                             axis=1).astype(BF16)
        br = jnp.concatenate([b_router_g[l], b_router_e[l].reshape(N_EXPERTS),
                              jnp.zeros((ROUTER_W - N_GROUPS - N_EXPERTS,), F32)]).reshape(1, ROUTER_W)
        xp, ids, gates = _norm_router(x, norm_ffn[l], wr, br)
        plan = _dispatch_plan(ids)
        ys = _moe_ffn(xp, plan, w_gate, w_up, w_down, l)
        if l < DEPTH - 1:
            x, xn = _combine(x, ys, plan["dest"], gates, norm_mix[l + 1], final=False)
        else:
            y_p, y_s = _combine(x, ys, plan["dest"], gates, norm_final, final=True)

    y_prompt = y_p.reshape(BATCH, SEQ, D_MODEL)
    y_sample = y_s.reshape(DEC_BATCH, 1, D_MODEL)
    return (y_prompt, y_sample, jnp.stack(conv_p), jnp.stack(lruc_p), jnp.stack(h_p),
            jnp.stack(mk_p), jnp.stack(mv_p), jnp.stack(conv_s).transpose(0, 2, 1, 3),
            jnp.stack(lruc_s).transpose(0, 2, 1, 3), jnp.stack(h_s))
```

```python
import functools
import math

import jax
import jax.numpy as jnp
from jax import lax
from jax.experimental import pallas as pl
from jax.experimental.pallas import tpu as pltpu

F32 = jnp.float32
BF16 = jnp.bfloat16
I32 = jnp.int32

D_MODEL = 4096
BATCH = 4
SEQ = 2048
DEPTH = 2
DEC_BATCH = 128
C_CONV = D_MODEL // 2
C_LRU = D_MODEL // 2
D_IN = 2 * C_CONV + 2 * C_LRU
CONV_GROUPS = 16
GROUP_W = C_CONV // CONV_GROUPS
CONV_K = 31
LRU_HEADS = 16
LRU_HD = C_LRU // LRU_HEADS
LRU_CONV_K = 4
RG_C = 8.0
MEM_LEN = 256
MEM_HEADS = 4
MEM_HD = D_MODEL // 16
MEM_W = MEM_HEADS * MEM_HD
N_GROUPS = 4
EXPERTS_PER_GROUP = 8
N_EXPERTS = N_GROUPS * EXPERTS_PER_GROUP
TOP_K = 2
D_EXPERT = D_MODEL // 4
EPS = 1e-6

N_P = BATCH * SEQ
N_S = DEC_BATCH
N_TOK = N_P + N_S
N_SLOT = N_TOK * TOP_K

LANES = 128
SUBLANES = 8
VMEM_CAP = 56 * 1024 * 1024

TM = 1040
TM_NORM = 416
MOE_BLOCK = 128
N_BLOCKS = -(-(N_SLOT + N_EXPERTS * (MOE_BLOCK - 1)) // MOE_BLOCK)
N_ROWS = N_BLOCKS * MOE_BLOCK
TF = 256
ROUTER_W = LANES
HALF = D_MODEL // 2
CONV_TT = 64
CONV_UNROLL = 4
SCAN_UNROLL = 8
CONV_TC = 256
LRU_TT = 256
COMB_TM = 128
PLAN_CHUNK = 128
ATT_TQ = 512
S_BT = 16
ATT_S_BT = 4
ISSUE_UNROLL = 8
GATHER_PRIORITY = 1
NSUB = 6
SB_ROWS = NSUB * MOE_BLOCK
N_SB = (N_BLOCKS + (NSUB - 1) * (N_EXPERTS + 1)) // NSUB + 1


def _cparams(sem, vmem_bytes):
    limit = min(VMEM_CAP, max(32 * 1024 * 1024, int(vmem_bytes)))
    return pltpu.CompilerParams(dimension_semantics=sem, vmem_limit_bytes=limit)


def _sigmoid(x):
    return 1.0 / (1.0 + jnp.exp(-x))


def _silu(x):
    return x * _sigmoid(x)


def _gelu_tanh(x):
    c = math.sqrt(2.0 / math.pi)
    return 0.5 * x * (1.0 + jnp.tanh(c * (x + 0.044715 * (x * x * x))))


def _dot(a, w):
    return lax.dot_general(a, w, (((1,), (0,)), ((), ())), preferred_element_type=F32)


def _bf16_round(x):
    return x.astype(BF16).astype(F32)


def _rms(x, g):
    ms = jnp.mean(x * x, axis=-1, keepdims=True)
    return x * lax.rsqrt(ms + EPS) * g


def _norm_kernel(x_ref, g_ref, o_ref):
    o_ref[...] = _rms(x_ref[...], g_ref[...]).astype(o_ref.dtype)


def _norm(x, g, *, tm):
    n_rows, d = x.shape
    return pl.pallas_call(
        _norm_kernel,
        grid=(n_rows // tm,),
        in_specs=[pl.BlockSpec((tm, d), lambda i: (i, 0)),
                  pl.BlockSpec((1, d), lambda i: (0, 0))],
        out_specs=pl.BlockSpec((tm, d), lambda i: (i, 0)),
        out_shape=jax.ShapeDtypeStruct((n_rows, d), BF16),
        compiler_params=_cparams(("arbitrary",), 6 * tm * d * 4),
        name="rmsnorm",
    )(x, g.reshape(1, d))


def _mm_kernel(*refs, n_a, has_res):
    a_refs = refs[:n_a]
    w_ref = refs[n_a]
    res_ref = refs[n_a + 1] if has_res else None
    o_ref = refs[n_a + 1 + int(has_res)]
    acc = None
    off = 0
    for a_ref in a_refs:
        k = a_ref.shape[1]
        part = lax.dot_general(a_ref[...].astype(BF16), w_ref[off:off + k, :], (((1,), (0,)), ((), ())),
                               preferred_element_type=F32)
        acc = part if acc is None else acc + part
        off += k
    if has_res:
        acc = acc + res_ref[...]
    o_ref[...] = acc.astype(o_ref.dtype)


def _mm(a_list, w, layer, *, tn, tm, res=None, out_dtype=F32, name="proj"):
    m = a_list[0].shape[0]
    _, k, n = w.shape
    assert sum(a.shape[1] for a in a_list) == k
    in_specs = [pl.BlockSpec((tm, a.shape[1]), lambda j, i: (i, 0)) for a in a_list]
    in_specs.append(pl.BlockSpec((None, k, tn), lambda j, i: (layer, 0, j), pipeline_mode=pl.Buffered(1)))
    args = list(a_list) + [w]
    if res is not None:
        in_specs.append(pl.BlockSpec((tm, tn), lambda j, i: (i, j)))
        args.append(res)
    a_bytes = sum(a.dtype.itemsize * a.shape[1] for a in a_list) * tm
    vmem = 2 * a_bytes + k * tn * 4 + 6 * tm * tn * 4 + (4 << 20)
    return pl.pallas_call(
        functools.partial(_mm_kernel, n_a=len(a_list), has_res=res is not None),
        grid=(n // tn, m // tm),
        in_specs=in_specs,
        out_specs=pl.BlockSpec((tm, tn), lambda j, i: (i, j)),
        out_shape=jax.ShapeDtypeStruct((m, n), out_dtype),
        compiler_params=_cparams(("arbitrary", "arbitrary"), vmem),
        name=name,
    )(*args)


def _norm_mm_kernel(x_ref, g_ref, w_ref, o_ref):
    o_ref[...] = _dot(_rms(x_ref[...], g_ref[...]).astype(BF16), w_ref[...])


def _norm_mm(x, g, w, layer, *, tm, name):
    m, d = x.shape
    _, k, n = w.shape
    return pl.pallas_call(
        _norm_mm_kernel,
        grid=(m // tm,),
        in_specs=[pl.BlockSpec((tm, d), lambda i: (i, 0)),
                  pl.BlockSpec((1, d), lambda i: (0, 0)),
                  pl.BlockSpec((None, k, n), lambda i: (layer, 0, 0), pipeline_mode=pl.Buffered(1))],
        out_specs=pl.BlockSpec((tm, n), lambda i: (i, 0)),
        out_shape=jax.ShapeDtypeStruct((m, n), F32),
        compiler_params=_cparams(("arbitrary",), VMEM_CAP),
        name=name,
    )(x, g.reshape(1, d), w)


def _group_norm_silu(y, gn_g, gn_b, store):
    for g in range(y.shape[1] // GROUP_W):
        sl = slice(g * GROUP_W, (g + 1) * GROUP_W)
        yg = y[:, sl]
        mu = jnp.mean(yg, axis=-1, keepdims=True)
        d = yg - mu
        var = jnp.mean(d * d, axis=-1, keepdims=True)
        yn = d * lax.rsqrt(var + EPS) * gn_g[:, sl] + gn_b[:, sl]
        store(sl, _silu(yn))


CONV_PAD = 32


def _causal_taps(src_ref, w_ref, t0, tt, first_off, n_taps, init):
    acc = init
    for s in range(SUBLANES):
        part = None
        for k in range(n_taps):
            off = first_off + k
            if off % SUBLANES != s:
                continue
            base = pl.multiple_of(t0 + (off - s), SUBLANES)
            term = src_ref[pl.ds(base, tt + SUBLANES), :] * w_ref[k:k + 1, :]
            part = term if part is None else part + term
        if part is not None:
            acc = acc + part[s:s + tt, :]
    return acc


def _last_seq(b):
    return jnp.minimum(b, BATCH - 1)


def _prompt_or_zero(body, y_ref):
    b = pl.program_id(1)

    @pl.when(b < BATCH)
    def _():
        body()

    @pl.when(b == BATCH)
    def _():
        y_ref[...] = jnp.zeros(y_ref.shape, y_ref.dtype)


def _conv_p_kernel(*refs):
    _prompt_or_zero(functools.partial(_conv_p_body, *refs), refs[6])


def _conv_p_body(cv_ref, cg_ref, w_ref, b_ref, gng_ref, gnb_ref, y_ref, st_ref, upad_ref, wr_ref):
    t_len = cv_ref.shape[0]
    tc = cv_ref.shape[1]
    wr_ref[0:CONV_K, :] = _bf16_round(w_ref[...])
    upad_ref[0:CONV_PAD, :] = jnp.zeros((CONV_PAD, tc), F32)
    upad_ref[CONV_PAD + t_len:CONV_PAD + t_len + SUBLANES, :] = jnp.zeros((SUBLANES, tc), F32)

    def glu(c, carry):
        t0 = pl.multiple_of(c * LRU_TT, LRU_TT)
        u = cv_ref[pl.ds(t0, LRU_TT), :] * _sigmoid(cg_ref[pl.ds(t0, LRU_TT), :])
        upad_ref[pl.ds(CONV_PAD + t0, LRU_TT), :] = _bf16_round(u)
        return carry

    lax.fori_loop(0, t_len // LRU_TT, glu, 0)
    tail = slice(t_len - (CONV_K - 1), t_len)
    st_ref[0] = cv_ref[tail, :] * _sigmoid(cg_ref[tail, :])

    bias = b_ref[...]
    gn_g = gng_ref[...]
    gn_b = gnb_ref[...]
    shift = CONV_PAD - (CONV_K - 1)

    def chunk(c, carry):
        t0 = pl.multiple_of(c * CONV_TT, CONV_TT)
        acc = _causal_taps(upad_ref, wr_ref, t0, CONV_TT, shift, CONV_K,
                           jnp.zeros((CONV_TT, tc), F32) + bias)

        def store(sl, v):
            y_ref[pl.ds(t0, CONV_TT), sl] = v.astype(y_ref.dtype)

        _group_norm_silu(acc, gn_g, gn_b, store)
        return carry

    lax.fori_loop(0, t_len // CONV_TT, chunk, 0, unroll=CONV_UNROLL)


def _conv_prompt(z, conv_w, conv_b, gn_g, gn_b):
    tc = CONV_TC
    nct = C_CONV // tc
    return pl.pallas_call(
        _conv_p_kernel,
        grid=(nct, BATCH + 1),
        in_specs=[pl.BlockSpec((SEQ, tc), lambda c, b: (_last_seq(b), c)),
                  pl.BlockSpec((SEQ, tc), lambda c, b: (_last_seq(b), c + nct)),
                  pl.BlockSpec((CONV_K, tc), lambda c, b: (0, c)),
                  pl.BlockSpec((1, tc), lambda c, b: (0, c)),
                  pl.BlockSpec((1, tc), lambda c, b: (0, c)),
                  pl.BlockSpec((1, tc), lambda c, b: (0, c))],
        out_specs=[pl.BlockSpec((SEQ, tc), lambda c, b: (b, c)),
                   pl.BlockSpec((1, CONV_K - 1, tc), lambda c, b: (_last_seq(b), 0, c))],
        out_shape=[jax.ShapeDtypeStruct((N_TOK, C_CONV), BF16),
                   jax.ShapeDtypeStruct((BATCH, CONV_K - 1, C_CONV), F32)],
        scratch_shapes=[pltpu.VMEM((CONV_PAD + SEQ + SUBLANES, tc), F32),
                        pltpu.VMEM((CONV_PAD, tc), F32)],
        compiler_params=_cparams(("arbitrary", "arbitrary"), 8 * SEQ * tc * 4),
        name="conv_prompt",
    )(z, z, conv_w, conv_b.reshape(1, -1), gn_g.reshape(1, -1), gn_b.reshape(1, -1))


def _log_sigmoid(x):
    return -(jnp.maximum(-x, 0.0) + jnp.log(1.0 + jnp.exp(-jnp.abs(x))))


def _lru_gates(xc, wa, ba, wi, bi, log_sig_lam):
    xb = xc.astype(BF16)
    r = _sigmoid(jnp.dot(xb, wa, preferred_element_type=F32) + ba)
    i = _sigmoid(jnp.dot(xb, wi, preferred_element_type=F32) + bi)
    a = jnp.exp(RG_C * r * log_sig_lam)
    return a, jnp.sqrt(1.0 - a * a) * (i * xc)


LRU_PAD = 8


def _lru_p_kernel(*refs):
    _prompt_or_zero(functools.partial(_lru_p_body, *refs), refs[9])


def _lru_p_body(lx_ref, lg_ref, cw_ref, cb_ref, wa_ref, ba_ref, wi_ref, bi_ref, lam_ref,
                y_ref, buf_ref, h_ref, xpad_ref, a_ref, b_ref, cwr_ref, hs_ref):
    t_len = lx_ref.shape[0]
    tc = lx_ref.shape[1]
    n_heads = tc // LRU_HD
    xpad_ref[0:LRU_PAD, :] = jnp.zeros((LRU_PAD, tc), F32)
    xpad_ref[LRU_PAD + t_len:LRU_PAD + t_len + SUBLANES, :] = jnp.zeros((SUBLANES, tc), F32)

    def copy_in(c, carry):
        t0 = pl.multiple_of(c * LRU_TT, LRU_TT)
        xpad_ref[pl.ds(LRU_PAD + t0, LRU_TT), :] = _bf16_round(lx_ref[pl.ds(t0, LRU_TT), :])
        return carry

    lax.fori_loop(0, t_len // LRU_TT, copy_in, 0)
    buf_ref[0] = lx_ref[t_len - (LRU_CONV_K - 1):t_len, :]

    cb = cb_ref[...]
    cwr_ref[0:LRU_CONV_K, :] = _bf16_round(cw_ref[...])
    log_sig_lam = _log_sigmoid(lam_ref[...])
    shift = LRU_PAD - (LRU_CONV_K - 1)

    def gates(c, carry):
        t0 = pl.multiple_of(c * LRU_TT, LRU_TT)
        xc = _causal_taps(xpad_ref, cwr_ref, t0, LRU_TT, shift, LRU_CONV_K,
                          jnp.zeros((LRU_TT, tc), F32) + cb)
        for hh in range(n_heads):
            sl = slice(hh * LRU_HD, (hh + 1) * LRU_HD)
            a, b = _lru_gates(xc[:, sl], wa_ref[hh].astype(BF16), ba_ref[:, sl],
                              wi_ref[hh].astype(BF16), bi_ref[:, sl], log_sig_lam[:, sl])
            a_ref[pl.ds(t0, LRU_TT), sl] = a
            b_ref[pl.ds(t0, LRU_TT), sl] = b
        return carry

    lax.fori_loop(0, t_len // LRU_TT, gates, 0)

    row = lax.broadcasted_iota(I32, (SUBLANES, tc), 0)

    def scan(i, h):
        t0 = pl.multiple_of(i * SUBLANES, SUBLANES)
        a = a_ref[pl.ds(t0, SUBLANES), :]
        b = b_ref[pl.ds(t0, SUBLANES), :]
        for d in (1, 2, 4):
            a_sh = pltpu.roll(a, d, axis=0)
            b_sh = pltpu.roll(b, d, axis=0)
            m = row >= d
            b = jnp.where(m, a * b_sh + b, b)
            a = jnp.where(m, a * a_sh, a)
        hs_ref[pl.ds(t0, SUBLANES), :] = a * h + b
        return a[SUBLANES - 1:SUBLANES, :] * h + b[SUBLANES - 1:SUBLANES, :]

    h_last = lax.fori_loop(0, t_len // SUBLANES, scan, jnp.zeros((1, tc), F32), unroll=SCAN_UNROLL)
    h_ref[0] = h_last

    def gate_out(c, carry):
        t0 = pl.multiple_of(c * LRU_TT, LRU_TT)
        y = hs_ref[pl.ds(t0, LRU_TT), :] * _gelu_tanh(lg_ref[pl.ds(t0, LRU_TT), :])
        y_ref[pl.ds(t0, LRU_TT), :] = y.astype(y_ref.dtype)
        return carry

    lax.fori_loop(0, t_len // LRU_TT, gate_out, 0)


def _lru_prompt(z, cw, cb, wa, ba, wi, bi, lam):
    tc = CONV_TC
    nct = C_LRU // tc
    hpb = tc // LRU_HD
    col0 = 2 * C_CONV // tc
    vec = lambda v: v.reshape(1, -1)
    return pl.pallas_call(
        _lru_p_kernel,
        grid=(nct, BATCH + 1),
        in_specs=[pl.BlockSpec((SEQ, tc), lambda c, b: (_last_seq(b), c + col0)),
                  pl.BlockSpec((SEQ, tc), lambda c, b: (_last_seq(b), c + col0 + nct)),
                  pl.BlockSpec((LRU_CONV_K, tc), lambda c, b: (0, c)),
                  pl.BlockSpec((1, tc), lambda c, b: (0, c)),
                  pl.BlockSpec((hpb, LRU_HD, LRU_HD), lambda c, b: (c, 0, 0)),
                  pl.BlockSpec((1, tc), lambda c, b: (0, c)),
                  pl.BlockSpec((hpb, LRU_HD, LRU_HD), lambda c, b: (c, 0, 0)),
                  pl.BlockSpec((1, tc), lambda c, b: (0, c)),
                  pl.BlockSpec((1, tc), lambda c, b: (0, c))],
        out_specs=[pl.BlockSpec((SEQ, tc), lambda c, b: (b, c)),
                   pl.BlockSpec((1, LRU_CONV_K - 1, tc), lambda c, b: (_last_seq(b), 0, c)),
                   pl.BlockSpec((1, 1, tc), lambda c, b: (_last_seq(b), 0, c))],
        out_shape=[jax.ShapeDtypeStruct((N_TOK, C_LRU), BF16),
                   jax.ShapeDtypeStruct((BATCH, LRU_CONV_K - 1, C_LRU), F32),
                   jax.ShapeDtypeStruct((BATCH, 1, C_LRU), F32)],
        scratch_shapes=[pltpu.VMEM((LRU_PAD + SEQ + SUBLANES, tc), F32),
                        pltpu.VMEM((SEQ, tc), F32),
                        pltpu.VMEM((SEQ, tc), F32),
                        pltpu.VMEM((LRU_PAD, tc), F32),
                        pltpu.VMEM((SEQ, tc), F32)],
        compiler_params=_cparams(("arbitrary", "arbitrary"), 12 * SEQ * tc * 4),
        name="lru_prompt",
    )(z, z, cw, vec(cb), wa, vec(ba), wi, vec(bi), vec(lam))


def _mixer_s_kernel(cv_ref, cg_ref, lx_ref, lg_ref, st_ref, lst_ref, h0_ref,
                    cw_ref, cb_ref, gng_ref, gnb_ref, lcw_ref, lcb_ref,
                    wa_ref, ba_ref, wi_ref, bi_ref, lam_ref, yc_in, yl_in,
                    yc_ref, yl_ref, nst_ref, nlst_ref, nh_ref):
    del yc_in, yl_in
    u = cv_ref[...] * _sigmoid(cg_ref[...])
    acc = u * cw_ref[CONV_K - 1:CONV_K, :] + cb_ref[...]
    for k in range(CONV_K - 1):
        row = st_ref[k]
        acc = acc + row * cw_ref[k:k + 1, :]
        if k > 0:
            nst_ref[k - 1] = row
    nst_ref[CONV_K - 2] = u

    def store_c(sl, v):
        yc_ref[:, sl] = v.astype(yc_ref.dtype)

    _group_norm_silu(acc, gng_ref[...], gnb_ref[...], store_c)

    lx = lx_ref[...]
    xc = lx * lcw_ref[LRU_CONV_K - 1:LRU_CONV_K, :] + lcb_ref[...]
    for k in range(LRU_CONV_K - 1):
        row = lst_ref[k]
        xc = xc + row * lcw_ref[k:k + 1, :]
        if k > 0:
            nlst_ref[k - 1] = row
    nlst_ref[LRU_CONV_K - 2] = lx

    log_sig_lam = _log_sigmoid(lam_ref[...])
    for hh in range(LRU_HEADS):
        sl = slice(hh * LRU_HD, (hh + 1) * LRU_HD)
        a, b = _lru_gates(xc[:, sl], wa_ref[hh].astype(BF16), ba_ref[:, sl],
                          wi_ref[hh].astype(BF16), bi_ref[:, sl], log_sig_lam[:, sl])
        h = a * h0_ref[:, sl] + b
        nh_ref[:, sl] = h
        yl_ref[:, sl] = (h * _gelu_tanh(lg_ref[:, sl])).astype(yl_ref.dtype)


def _mixer_sample(z, st, lst, h0, layer, yc, yl, cw, cb, gn_g, gn_b, lcw, lcb, wa, ba, wi, bi, lam):
    bt = S_BT
    rb0 = N_P // bt
    vec = lambda v: v.reshape(1, -1)
    zspec = lambda col: pl.BlockSpec((bt, C_CONV), lambda i: (i + rb0, col))
    full = lambda shape: pl.BlockSpec(shape, lambda i: (0,) * len(shape))
    any_spec = pl.BlockSpec(memory_space=pl.ANY)
    return pl.pallas_call(
        _mixer_s_kernel,
        grid=(N_S // bt,),
        in_specs=[zspec(0), zspec(1), zspec(2), zspec(3),
                  pl.BlockSpec((None, CONV_K - 1, bt, C_CONV), lambda i: (layer, 0, i, 0)),
                  pl.BlockSpec((None, LRU_CONV_K - 1, bt, C_LRU), lambda i: (layer, 0, i, 0)),
                  pl.BlockSpec((None, bt, C_LRU), lambda i: (layer, i, 0)),
                  full((CONV_K, C_CONV)), full((1, C_CONV)), full((1, C_CONV)), full((1, C_CONV)),
                  full((LRU_CONV_K, C_LRU)), full((1, C_LRU)),
                  full((LRU_HEADS, LRU_HD, LRU_HD)), full((1, C_LRU)),
                  full((LRU_HEADS, LRU_HD, LRU_HD)), full((1, C_LRU)), full((1, C_LRU)),
                  any_spec, any_spec],
        out_specs=[pl.BlockSpec((bt, C_CONV), lambda i: (i + rb0, 0)),
                   pl.BlockSpec((bt, C_LRU), lambda i: (i + rb0, 0)),
                   pl.BlockSpec((CONV_K - 1, bt, C_CONV), lambda i: (0, i, 0)),
                   pl.BlockSpec((LRU_CONV_K - 1, bt, C_LRU), lambda i: (0, i, 0)),
                   pl.BlockSpec((bt, C_LRU), lambda i: (i, 0))],
        out_shape=[jax.ShapeDtypeStruct(yc.shape, yc.dtype),
                   jax.ShapeDtypeStruct(yl.shape, yl.dtype),
                   jax.ShapeDtypeStruct(st.shape[1:], F32),
                   jax.ShapeDtypeStruct(lst.shape[1:], F32),
                   jax.ShapeDtypeStruct(h0.shape[1:], F32)],
        input_output_aliases={18: 0, 19: 1},
        compiler_params=_cparams(("arbitrary",), 48 << 20),
        name="mixer_sample",
    )(z, z, z, z, st, lst, h0, cw, vec(cb), vec(gn_g), vec(gn_b), lcw, vec(lcb),
      wa, vec(ba), wi, vec(bi), vec(lam), yc, yl)


def _attn_p_kernel(q_ref, k_ref, v_ref, o_ref):
    i = pl.program_id(0)

    @pl.when(i < N_P // ATT_TQ)
    def _():
        _attn_p_body(q_ref, k_ref, v_ref, o_ref)

    @pl.when(i == N_P // ATT_TQ)
    def _():
        o_ref[...] = jnp.zeros(o_ref.shape, o_ref.dtype)


def _attn_p_body(q_ref, k_ref, v_ref, o_ref):
    scale = MEM_HD ** -0.5
    for h in range(MEM_HEADS):
        sl = slice(h * MEM_HD, (h + 1) * MEM_HD)
        q = q_ref[:, sl].astype(BF16)
        k = k_ref[:, sl].astype(BF16)
        v = v_ref[:, sl].astype(BF16)
        s = lax.dot_general(q, k, (((1,), (1,)), ((), ())), preferred_element_type=F32) * scale
        p = jnp.exp(s - jnp.max(s, axis=-1, keepdims=True))
        pr = p / jnp.sum(p, axis=-1, keepdims=True)
        o = jnp.dot(pr.astype(BF16), v, preferred_element_type=F32)
        o_ref[:, sl] = o.astype(o_ref.dtype)


def _attn_prompt(q, k, v):
    nq = SEQ // ATT_TQ
    return pl.pallas_call(
        _attn_p_kernel,
        grid=(BATCH * nq + 1,),
        in_specs=[pl.BlockSpec((ATT_TQ, MEM_W), lambda i: (i, 0)),
                  pl.BlockSpec((MEM_LEN, MEM_W), lambda i: (_last_seq(i // nq), 0)),
                  pl.BlockSpec((MEM_LEN, MEM_W), lambda i: (_last_seq(i // nq), 0))],
        out_specs=pl.BlockSpec((ATT_TQ, MEM_W), lambda i: (i, 0)),
        out_shape=jax.ShapeDtypeStruct((N_TOK, MEM_W), BF16),
        compiler_params=_cparams(("arbitrary",), 32 << 20),
        name="attn_prompt",
    )(q, k, v)


def _attn_s_kernel(q_ref, k_ref, v_ref, o_in, o_ref, stage_ref):
    del o_in
    jj = pl.program_id(1)
    scale = MEM_HD ** -0.5
    for bb in range(ATT_S_BT):
        r = jj * ATT_S_BT + bb
        q = q_ref[pl.ds(r, 1)]
        s = jnp.sum(k_ref[bb] * q, axis=-1, keepdims=True) * scale
        p = jnp.exp(s - jnp.max(s, axis=0, keepdims=True))
        l = jnp.sum(p, axis=0)
        o = jnp.sum(p * v_ref[bb], axis=0) / l
        for h in range(MEM_HEADS):
            stage_ref[pl.ds(r, 1), h * MEM_HD:(h + 1) * MEM_HD] = o[h:h + 1, :]

    @pl.when(jj == pl.num_programs(1) - 1)
    def _():
        o_ref[...] = stage_ref[...].astype(o_ref.dtype)


def _attn_sample(q4, kc, vc, layer, o):
    bt = S_BT
    inner = bt // ATT_S_BT
    rb0 = N_P // bt
    cache_spec = pl.BlockSpec((None, ATT_S_BT, MEM_LEN, MEM_HEADS, MEM_HD),
                              lambda i, j: (layer, i * inner + j, 0, 0, 0))
    return pl.pallas_call(
        _attn_s_kernel,
        grid=(N_S // bt, inner),
        in_specs=[pl.BlockSpec((bt, MEM_HEADS, MEM_HD), lambda i, j: (i, 0, 0)),
                  cache_spec, cache_spec,
                  pl.BlockSpec(memory_space=pl.ANY)],
        out_specs=pl.BlockSpec((bt, MEM_W), lambda i, j: (i + rb0, 0)),
        out_shape=jax.ShapeDtypeStruct(o.shape, o.dtype),
        scratch_shapes=[pltpu.VMEM((bt, MEM_W), F32)],
        input_output_aliases={3: 0},
        compiler_params=_cparams(("arbitrary", "arbitrary"), 48 << 20),
        name="attn_sample",
    )(q4, kc, vc, o)


def _norm_router_kernel(x_ref, g_ref, wr_ref, br_ref, xp_ref, ids_ref, gates_ref):
    xn = _rms(x_ref[...], g_ref[...])
    xp_ref[...] = pltpu.pack_elementwise([xn[:, :HALF], xn[:, HALF:]], packed_dtype=BF16)
    logits = jnp.dot(xn.astype(BF16), wr_ref[...], preferred_element_type=F32) + br_ref[...]
    lane = lax.broadcasted_iota(I32, logits.shape, 1)
    neg = jnp.float32(-jnp.inf)

    def first_max(vals):
        m = jnp.max(vals, axis=-1, keepdims=True)
        idx = jnp.min(jnp.where(vals == m, lane, ROUTER_W), axis=-1, keepdims=True)
        return m, idx

    is_group = lane < N_GROUPS
    g_max, g_sel = first_max(jnp.where(is_group, logits, neg))
    p_g = 1.0 / jnp.sum(jnp.where(is_group, jnp.exp(logits - g_max), 0.0), axis=-1, keepdims=True)
    lo = N_GROUPS + EXPERTS_PER_GROUP * g_sel
    in_group = (lane >= lo) & (lane < lo + EXPERTS_PER_GROUP)
    e_logits = jnp.where(in_group, logits, neg)
    t1, i1 = first_max(e_logits)
    t2, i2 = first_max(jnp.where(lane == i1, neg, e_logits))
    e = jnp.exp(t2 - t1)
    w1 = p_g / (1.0 + e)
    w2 = p_g * e / (1.0 + e)
    ids_ref[...] = jnp.where(lane == 0, i1 - N_GROUPS, jnp.where(lane == 1, i2 - N_GROUPS, 0))
    gates_ref[...] = jnp.where(lane == 0, w1, jnp.where(lane == 1, w2, 0.0))


def _norm_router(x, g, wr, br):
    tm = TM_NORM
    return pl.pallas_call(
        _norm_router_kernel,
        grid=(N_TOK // tm,),
        in_specs=[pl.BlockSpec((tm, D_MODEL), lambda i: (i, 0)),
                  pl.BlockSpec((1, D_MODEL), lambda i: (0, 0)),
                  pl.BlockSpec((D_MODEL, ROUTER_W), lambda i: (0, 0)),
                  pl.BlockSpec((1, ROUTER_W), lambda i: (0, 0))],
        out_specs=[pl.BlockSpec((tm, HALF), lambda i: (i, 0)),
                   pl.BlockSpec((tm, ROUTER_W), lambda i: (i, 0)),
                   pl.BlockSpec((tm, ROUTER_W), lambda i: (i, 0))],
        out_shape=[jax.ShapeDtypeStruct((N_TOK, HALF), jnp.uint32),
                   jax.ShapeDtypeStruct((N_TOK, ROUTER_W), I32),
                   jax.ShapeDtypeStruct((N_TOK, ROUTER_W), F32)],
        compiler_params=_cparams(("arbitrary",), 8 * tm * D_MODEL * 4),
        name="norm_router",
    )(x, g.reshape(1, -1), wr, br)


def _unpack_x(xp):
    lo = pltpu.unpack_elementwise(xp, index=0, packed_dtype=BF16, unpacked_dtype=F32)
    hi = pltpu.unpack_elementwise(xp, index=1, packed_dtype=BF16, unpacked_dtype=F32)
    return lo.astype(BF16), hi.astype(BF16)


def _ffn_kernel(sbe_ref, sbb_ref, sbn_ref, sbz_ref, tok_ref, xp_hbm, wg_ref, wu_ref, wd_ref, ys_hbm,
                xbuf_ref, acc_ref, xsem, ysem, pend_ref):
    del sbe_ref
    sb = pl.program_id(0)
    j = pl.program_id(1)
    nj = pl.num_programs(1)
    slot = sb % 2

    @pl.when((sb == 0) & (j == 0))
    def _():
        pend_ref[0] = 0

    def row_copy(s, slot_, r):
        tok = tok_ref[sbb_ref[s] * MOE_BLOCK + r]
        return pltpu.make_async_copy(xp_hbm.at[pl.ds(tok, 1), :],
                                     xbuf_ref.at[slot_, pl.ds(r, 1), :], xsem.at[slot_])

    def start_blocks(s, slot_, lo, hi):
        def block(i, carry):
            def body(r, c):
                row_copy(s, slot_, i * MOE_BLOCK + r).start(priority=GATHER_PRIORITY)
                return c

            return lax.fori_loop(0, MOE_BLOCK, body, carry, unroll=ISSUE_UNROLL)

        lax.fori_loop(lo, jnp.minimum(hi, sbn_ref[s]), block, 0)

    def wait_rows(s, slot_):
        def block(i, carry):
            r0 = pl.multiple_of(i * MOE_BLOCK, MOE_BLOCK)
            pltpu.make_async_copy(xp_hbm.at[pl.ds(0, MOE_BLOCK), :],
                                  xbuf_ref.at[slot_, pl.ds(r0, MOE_BLOCK), :], xsem.at[slot_]).wait()
            return carry

        lax.fori_loop(0, sbn_ref[s], block, 0)

    def out_copy(i, blk):
        r0 = pl.multiple_of(i * MOE_BLOCK, MOE_BLOCK)
        r1 = pl.multiple_of(blk * MOE_BLOCK, MOE_BLOCK)
        return pltpu.make_async_copy(acc_ref.at[pl.ds(r0, MOE_BLOCK), :],
                                     ys_hbm.at[pl.ds(r1, MOE_BLOCK), :], ysem)

    def wait_out():
        def body(i, carry):
            out_copy(0, 0).wait()
            return carry

        lax.fori_loop(0, pend_ref[0], body, 0)
        pend_ref[0] = 0

    per_j = -(-NSUB // (D_EXPERT // TF))

    @pl.when((sb == 0) & (j == 0))
    def _():
        start_blocks(sb, slot, 0, NSUB)

    @pl.when(sb + 1 < pl.num_programs(0))
    def _():
        start_blocks(sb + 1, 1 - slot, j * per_j, (j + 1) * per_j)

    nsub = sbn_ref[sb]
    nzero = sbz_ref[sb]

    def zero_block(i, carry):
        r0 = pl.multiple_of(i * MOE_BLOCK, MOE_BLOCK)
        acc_ref[pl.ds(r0, MOE_BLOCK), :] = jnp.zeros((MOE_BLOCK, D_MODEL), F32)
        return carry

    def start_out(i, carry):
        out_copy(i, sbb_ref[sb] + i).start()
        return carry

    @pl.when(j == 0)
    def _():
        wait_rows(sb, slot)
        wait_out()
        lax.fori_loop(0, jnp.maximum(nsub, nzero), zero_block, 0)
        lax.fori_loop(0, nzero, start_out, 0)
        pend_ref[0] = nzero

    def ffn_rows(r0, rows):
        lo, hi = _unpack_x(xbuf_ref[slot, pl.ds(r0, rows), :])
        g = _dot(lo, wg_ref[:HALF, :]) + _dot(hi, wg_ref[HALF:, :])
        u = _dot(lo, wu_ref[:HALF, :]) + _dot(hi, wu_ref[HALF:, :])
        h = (_silu(g) * u).astype(BF16)
        acc_ref[pl.ds(r0, rows), :] += _dot(h, wd_ref[...])

    def pair(i, carry):
        ffn_rows(pl.multiple_of(i * (2 * MOE_BLOCK), 2 * MOE_BLOCK), 2 * MOE_BLOCK)
        return carry

    lax.fori_loop(0, nsub // 2, pair, 0)

    @pl.when(nsub % 2 == 1)
    def _():
        ffn_rows(pl.multiple_of((nsub - 1) * MOE_BLOCK, MOE_BLOCK), MOE_BLOCK)

    @pl.when((j == nj - 1) & (nsub > 0))
    def _():
        lax.fori_loop(0, nsub, start_out, 0)
        pend_ref[0] = nsub

    @pl.when((sb == pl.num_programs(0) - 1) & (j == nj - 1))
    def _():
        wait_out()


def _moe_ffn(xp, plan, w_gate, w_up, w_down, layer):
    nj = D_EXPERT // TF

    def hidden_slice(sb, j, sbn):
        return jnp.where(sbn[sb] > 0, j, nj - 1)

    up_spec = pl.BlockSpec((None, None, D_MODEL, TF),
                           lambda sb, j, sbe, sbb, sbn, sbz, tok: (layer, sbe[sb], 0, hidden_slice(sb, j, sbn)))
    down_spec = pl.BlockSpec((None, None, TF, D_MODEL),
                             lambda sb, j, sbe, sbb, sbn, sbz, tok: (layer, sbe[sb], hidden_slice(sb, j, sbn), 0))
    return pl.pallas_call(
        _ffn_kernel,
        grid_spec=pltpu.PrefetchScalarGridSpec(
            num_scalar_prefetch=5,
            grid=(N_SB, nj),
            in_specs=[pl.BlockSpec(memory_space=pl.ANY), up_spec, up_spec, down_spec],
            out_specs=pl.BlockSpec(memory_space=pl.ANY),
            scratch_shapes=[pltpu.VMEM((2, SB_ROWS, HALF), jnp.uint32),
                            pltpu.VMEM((SB_ROWS, D_MODEL), F32),
                            pltpu.SemaphoreType.DMA((2,)),
                            pltpu.SemaphoreType.DMA(()),
                            pltpu.SMEM((1,), I32)]),
        out_shape=jax.ShapeDtypeStruct((N_ROWS, D_MODEL), F32),
        compiler_params=_cparams(("arbitrary", "arbitrary"), VMEM_CAP),
        name="moe_ffn",
    )(plan["sb_e"], plan["sb_blk"], plan["sb_n"], plan["sb_z"], plan["row_tok"], xp, w_gate, w_up, w_down)


def _combine_kernel(dest_ref, x_ref, gates_ref, g_ref, ys_hbm, out_a, out_b, buf_ref, sem, *, final):
    i = pl.program_id(0)
    n = pl.num_programs(0)

    def row_copy(step, slot, t, k):
        s = (step * COMB_TM + t) * TOP_K + k
        return pltpu.make_async_copy(ys_hbm.at[pl.ds(dest_ref[s], 1), :],
                                     buf_ref.at[slot, k, pl.ds(t, 1), :], sem.at[slot])

    def start_all(step, slot):
        def body(t, carry):
            for k in range(TOP_K):
                row_copy(step, slot, t, k).start(priority=k % 2)
            return carry

        lax.fori_loop(0, COMB_TM, body, 0, unroll=ISSUE_UNROLL)

    def wait_all(step, slot):
        for k in range(TOP_K):
            pltpu.make_async_copy(ys_hbm.at[pl.ds(0, COMB_TM), :], buf_ref.at[slot, k], sem.at[slot]).wait()

    slot = i % 2

    @pl.when(i == 0)
    def _():
        start_all(i, slot)

    @pl.when(i + 1 < n)
    def _():
        start_all(i + 1, 1 - slot)

    wait_all(i, slot)
    gates = gates_ref[...]
    y = x_ref[...] + gates[:, 0:1] * buf_ref[slot, 0] + gates[:, 1:2] * buf_ref[slot, 1]
    yn = _rms(y, g_ref[...])
    if final:
        @pl.when(i < n - 1)
        def _():
            out_a[...] = yn

        @pl.when(i == n - 1)
        def _():
            out_b[...] = yn
    else:
        out_a[...] = y
        out_b[...] = yn.astype(out_b.dtype)


def _combine(x, ys, dest, gates, g_next, *, final):
    tm = COMB_TM
    assert N_S == tm
    n_p_blocks = N_P // tm
    if final:
        out_specs = [pl.BlockSpec((tm, D_MODEL), lambda i, d: (jnp.minimum(i, n_p_blocks - 1), 0)),
                     pl.BlockSpec((tm, D_MODEL), lambda i, d: (0, 0))]
        out_shape = [jax.ShapeDtypeStruct((N_P, D_MODEL), F32),
                     jax.ShapeDtypeStruct((N_S, D_MODEL), F32)]
    else:
        out_specs = [pl.BlockSpec((tm, D_MODEL), lambda i, d: (i, 0)),
                     pl.BlockSpec((tm, D_MODEL), lambda i, d: (i, 0))]
        out_shape = [jax.ShapeDtypeStruct((N_TOK, D_MODEL), F32),
                     jax.ShapeDtypeStruct((N_TOK, D_MODEL), BF16)]
    return pl.pallas_call(
        functools.partial(_combine_kernel, final=final),
        grid_spec=pltpu.PrefetchScalarGridSpec(
            num_scalar_prefetch=1,
            grid=(N_TOK // tm,),
            in_specs=[pl.BlockSpec((tm, D_MODEL), lambda i, d: (i, 0)),
                      pl.BlockSpec((tm, ROUTER_W), lambda i, d: (i, 0)),
                      pl.BlockSpec((1, D_MODEL), lambda i, d: (0, 0)),
                      pl.BlockSpec(memory_space=pl.ANY)],
            out_specs=out_specs,
            scratch_shapes=[pltpu.VMEM((2, TOP_K, tm, D_MODEL), F32),
                            pltpu.SemaphoreType.DMA((2,))]),
        out_shape=out_shape,
        compiler_params=_cparams(("arbitrary",), 40 << 20),
        name="moe_combine",
    )(dest, x, gates, g_next.reshape(1, -1), ys)


def _dispatch_plan(ids):
    flat_e = ids[:, :TOP_K].reshape(-1)
    onehot = (flat_e[:, None] == jnp.arange(N_EXPERTS, dtype=I32)[None, :]).astype(I32)
    chunks = onehot.reshape(N_SLOT // PLAN_CHUNK, PLAN_CHUNK, N_EXPERTS).astype(BF16)
    tri = jnp.tril(jnp.ones((PLAN_CHUNK, PLAN_CHUNK), BF16))
    within = jnp.einsum("ij,cjk->cik", tri, chunks, preferred_element_type=F32).astype(I32)
    totals = within[:, -1, :]
    before = jnp.cumsum(totals, axis=0) - totals
    csum = (within + before[:, None, :]).reshape(N_SLOT, N_EXPERTS)
    rank = jnp.sum(onehot * csum, axis=1) - 1
    counts = csum[-1]
    nblk = (counts + MOE_BLOCK - 1) // MOE_BLOCK
    bend = jnp.cumsum(nblk)
    bstart = bend - nblk
    dest = (jnp.sum(onehot * bstart[None, :], axis=1) * MOE_BLOCK + rank).astype(I32)
    row_tok = jnp.zeros((N_ROWS,), I32).at[dest].set(jnp.arange(N_SLOT, dtype=I32) // TOP_K)

    nsup = (nblk + NSUB - 1) // NSUB
    sup_end = jnp.cumsum(nsup)
    sup_start = sup_end - nsup
    total_sup = sup_end[-1]
    sb = jnp.arange(N_SB, dtype=I32)
    e_of = jnp.minimum(jnp.sum((sup_end[None, :] <= sb[:, None]).astype(I32), axis=1), N_EXPERTS - 1)
    q = sb - sup_start[e_of]
    real = sb < total_sup
    n_real = jnp.clip(nblk[e_of] - NSUB * q, 0, NSUB)
    blk_real = bstart[e_of] + NSUB * q
    blk_tail = bend[-1] + NSUB * (sb - total_sup)
    n_tail = jnp.clip(N_BLOCKS - blk_tail, 0, NSUB)
    e_last = e_of[total_sup - 1]
    return dict(
        dest=dest, row_tok=row_tok,
        sb_e=jnp.where(real, e_of, e_last).astype(I32),
        sb_blk=jnp.where(real, blk_real, jnp.minimum(blk_tail, N_BLOCKS - 1)).astype(I32),
        sb_n=jnp.where(real, n_real, 0).astype(I32),
        sb_z=jnp.where(real, 0, n_tail).astype(I32))


def kernel(x_prompt, x_sample, mem_prompt, state_conv, state_lru_conv, state_lru_h, cache_mem_k, cache_mem_v, norm_mix, w_in, conv_w, conv_b, conv_gn_g, conv_gn_b, lru_conv_w, lru_conv_b, lru_wa, lru_ba, lru_wi, lru_bi, lru_lambda, w_out, norm_attn, norm_mem_kv, w_q, w_k, w_v, w_o, norm_ffn, w_router_g, b_router_g, w_router_e, b_router_e, w_gate, w_up, w_down, norm_final):
    x = jnp.concatenate([x_prompt.reshape(N_P, D_MODEL), x_sample.reshape(N_S, D_MODEL)], axis=0)
    mem = mem_prompt.reshape(BATCH * MEM_LEN, D_MODEL)
    xn = _norm(x, norm_mix[0], tm=TM_NORM)

    state_conv_t = state_conv.transpose(0, 2, 1, 3)
    state_lru_conv_t = state_lru_conv.transpose(0, 2, 1, 3)
    conv_p, lruc_p, h_p, mk_p, mv_p, conv_s, lruc_s, h_s = ([] for _ in range(8))
    for l in range(DEPTH):
        z = _mm([xn], w_in, l, tn=1024, tm=TM, name="w_in")
        yc, cst = _conv_prompt(z, conv_w[l], conv_b[l], conv_gn_g[l], conv_gn_b[l])
        yl, lst, hl = _lru_prompt(z, lru_conv_w[l], lru_conv_b[l], lru_wa[l], lru_ba[l],
                                  lru_wi[l], lru_bi[l], lru_lambda[l])
        yc, yl, cst_s, lst_s, hl_s = _mixer_sample(
            z, state_conv_t, state_lru_conv_t, state_lru_h, l, yc, yl,
            conv_w[l], conv_b[l], conv_gn_g[l], conv_gn_b[l], lru_conv_w[l], lru_conv_b[l],
            lru_wa[l], lru_ba[l], lru_wi[l], lru_bi[l], lru_lambda[l])
        x = _mm([yc, yl], w_out, l, tn=1024, tm=TM_NORM, res=x, name="w_out")
        conv_p.append(cst); lruc_p.append(lst); h_p.append(hl.reshape(BATCH, C_LRU))
        conv_s.append(cst_s); lruc_s.append(lst_s); h_s.append(hl_s)

        q = _norm_mm(x, norm_attn[l], w_q, l, tm=TM_NORM, name="w_q")
        mn = _norm(mem, norm_mem_kv[l], tm=256)
        k_p = _mm([mn], w_k, l, tn=512, tm=BATCH * MEM_LEN, name="w_k")
        v_p = _mm([mn], w_v, l, tn=512, tm=BATCH * MEM_LEN, name="w_v")
        o = _attn_prompt(q, k_p, v_p)
        o = _attn_sample(q[N_P:].reshape(N_S, MEM_HEADS, MEM_HD), cache_mem_k, cache_mem_v, l, o)
        x = _mm([o], w_o, l, tn=1024, tm=TM, res=x, name="w_o")
        mk_p.append(k_p.reshape(BATCH, MEM_LEN, MEM_HEADS, MEM_HD))
        mv_p.append(v_p.reshape(BATCH, MEM_LEN, MEM_HEADS, MEM_HD))

        wr = jnp.concatenate([w_router_g[l], w_router_e[l].reshape(D_MODEL, N_EXPERTS),
                              jnp.zeros((D_MODEL, ROUTER_W - N_GROUPS - N_EXPERTS), F32)],
                             axis=1).astype(BF16)
        br = jnp.concatenate([b_router_g[l], b_router_e[l].reshape(N_EXPERTS),
                              jnp.zeros((ROUTER_W - N_GROUPS - N_EXPERTS,), F32)]).reshape(1, ROUTER_W)
        xp, ids, gates = _norm_router(x, norm_ffn[l], wr, br)
        plan = _dispatch_plan(ids)
        ys = _moe_ffn(xp, plan, w_gate, w_up, w_down, l)
        if l < DEPTH - 1:
            x, xn = _combine(x, ys, plan["dest"], gates, norm_mix[l + 1], final=False)
        else:
            y_p, y_s = _combine(x, ys, plan["dest"], gates, norm_final, final=True)

    y_prompt = y_p.reshape(BATCH, SEQ, D_MODEL)
    y_sample = y_s.reshape(DEC_BATCH, 1, D_MODEL)
    return (y_prompt, y_sample, jnp.stack(conv_p), jnp.stack(lruc_p), jnp.stack(h_p),
            jnp.stack(mk_p), jnp.stack(mv_p), jnp.stack(conv_s).transpose(0, 2, 1, 3),
            jnp.stack(lruc_s).transpose(0, 2, 1, 3), jnp.stack(h_s))
```
